```python
import math
import jax, jax.numpy as jnp
from jax import lax
import numpy as np

D_MODEL = 4096
BATCH = 2
SEQ = 4096
DEPTH = 2

N_EVEN = (DEPTH + 1) // 2
N_ODD = DEPTH // 2
ROPE_THETA = 10000.0
NORM_EPS = 1e-6
MOBA_HEADS = 16
MOBA_HEAD_DIM = 128
MOBA_BLOCK = 256
MOBA_TOPK = 3
MOBA_Q_CHUNK = 32
RET_HEADS = 8
RET_KEY_DIM = 256
RET_VAL_DIM = 512
RET_CHUNK = 128
CONV_CH = D_MODEL // 2
CONV_WIDTH = 31
RWKV_HEAD_DIM = 64
RWKV_DIM = D_MODEL // 2
RWKV_HEADS = RWKV_DIM // RWKV_HEAD_DIM
DECAY_LORA = max(32, int(round(1.8 * RWKV_DIM ** 0.5 / 32)) * 32)
ICLR_LORA = max(32, int(round(1.8 * RWKV_DIM ** 0.5 / 32)) * 32)
GATE_LORA = max(32, int(round(0.6 * RWKV_DIM ** 0.8 / 32)) * 32)
RWKV_LNX_EPS = 64e-5
FFN_HIDDEN = -(-8 * D_MODEL // (3 * 256)) * 256
EVEN_IN = 3 * MOBA_HEADS * MOBA_HEAD_DIM + 2 * RET_HEADS * RET_KEY_DIM + 2 * RET_HEADS * RET_VAL_DIM
EVEN_OUT = MOBA_HEADS * MOBA_HEAD_DIM + RET_HEADS * RET_VAL_DIM
RWKV_IN = 3 * RWKV_DIM + DECAY_LORA + ICLR_LORA + GATE_LORA
ODD_IN = 2 * CONV_CH + RWKV_IN
ODD_OUT = CONV_CH + RWKV_DIM

kernel_name = 'hybrid_moba_retention_conformer_rwkv7_block'


def split_cols(t, sizes):
    offs = np.cumsum(sizes)[:-1].tolist()
    return jnp.split(t, offs, axis=-1)


def rms_norm(x, g):
    xf = x.astype(jnp.float32)
    y = xf * lax.rsqrt(jnp.mean(xf * xf, axis=-1, keepdims=True) + NORM_EPS)
    return (y * g.astype(jnp.float32)).astype(x.dtype)


def layer_norm(x, g, b, eps):
    xf = x.astype(jnp.float32)
    mu = jnp.mean(xf, axis=-1, keepdims=True)
    var = jnp.mean(jnp.square(xf - mu), axis=-1, keepdims=True)
    return (xf - mu) * lax.rsqrt(var + eps) * g.astype(jnp.float32) + b.astype(jnp.float32)


def modulation(c, w_ada, b_ada):
    m = jax.nn.silu(c) @ w_ada + b_ada
    return [t[:, None, :] for t in jnp.split(m, 6, axis=-1)]


def rope_tables(seq, dim):
    inv = 1.0 / (ROPE_THETA ** (jnp.arange(0, dim, 2, dtype=jnp.float32) / dim))
    ang = jnp.arange(seq, dtype=jnp.float32)[:, None] * inv[None, :]
    return jnp.cos(ang), jnp.sin(ang)


def apply_rope(x, cos, sin):
    half = x.shape[-1] // 2
    x1, x2 = x[..., :half], x[..., half:]
    return jnp.concatenate([x1 * cos - x2 * sin, x2 * cos + x1 * sin], axis=-1).astype(x.dtype)


def to_heads(t, n_heads):
    B, S, _ = t.shape
    return t.reshape(B, S, n_heads, -1).transpose(0, 2, 1, 3)


def from_heads(t):
    B, H, S, Dh = t.shape
    return t.transpose(0, 2, 1, 3).reshape(B, S, H * Dh)


def moba_attention(q, k, v):
    B, H, S, Dh = q.shape
    nb = -(-S // MOBA_BLOCK)
    pad = nb * MOBA_BLOCK - S
    kp = jnp.pad(k, ((0, 0), (0, 0), (0, pad), (0, 0)))
    vp = jnp.pad(v, ((0, 0), (0, 0), (0, pad), (0, 0)))
    k_blocks = kp.reshape(B, H, nb, MOBA_BLOCK, Dh)
    v_blocks = vp.reshape(B, H, nb, MOBA_BLOCK, Dh)
    k_mean = jnp.mean(k_blocks.astype(jnp.float32), axis=3)
    topk = min(MOBA_TOPK, nb)
    n_sel = topk * MOBA_BLOCK
    scale = Dh ** -0.5
    n_chunks = S // MOBA_Q_CHUNK
    q_chunks = q.reshape(B, H, n_chunks, MOBA_Q_CHUNK, Dh).transpose(2, 0, 1, 3, 4)
    gather = jax.vmap(jax.vmap(lambda blocks, idx: blocks[idx]))

    def attend_chunk(args):
        ci, qc = args
        start = ci * MOBA_Q_CHUNK
        own = start // MOBA_BLOCK
        q_pos = start + jnp.arange(MOBA_Q_CHUNK)
        gate = jnp.einsum('bhqd,bhnd->bhqn', qc.astype(jnp.float32), k_mean)
        gate = jnp.where(jnp.arange(nb) < own, gate, -jnp.inf)
        g_val, g_idx = lax.top_k(gate, topk)
        sel_ok = jnp.isfinite(g_val)
        k_sel = gather(k_blocks, g_idx)
        v_sel = gather(v_blocks, g_idx)
        s_sel = jnp.einsum('bhqd,bhqtkd->bhqtk', qc, k_sel).astype(jnp.float32) * scale
        s_sel = jnp.where(sel_ok[..., None], s_sel, -jnp.inf)
        k_own = lax.dynamic_slice_in_dim(kp, own * MOBA_BLOCK, MOBA_BLOCK, axis=2)
        v_own = lax.dynamic_slice_in_dim(vp, own * MOBA_BLOCK, MOBA_BLOCK, axis=2)
        s_own = jnp.einsum('bhqd,bhkd->bhqk', qc, k_own).astype(jnp.float32) * scale
        own_pos = own * MOBA_BLOCK + jnp.arange(MOBA_BLOCK)
        s_own = jnp.where(own_pos[None, :] <= q_pos[:, None], s_own, -jnp.inf)
        scores = jnp.concatenate([s_sel.reshape(B, H, MOBA_Q_CHUNK, n_sel), s_own], axis=-1)
        p = jax.nn.softmax(scores, axis=-1)
        p_sel = p[..., :n_sel].reshape(B, H, MOBA_Q_CHUNK, topk, MOBA_BLOCK)
        out = (jnp.einsum('bhqtk,bhqtkd->bhqd', p_sel, v_sel.astype(jnp.float32))
               + jnp.einsum('bhqk,bhkd->bhqd', p[..., n_sel:], v_own.astype(jnp.float32)))
        return out.astype(q.dtype)

    out = lax.map(attend_chunk, (jnp.arange(n_chunks), q_chunks))
    return out.transpose(1, 2, 0, 3, 4).reshape(B, H, S, Dh)


def retention(q, k, v):
    B, H, S, Dk = q.shape
    Dv = v.shape[-1]
    C = RET_CHUNK
    nc = S // C
    log_g = jnp.log1p(-jnp.exp2(-5.0 - jnp.arange(H, dtype=jnp.float32)))
    idx = jnp.arange(C, dtype=jnp.float32)
    diff = idx[:, None] - idx[None, :]
    decay_mask = jnp.where(diff >= 0, jnp.exp(jnp.maximum(diff, 0.0) * log_g[:, None, None]), 0.0)
    q_decay = jnp.exp((idx + 1.0) * log_g[:, None])[..., None]
    k_decay = jnp.exp((C - 1.0 - idx) * log_g[:, None])[..., None]
    chunk_decay = jnp.exp(C * log_g)[:, None, None]
    qf = q.astype(jnp.float32)
    kf = k.astype(jnp.float32) * (Dk ** -0.5)
    vf = v.astype(jnp.float32)
    chunks = lambda t: t.reshape(B, H, nc, C, t.shape[-1]).transpose(2, 0, 1, 3, 4)

    def step(state, xs):
        qc, kc, vc = xs
        inner = jnp.einsum('bhid,bhjd->bhij', qc, kc) * decay_mask
        o = (jnp.einsum('bhij,bhjv->bhiv', inner, vc)
             + jnp.einsum('bhid,bhdv->bhiv', qc, state) * q_decay)
        state = state * chunk_decay + jnp.einsum('bhjd,bhjv->bhdv', kc * k_decay, vc)
        return state, o

    state0 = jnp.zeros((B, H, Dk, Dv), jnp.float32)
    _, o = lax.scan(step, state0, (chunks(qf), chunks(kf), chunks(vf)))
    return o.transpose(1, 2, 0, 3, 4).reshape(B, H, S, Dv)


def even_mixer(h, w_in, w_out, rope_m, rope_r):
    dm = MOBA_HEADS * MOBA_HEAD_DIM
    dk = RET_HEADS * RET_KEY_DIM
    dv = RET_HEADS * RET_VAL_DIM
    q_m, k_m, v_m, q_r, k_r, v_r, g_r = split_cols(h @ w_in, [dm, dm, dm, dk, dk, dv, dv])
    o_m = moba_attention(apply_rope(to_heads(q_m, MOBA_HEADS), *rope_m),
                         apply_rope(to_heads(k_m, MOBA_HEADS), *rope_m),
                         to_heads(v_m, MOBA_HEADS))
    o_r = retention(apply_rope(to_heads(q_r, RET_HEADS), *rope_r),
                    apply_rope(to_heads(k_r, RET_HEADS), *rope_r),
                    to_heads(v_r, RET_HEADS))
    o_r = o_r * lax.rsqrt(jnp.mean(o_r * o_r, axis=-1, keepdims=True) + NORM_EPS)
    o_r = from_heads(o_r) * jax.nn.silu(g_r.astype(jnp.float32))
    merged = jnp.concatenate([from_heads(o_m).astype(jnp.float32), o_r], axis=-1)
    return (merged.astype(h.dtype) @ w_out).astype(h.dtype)


def causal_depthwise_conv(u, w, b):
    K, C = w.shape
    out = lax.conv_general_dilated(u, w[:, None, :].astype(u.dtype), window_strides=(1,),
                                   padding=[(K - 1, 0)], dimension_numbers=('NWC', 'WIO', 'NWC'),
                                   feature_group_count=C)
    return out + b


def token_shift(t, mu):
    prev = jnp.pad(t, ((0, 0), (1, 0), (0, 0)))[:, :-1]
    return t + (prev - t) * mu


def rwkv7_scan(r, w, k, v, a, b):
    B, S, H, N = r.shape
    xs = tuple(t.astype(jnp.float32).transpose(1, 0, 2, 3) for t in (r, w, k, v, a, b))

    def step(state, inp):
        r_t, w_t, k_t, v_t, a_t, b_t = inp
        sa = jnp.einsum('bhvk,bhk->bhv', state, a_t)
        state = (state * w_t[:, :, None, :] + sa[..., None] * b_t[:, :, None, :]
                 + v_t[..., None] * k_t[:, :, None, :])
        return state, jnp.einsum('bhvk,bhk->bhv', state, r_t)

    state0 = jnp.zeros((B, H, N, N), jnp.float32)
    _, y = lax.scan(step, state0, xs)
    return y.transpose(1, 0, 2, 3)


def odd_mixer(h, w_in, w_out, conv_w, conv_b, conv_ln_g, conv_ln_b, rwkv_mu, rwkv_w0, rwkv_w_up,
              rwkv_a0, rwkv_a_up, rwkv_g_up, rwkv_k_k, rwkv_k_a, rwkv_r_k, rwkv_lnx_g, rwkv_lnx_b):
    B, S, _ = h.shape
    conv_a, conv_gate, rw = split_cols(h @ w_in, [CONV_CH, CONV_CH, RWKV_IN])
    u = conv_a * jax.nn.sigmoid(conv_gate)
    u = causal_depthwise_conv(u, conv_w, conv_b)
    u = jax.nn.silu(layer_norm(u, conv_ln_g, conv_ln_b, 1e-5))
    rw = token_shift(rw, rwkv_mu)
    r, k, v, xw, xa, xg = split_cols(rw, [RWKV_DIM, RWKV_DIM, RWKV_DIM, DECAY_LORA, ICLR_LORA, GATE_LORA])
    w_log = -jax.nn.softplus(-(rwkv_w0 + jnp.tanh(xw) @ rwkv_w_up)) - 0.5
    decay = jnp.exp(-jnp.exp(w_log.astype(jnp.float32)))
    a = jax.nn.sigmoid(rwkv_a0 + xa @ rwkv_a_up)
    g = jax.nn.sigmoid(xg) @ rwkv_g_up
    heads = lambda t: t.reshape(B, S, RWKV_HEADS, RWKV_HEAD_DIM)
    kk = heads((k * rwkv_k_k).astype(jnp.float32))
    kk = kk / jnp.maximum(jnp.sqrt(jnp.sum(kk * kk, axis=-1, keepdims=True)), 1e-12)
    k = k * (1.0 + (a - 1.0) * rwkv_k_a)
    rh, kh, vh, ah = heads(r), heads(k), heads(v), heads(a)
    y = rwkv7_scan(rh, heads(decay), kh, vh, -kk, kk * ah)
    y = layer_norm(y, rwkv_lnx_g.reshape(RWKV_HEADS, RWKV_HEAD_DIM),
                   rwkv_lnx_b.reshape(RWKV_HEADS, RWKV_HEAD_DIM), RWKV_LNX_EPS)
    bonus = jnp.sum((rh * kh * rwkv_r_k).astype(jnp.float32), axis=-1, keepdims=True) * vh
    y = (y + bonus).reshape(B, S, RWKV_DIM) * g
    merged = jnp.concatenate([u, y.astype(jnp.float32)], axis=-1)
    return (merged.astype(h.dtype) @ w_out).astype(h.dtype)


def swiglu(h, w_in, w_out):
    gate, up = jnp.split(h @ w_in, 2, axis=-1)
    return (jax.nn.silu(gate) * up) @ w_out


def setup_inputs(seed: int = 0) -> dict:
    key = jax.random.key(seed)
    keys = iter(jax.random.split(key, 32))
    f32 = jnp.float32

    def nrm(shape, scale):
        return jax.random.normal(next(keys), shape, f32) * scale

    def unif(shape, lo, hi):
        return jax.random.uniform(next(keys), shape, f32, lo, hi)

    D = D_MODEL
    return {
        'x': nrm((BATCH, SEQ, D), 1.0),
        'c': nrm((BATCH, D), 1.0),
        'w_ada': nrm((DEPTH, D, 6 * D), 0.5 * D ** -0.5),
        'b_ada': nrm((DEPTH, 6 * D), 0.02),
        'norm_g': 1.0 + nrm((DEPTH, 4, D), 0.02),
        'w_ffn_in': nrm((DEPTH, D, 2 * FFN_HIDDEN), D ** -0.5),
        'w_ffn_out': nrm((DEPTH, FFN_HIDDEN, D), FFN_HIDDEN ** -0.5),
        'even_w_in': nrm((N_EVEN, D, EVEN_IN), D ** -0.5),
        'even_w_out': nrm((N_EVEN, EVEN_OUT, D), EVEN_OUT ** -0.5),
        'odd_w_in': nrm((N_ODD, D, ODD_IN), D ** -0.5),
        'odd_w_out': nrm((N_ODD, ODD_OUT, D), ODD_OUT ** -0.5),
        'conv_w': nrm((N_ODD, CONV_WIDTH, CONV_CH), CONV_WIDTH ** -0.5),
        'conv_b': nrm((N_ODD, CONV_CH), 0.02),
        'conv_ln_g': 1.0 + nrm((N_ODD, CONV_CH), 0.02),
        'conv_ln_b': nrm((N_ODD, CONV_CH), 0.02),
        'rwkv_mu': unif((N_ODD, RWKV_IN), 0.0, 1.0),
        'rwkv_w0': unif((N_ODD, RWKV_DIM), -6.0, -0.5),
        'rwkv_w_up': nrm((N_ODD, DECAY_LORA, RWKV_DIM), 0.5 * DECAY_LORA ** -0.5),
        'rwkv_a0': nrm((N_ODD, RWKV_DIM), 0.5),
        'rwkv_a_up': nrm((N_ODD, ICLR_LORA, RWKV_DIM), ICLR_LORA ** -0.5),
        'rwkv_g_up': nrm((N_ODD, GATE_LORA, RWKV_DIM), GATE_LORA ** -0.5),
        'rwkv_k_k': 0.85 + nrm((N_ODD, RWKV_DIM), 0.05),
        'rwkv_k_a': 1.0 + nrm((N_ODD, RWKV_DIM), 0.05),
        'rwkv_r_k': nrm((N_ODD, RWKV_HEADS, RWKV_HEAD_DIM), 0.1),
        'rwkv_lnx_g': 1.0 + nrm((N_ODD, RWKV_DIM), 0.02),
        'rwkv_lnx_b': nrm((N_ODD, RWKV_DIM), 0.02),
    }


def reference(x, c, w_ada, b_ada, norm_g, w_ffn_in, w_ffn_out, even_w_in, even_w_out, odd_w_in,
              odd_w_out, conv_w, conv_b, conv_ln_g, conv_ln_b, rwkv_mu, rwkv_w0, rwkv_w_up, rwkv_a0,
              rwkv_a_up, rwkv_g_up, rwkv_k_k, rwkv_k_a, rwkv_r_k, rwkv_lnx_g, rwkv_lnx_b):
    S = x.shape[1]
    rope_m = rope_tables(S, MOBA_HEAD_DIM)
    rope_r = rope_tables(S, RET_KEY_DIM)
    for layer in range(DEPTH):
        sh_m, sc_m, g_m, sh_f, sc_f, g_f = modulation(c, w_ada[layer], b_ada[layer])
        h = rms_norm(x, norm_g[layer, 0]) * (1.0 + sc_m) + sh_m
        j = layer // 2
        if layer % 2 == 0:
            o = even_mixer(h, even_w_in[j], even_w_out[j], rope_m, rope_r)
        else:
            o = odd_mixer(h, odd_w_in[j], odd_w_out[j], conv_w[j], conv_b[j], conv_ln_g[j],
                          conv_ln_b[j], rwkv_mu[j], rwkv_w0[j], rwkv_w_up[j], rwkv_a0[j],
                          rwkv_a_up[j], rwkv_g_up[j], rwkv_k_k[j], rwkv_k_a[j], rwkv_r_k[j],
                          rwkv_lnx_g[j], rwkv_lnx_b[j])
        x = x + g_m * rms_norm(o, norm_g[layer, 1])
        h = rms_norm(x, norm_g[layer, 2]) * (1.0 + sc_f) + sh_f
        x = x + g_f * rms_norm(swiglu(h, w_ffn_in[layer], w_ffn_out[layer]), norm_g[layer, 3])
    return x
```

```python
import dataclasses
import functools
import math

import jax
import jax.numpy as jnp
from jax import lax
from jax.experimental import pallas as pl
from jax.experimental.pallas import tpu as pltpu

F32 = jnp.float32
BF16 = jnp.bfloat16
HIGHEST = lax.Precision.HIGHEST

V7X_LANES = 128
V7X_SUBLANES = 8
MIB = 1024 * 1024
NORM_EPS = 1e-6
ROPE_THETA = 10000.0
CONV_LN_EPS = 1e-5
RWKV_LNX_EPS = 64e-5


@dataclasses.dataclass(frozen=True)
class Config:
    d_model: int = 4096
    moba_heads: int = 16
    moba_head_dim: int = 128
    moba_block: int = 256
    moba_topk: int = 3
    ret_heads: int = 8
    ret_key_dim: int = 256
    ret_val_dim: int = 512
    ret_chunk: int = 128
    conv_ch: int = 2048
    conv_width: int = 31
    rwkv_dim: int = 2048
    rwkv_head_dim: int = 64
    decay_lora: int = 96
    iclr_lora: int = 96
    gate_lora: int = 256
    ffn_hidden: int = 11008
    row_tile: int = 256
    mm_tm: int = 1024
    mm_tn: int = 512
    ffn_tn: int = 256
    ffn_out_tm: int = 512
    ret_rows: int = 512
    conv_rows: int = 256
    conv_cols: int = 256
    scan_rows: int = 512
    scan_chunk: int = 64
    scan_pairs: int = 2
    vmem_mib: int = 56

    @property
    def dm(self):
        return self.moba_heads * self.moba_head_dim

    @property
    def dk(self):
        return self.ret_heads * self.ret_key_dim

    @property
    def dv(self):
        return self.ret_heads * self.ret_val_dim

    @property
    def even_in(self):
        return 3 * self.dm + 2 * self.dk + 2 * self.dv

    @property
    def lora_in(self):
        return self.decay_lora + self.iclr_lora + self.gate_lora

    @property
    def odd_main(self):
        return 2 * self.conv_ch + 3 * self.rwkv_dim


def _cparams(cfg, *sem):
    return pltpu.CompilerParams(dimension_semantics=sem, vmem_limit_bytes=cfg.vmem_mib * MIB)


def _silu(x):
    return x * jax.nn.sigmoid(x)


def _dot(a, b, **kw):
    return jnp.dot(a, b, preferred_element_type=F32, **kw)


def _dot_nt(a, b, **kw):
    return lax.dot_general(a, b, (((1,), (1,)), ((), ())), preferred_element_type=F32, **kw)


def _dot_tn(a, b, **kw):
    return lax.dot_general(a, b, (((0,), (0,)), ((), ())), preferred_element_type=F32, **kw)


def _ada_kernel(c_ref, w_ref, b_ref, o_ref):
    s = _silu(c_ref[...])
    o_ref[...] = _dot(s, w_ref[...], precision=HIGHEST) + b_ref[...]


def _modulation(cfg, c, w_ada, b_ada, tn=512):
    depth, d, n = w_ada.shape
    bsz = c.shape[0]
    cp = jnp.zeros((V7X_SUBLANES, d), F32).at[:bsz].set(c)
    out = pl.pallas_call(
        _ada_kernel,
        grid=(depth, n // tn),
        in_specs=[
            pl.BlockSpec((V7X_SUBLANES, d), lambda l, j: (0, 0)),
            pl.BlockSpec((None, d, tn), lambda l, j: (l, 0, j)),
            pl.BlockSpec((None, 1, tn), lambda l, j: (l, 0, j)),
        ],
        out_specs=pl.BlockSpec((None, V7X_SUBLANES, tn), lambda l, j: (l, 0, j)),
        out_shape=jax.ShapeDtypeStruct((depth, V7X_SUBLANES, n), F32),
        compiler_params=_cparams(cfg, "parallel", "parallel"),
        name="adaln_modulation",
    )(cp, w_ada, b_ada.reshape(depth, 1, n))
    return out[:, :bsz].reshape(depth, bsz, 6, 1, d)


def _rms(x, g):
    return x * lax.rsqrt(jnp.mean(x * x, axis=-1, keepdims=True) + NORM_EPS) * g


def _norm_mod_kernel(x_ref, g_ref, sc_ref, sh_ref, o_ref):
    y = _rms(x_ref[...], g_ref[...])
    o_ref[...] = (y * (1.0 + sc_ref[...]) + sh_ref[...]).astype(o_ref.dtype)


def _norm_mod(cfg, x, g, sc, sh):
    bsz, s, d = x.shape
    ts = min(cfg.row_tile, s)
    row = pl.BlockSpec((None, ts, d), lambda b, i: (b, i, 0))
    vec = pl.BlockSpec((1, d), lambda b, i: (0, 0))
    mod = pl.BlockSpec((None, 1, d), lambda b, i: (b, 0, 0))
    return pl.pallas_call(
        _norm_mod_kernel,
        grid=(bsz, s // ts),
        in_specs=[row, vec, mod, mod],
        out_specs=row,
        out_shape=jax.ShapeDtypeStruct((bsz, s, d), BF16),
        compiler_params=_cparams(cfg, "parallel", "parallel"),
        name="norm_modulate",
    )(x, g.reshape(1, d), sc, sh)


def _resid_kernel(x_ref, o_ref, ga_ref, gate_ref, *rest, with_h):
    xn = x_ref[...] + gate_ref[...] * _rms(o_ref[...], ga_ref[...])
    if with_h:
        gb_ref, sc_ref, sh_ref, xn_ref, h_ref = rest
        xn_ref[...] = xn
        h_ref[...] = (_rms(xn, gb_ref[...]) * (1.0 + sc_ref[...]) + sh_ref[...]).astype(h_ref.dtype)
    else:
        (xn_ref,) = rest
        xn_ref[...] = xn


def _resid(cfg, x, o, ga, gate, nxt=None):
    bsz, s, d = x.shape
    ts = min(cfg.row_tile, s)
    row = pl.BlockSpec((None, ts, d), lambda b, i: (b, i, 0))
    vec = pl.BlockSpec((1, d), lambda b, i: (0, 0))
    mod = pl.BlockSpec((None, 1, d), lambda b, i: (b, 0, 0))
    with_h = nxt is not None
    in_specs = [row, row, vec, mod]
    args = [x, o, ga.reshape(1, d), gate]
    out_specs = [row]
    out_shape = [jax.ShapeDtypeStruct((bsz, s, d), F32)]
    if with_h:
        gb, sc, sh = nxt
        in_specs += [vec, mod, mod]
        args += [gb.reshape(1, d), sc, sh]
        out_specs.append(row)
        out_shape.append(jax.ShapeDtypeStruct((bsz, s, d), BF16))
    outs = pl.pallas_call(
        functools.partial(_resid_kernel, with_h=with_h),
        grid=(bsz, s // ts),
        in_specs=in_specs,
        out_specs=out_specs,
        out_shape=out_shape,
        compiler_params=_cparams(cfg, "parallel", "parallel"),
        name="residual_norm",
    )(*args)
    return outs if with_h else outs[0]


def _mm_kernel(x_ref, w_ref, o_ref, *acc, nk):
    part = _dot(x_ref[...], w_ref[...].astype(BF16))
    if nk == 1:
        o_ref[...] = part.astype(o_ref.dtype)
        return
    (acc_ref,) = acc
    k = pl.program_id(2)

    @pl.when(k == 0)
    def _():
        acc_ref[...] = part

    @pl.when(k > 0)
    def _():
        acc_ref[...] += part

    @pl.when(k == nk - 1)
    def _():
        o_ref[...] = acc_ref[...].astype(o_ref.dtype)


def _matmul(cfg, x, w, layer, col0, n, out_dtype, tm, tn, tk):
    m, kdim = x.shape
    tm, tn, tk = min(tm, m), min(tn, n), min(tk, kdim)
    nk = kdim // tk
    cb = col0 // tn
    assert m % tm == 0 and n % tn == 0 and kdim % tk == 0 and col0 % tn == 0
    scratch = [] if nk == 1 else [pltpu.VMEM((tm, tn), F32)]
    return pl.pallas_call(
        functools.partial(_mm_kernel, nk=nk),
        grid=(m // tm, n // tn, nk),
        in_specs=[
            pl.BlockSpec((tm, tk), lambda i, j, k: (i, k)),
            pl.BlockSpec((None, tk, tn), lambda i, j, k: (layer, k, cb + j)),
        ],
        out_specs=pl.BlockSpec((tm, tn), lambda i, j, k: (i, j)),
        out_shape=jax.ShapeDtypeStruct((m, n), out_dtype),
        scratch_shapes=scratch,
        compiler_params=_cparams(cfg, "parallel", "parallel", "arbitrary"),
        name="matmul",
    )(x, w)


def _ffn_in_kernel(x_ref, wg_ref, wu_ref, o_ref):
    x = x_ref[...]
    gate = _dot(x, wg_ref[...].astype(BF16))
    up = _dot(x, wu_ref[...].astype(BF16))
    o_ref[...] = (_silu(gate) * up).astype(o_ref.dtype)


def _ffn_in(cfg, x, w, layer):
    m, kdim = x.shape
    hid = w.shape[2] // 2
    tm, tn = min(cfg.mm_tm, m), min(cfg.ffn_tn, hid)
    nt = hid // tn
    assert m % tm == 0 and hid % tn == 0
    return pl.pallas_call(
        _ffn_in_kernel,
        grid=(m // tm, nt),
        in_specs=[
            pl.BlockSpec((tm, kdim), lambda i, j: (i, 0)),
            pl.BlockSpec((None, kdim, tn), lambda i, j: (layer, 0, j)),
            pl.BlockSpec((None, kdim, tn), lambda i, j: (layer, 0, nt + j)),
        ],
        out_specs=pl.BlockSpec((tm, tn), lambda i, j: (i, j)),
        out_shape=jax.ShapeDtypeStruct((m, hid), BF16),
        compiler_params=_cparams(cfg, "parallel", "parallel"),
        name="ffn_in_swiglu",
    )(x, w, w)


def _moba_kernel(q_ref, k_ref, v_ref, cq_ref, sq_ref, ck_ref, sk_ref, o_ref,
                 kr_ref, vt_ref, km_ref, sel_ref, *, nb, blk, dh, topk):
    qi = pl.program_id(2)
    half = dh // 2

    @pl.when(qi == 0)
    def _():
        for j in range(nb):
            rows = slice(j * blk, (j + 1) * blk)
            kb = k_ref[rows, :]
            kr = kb * ck_ref[rows, :] + pltpu.roll(kb, half, 1) * sk_ref[rows, :]
            km_ref[j:j + 1, :] = jnp.mean(kr, axis=0, keepdims=True)
            kr_ref[j] = kr.astype(BF16)
            vt_ref[j] = v_ref[rows, :].T.astype(BF16)

    q = q_ref[...]
    qr = q * cq_ref[...] + pltpu.roll(q, half, 1) * sq_ref[...]

    gate = _dot_nt(km_ref[...], qr, precision=HIGHEST)
    brow = lax.broadcasted_iota(jnp.int32, (nb, blk), 0)
    gm = jnp.where(brow < qi, gate, -jnp.inf)
    for j in range(nb):
        gj = gm[j:j + 1, :]
        gt = jnp.where(gm > gj, 1.0, 0.0)
        eq = jnp.where(gm == gj, jnp.where(brow < j, 1.0, 0.0), 0.0)
        cnt = jnp.sum(gt + eq, axis=0, keepdims=True)
        keep = jnp.where(cnt < topk, 1.0, 0.0)
        keep = jnp.where(j < qi, keep, 0.0)
        sel_ref[j] = jnp.broadcast_to(keep, (V7X_SUBLANES, blk))

    qs = (qr * (dh ** -0.5)).T.astype(BF16)
    kpos = lax.broadcasted_iota(jnp.int32, (blk, blk), 0)
    qpos = lax.broadcasted_iota(jnp.int32, (blk, blk), 1)
    s = _dot(kr_ref[qi], qs)
    s = jnp.where(kpos <= qpos, s, -jnp.inf)
    m = jnp.max(s, axis=0, keepdims=True)
    p = jnp.exp(s - m)
    l = jnp.sum(p, axis=0, keepdims=True)
    acc = _dot(vt_ref[qi], p.astype(BF16))

    def body(j, carry):
        m, l, acc = carry
        s = _dot(kr_ref[j], qs)
        keep = sel_ref[j][0:1, :]
        s = jnp.where(keep > 0.0, s, -jnp.inf)
        m_new = jnp.maximum(m, jnp.max(s, axis=0, keepdims=True))
        alpha = jnp.exp(m - m_new)
        p = jnp.exp(s - m_new)
        l = alpha * l + jnp.sum(p, axis=0, keepdims=True)
        acc = alpha * acc + _dot(vt_ref[j], p.astype(BF16))
        return m_new, l, acc

    m, l, acc = lax.fori_loop(0, qi, body, (m, l, acc))
    o_ref[...] = (acc / l).T.astype(o_ref.dtype)


def _rope_tables(seq, dim):
    inv = 1.0 / (ROPE_THETA ** (jnp.arange(0, dim, 2, dtype=F32) / dim))
    ang = jnp.arange(seq, dtype=F32)[:, None] * inv[None, :]
    return jnp.cos(ang), jnp.sin(ang)


def _moba(cfg, proj):
    bsz, s, _ = proj.shape
    h, dh, blk = cfg.moba_heads, cfg.moba_head_dim, cfg.moba_block
    assert dh == V7X_LANES and s % blk == 0
    nb = s // blk
    cos, sin = _rope_tables(s, dh)
    cosf = jnp.concatenate([cos, cos], axis=1)
    sinf = jnp.concatenate([-sin, sin], axis=1)
    qspec = pl.BlockSpec((None, blk, dh), lambda b, hh, i: (b, i, hh))
    kspec = pl.BlockSpec((None, s, dh), lambda b, hh, i: (b, 0, h + hh))
    vspec = pl.BlockSpec((None, s, dh), lambda b, hh, i: (b, 0, 2 * h + hh))
    tq = pl.BlockSpec((blk, dh), lambda b, hh, i: (i, 0))
    tk = pl.BlockSpec((s, dh), lambda b, hh, i: (0, 0))
    return pl.pallas_call(
        functools.partial(_moba_kernel, nb=nb, blk=blk, dh=dh, topk=cfg.moba_topk),
        grid=(bsz, h, nb),
        in_specs=[qspec, kspec, vspec, tq, tq, tk, tk],
        out_specs=pl.BlockSpec((None, blk, dh), lambda b, hh, i: (b, i, hh)),
        out_shape=jax.ShapeDtypeStruct((bsz, s, h * dh), BF16),
        scratch_shapes=[
            pltpu.VMEM((nb, blk, dh), BF16),
            pltpu.VMEM((nb, dh, blk), BF16),
            pltpu.VMEM((nb, dh), F32),
            pltpu.VMEM((nb, V7X_SUBLANES, blk), F32),
        ],
        compiler_params=_cparams(cfg, "parallel", "parallel", "arbitrary"),
        name="moba_attention",
    )(proj, proj, proj, cosf, sinf, cosf, sinf)


def _ret_kernel(q_ref, k_ref, v_ref, g_ref, cos_ref, sin_ref, dm_ref, qd_ref, kd_ref, cd_ref,
                o_ref, st_ref, *, c, nsub, dk):
    @pl.when(pl.program_id(2) == 0)
    def _():
        st_ref[...] = jnp.zeros_like(st_ref)

    half = dk // 2
    for sidx in range(nsub):
        rows = slice(sidx * c, (sidx + 1) * c)
        cos = cos_ref[rows, :]
        sin = sin_ref[rows, :]

        def rope(x):
            x1, x2 = x[:, :half], x[:, half:]
            return jnp.concatenate([x1 * cos - x2 * sin, x2 * cos + x1 * sin], axis=1)

        q = rope(q_ref[rows, :])
        k = rope(k_ref[rows, :]) * (dk ** -0.5)
        qb, kb = q.astype(BF16), k.astype(BF16)
        vb = v_ref[rows, :].astype(BF16)
        st = st_ref[...]
        inner = _dot_nt(qb, kb) * dm_ref[...]
        o = _dot(inner.astype(BF16), vb) + _dot(qb, st.astype(BF16)) * qd_ref[...]
        kd = (k * kd_ref[...]).astype(BF16)
        st_ref[...] = st * cd_ref[...] + _dot_tn(kd, vb)
        on = o * lax.rsqrt(jnp.mean(o * o, axis=-1, keepdims=True) + NORM_EPS)
        o_ref[rows, :] = (on * _silu(g_ref[rows, :])).astype(o_ref.dtype)


def _retention(cfg, proj):
    bsz, s, _ = proj.shape
    h, dk, dv, c = cfg.ret_heads, cfg.ret_key_dim, cfg.ret_val_dim, cfg.ret_chunk
    ts = min(cfg.ret_rows, s)
    assert s % ts == 0 and ts % c == 0
    q0 = 3 * cfg.dm // dk
    k0 = (3 * cfg.dm + cfg.dk) // dk
    v0 = (3 * cfg.dm + 2 * cfg.dk) // dv
    g0 = (3 * cfg.dm + 2 * cfg.dk + cfg.dv) // dv
    assert (3 * cfg.dm) % dk == 0 and (3 * cfg.dm + 2 * cfg.dk) % dv == 0
    cos, sin = _rope_tables(s, dk)
    log_g = jnp.log1p(-jnp.exp2(-5.0 - jnp.arange(h, dtype=F32)))
    idx = jnp.arange(c, dtype=F32)
    diff = idx[:, None] - idx[None, :]
    dmask = jnp.where(diff >= 0, jnp.exp(jnp.maximum(diff, 0.0) * log_g[:, None, None]), 0.0)
    qdec = jnp.exp((idx + 1.0) * log_g[:, None])[..., None]
    kdec = jnp.exp((c - 1.0 - idx) * log_g[:, None])[..., None]
    cdec = jnp.broadcast_to(jnp.exp(c * log_g)[:, None, None], (h, 1, dv))
    rowspec = lambda w, c0: pl.BlockSpec((None, ts, w), lambda b, hh, i: (b, i, c0 + hh))
    tab = pl.BlockSpec((ts, dk // 2), lambda b, hh, i: (i, 0))
    return pl.pallas_call(
        functools.partial(_ret_kernel, c=c, nsub=ts // c, dk=dk),
        grid=(bsz, h, s // ts),
        in_specs=[
            rowspec(dk, q0), rowspec(dk, k0), rowspec(dv, v0), rowspec(dv, g0), tab, tab,
            pl.BlockSpec((None, c, c), lambda b, hh, i: (hh, 0, 0)),
            pl.BlockSpec((None, c, 1), lambda b, hh, i: (hh, 0, 0)),
            pl.BlockSpec((None, c, 1), lambda b, hh, i: (hh, 0, 0)),
            pl.BlockSpec((None, 1, dv), lambda b, hh, i: (hh, 0, 0)),
        ],
        out_specs=pl.BlockSpec((None, ts, dv), lambda b, hh, i: (b, i, hh)),
        out_shape=jax.ShapeDtypeStruct((bsz, s, h * dv), BF16),
        scratch_shapes=[pltpu.VMEM((dk, dv), F32)],
        compiler_params=_cparams(cfg, "parallel", "parallel", "arbitrary"),
        name="retention",
    )(proj, proj, proj, proj, cos, sin, dmask, qdec, kdec, cdec)


CONV_HALO = 32


def _conv_kernel(a_ref, g_ref, w_ref, b_ref, o_ref, buf_ref, *, ts, kw, rc):
    @pl.when(pl.program_id(2) == 0)
    def _():
        buf_ref[0:CONV_HALO, :] = jnp.zeros((CONV_HALO, buf_ref.shape[1]), F32)

    buf_ref[CONV_HALO:CONV_HALO + ts, :] = a_ref[...] * jax.nn.sigmoid(g_ref[...])
    off = CONV_HALO - (kw - 1)
    bias = b_ref[...]
    for r0 in range(0, ts, rc):
        acc = jnp.broadcast_to(bias, (rc, bias.shape[1]))
        for j in range(kw):
            acc = acc + w_ref[j:j + 1, :] * buf_ref[off + r0 + j:off + r0 + j + rc, :]
        o_ref[r0:r0 + rc, :] = acc
    buf_ref[0:CONV_HALO, :] = buf_ref[ts:ts + CONV_HALO, :]


def _conv_glu(cfg, proj, conv_w, conv_b):
    bsz, s, _ = proj.shape
    ch, kw = cfg.conv_ch, cfg.conv_width
    ts, tc = min(cfg.conv_rows, s), min(cfg.conv_cols, ch)
    assert kw - 1 <= CONV_HALO <= ts and s % ts == 0 and ch % tc == 0
    nct = ch // tc
    wp = jnp.zeros((CONV_HALO, ch), F32).at[:kw].set(conv_w)
    return pl.pallas_call(
        functools.partial(_conv_kernel, ts=ts, kw=kw, rc=32),
        grid=(bsz, nct, s // ts),
        in_specs=[
            pl.BlockSpec((None, ts, tc), lambda b, c, i: (b, i, c)),
            pl.BlockSpec((None, ts, tc), lambda b, c, i: (b, i, nct + c)),
            pl.BlockSpec((CONV_HALO, tc), lambda b, c, i: (0, c)),
            pl.BlockSpec((1, tc), lambda b, c, i: (0, c)),
        ],
        out_specs=pl.BlockSpec((None, ts, tc), lambda b, c, i: (b, i, c)),
        out_shape=jax.ShapeDtypeStruct((bsz, s, ch), F32),
        scratch_shapes=[pltpu.VMEM((CONV_HALO + ts, tc), F32)],
        compiler_params=_cparams(cfg, "parallel", "parallel", "arbitrary"),
        name="glu_causal_conv",
    )(proj, proj, wp, conv_b.reshape(1, ch))


def _ln_silu_kernel(x_ref, g_ref, b_ref, o_ref):
    x = x_ref[...]
    mu = jnp.mean(x, axis=-1, keepdims=True)
    d = x - mu
    var = jnp.mean(d * d, axis=-1, keepdims=True)
    y = d * lax.rsqrt(var + CONV_LN_EPS) * g_ref[...] + b_ref[...]
    o_ref[...] = _silu(y).astype(o_ref.dtype)


def _ln_silu(cfg, x, g, b):
    bsz, s, d = x.shape
    ts = min(cfg.row_tile, s)
    row = pl.BlockSpec((None, ts, d), lambda bb, i: (bb, i, 0))
    vec = pl.BlockSpec((1, d), lambda bb, i: (0, 0))
    return pl.pallas_call(
        _ln_silu_kernel,
        grid=(bsz, s // ts),
        in_specs=[row, vec, vec],
        out_specs=row,
        out_shape=jax.ShapeDtypeStruct((bsz, s, d), BF16),
        compiler_params=_cparams(cfg, "parallel", "parallel"),
        name="layernorm_swish",
    )(x, g.reshape(1, d), b.reshape(1, d))


def _group_ones(n, group):
    r = lax.broadcasted_iota(jnp.int32, (n, n), 0)
    c = lax.broadcasted_iota(jnp.int32, (n, n), 1)
    shift = int(math.log2(group))
    return jnp.where((r >> shift) == (c >> shift), 1.0, 0.0).astype(F32)


def _group_sum(x, gmat):
    n = x.shape[1]
    parts = [_dot(x[:, s0:s0 + V7X_LANES], gmat, precision=HIGHEST) for s0 in range(0, n, V7X_LANES)]
    return parts[0] if len(parts) == 1 else jnp.concatenate(parts, axis=1)


def _rwkv_pre_kernel(r_ref, k_ref, v_ref, lo_ref, mur_ref, muk_ref, muv_ref, mul_ref,
                     w0_ref, wup_ref, a0_ref, aup_ref, gup_ref, kk_ref, ka_ref,
                     ro_ref, lw_ref, ko_ref, vo_ref, ao_ref, bo_ref, go_ref,
                     lr_ref, lk_ref, lv_ref, ll_ref, *, ts, hd, lw_pad):
    first = pl.program_id(1) == 0

    def shift(x_ref, last_ref, mu_ref):
        @pl.when(first)
        def _():
            last_ref[...] = jnp.zeros_like(last_ref)

        x = x_ref[...]
        row = lax.broadcasted_iota(jnp.int32, x.shape, 0)
        prev = jnp.where(row == 0, last_ref[V7X_SUBLANES - 1:V7X_SUBLANES, :], pltpu.roll(x, 1, 0))
        last_ref[...] = x[ts - V7X_SUBLANES:ts, :]
        return x + (prev - x) * mu_ref[...]

    r = shift(r_ref, lr_ref, mur_ref)
    k = shift(k_ref, lk_ref, muk_ref)
    v = shift(v_ref, lv_ref, muv_ref)
    lo = shift(lo_ref, ll_ref, mul_ref)
    xw, xa, xg = lo[:, :lw_pad], lo[:, lw_pad:2 * lw_pad], lo[:, 2 * lw_pad:]

    z = w0_ref[...] + _dot(jnp.tanh(xw), wup_ref[...], precision=HIGHEST)
    softplus = jnp.maximum(-z, 0.0) + jnp.log(1.0 + jnp.exp(-jnp.abs(z)))
    lw_ref[...] = -jnp.exp(-softplus - 0.5)
    a = jax.nn.sigmoid(a0_ref[...] + _dot(xa, aup_ref[...], precision=HIGHEST))
    go_ref[...] = _dot(jax.nn.sigmoid(xg).astype(BF16), gup_ref[...].astype(BF16))

    kkr = k * kk_ref[...]
    ss = _group_sum(kkr * kkr, _group_ones(V7X_LANES, hd))
    kk = kkr / jnp.maximum(jnp.sqrt(ss), 1e-12)
    ro_ref[...] = r
    vo_ref[...] = v
    ko_ref[...] = k * (1.0 + (a - 1.0) * ka_ref[...])
    ao_ref[...] = -kk
    bo_ref[...] = kk * a


def _rwkv_pre(cfg, proj, lora, mu, w0, w_up, a0, a_up, g_up, k_k, k_a):
    bsz, s, _ = proj.shape
    d = cfg.rwkv_dim
    ts = min(cfg.row_tile // 2, s)
    lw_pad = V7X_LANES
    assert cfg.decay_lora <= lw_pad and cfg.iclr_lora <= lw_pad and (2 * cfg.conv_ch) % d == 0
    lo_w = lora.shape[2]
    c0 = 2 * cfg.conv_ch // d
    pad_rows = lambda w: jnp.zeros((lw_pad, d), F32).at[:w.shape[0]].set(w)
    pad_vec = lambda vv, n: jnp.zeros((1, n), F32).at[0, :vv.shape[0]].set(vv)
    mu_r, mu_k, mu_v = (mu[i * d:(i + 1) * d].reshape(1, d) for i in range(3))
    o = 3 * d
    mu_l = jnp.concatenate([
        pad_vec(mu[o:o + cfg.decay_lora], lw_pad),
        pad_vec(mu[o + cfg.decay_lora:o + cfg.decay_lora + cfg.iclr_lora], lw_pad),
        mu[o + cfg.decay_lora + cfg.iclr_lora:].reshape(1, -1)], axis=1)
    row = lambda cb: pl.BlockSpec((None, ts, d), lambda b, i: (b, i, cb))
    lrow = pl.BlockSpec((None, ts, lo_w), lambda b, i: (b, i, 0))
    vec = lambda n: pl.BlockSpec((1, n), lambda b, i: (0, 0))
    mat = lambda rws: pl.BlockSpec((rws, d), lambda b, i: (0, 0))
    orow = pl.BlockSpec((None, ts, d), lambda b, i: (b, i, 0))
    return pl.pallas_call(
        functools.partial(_rwkv_pre_kernel, ts=ts, hd=cfg.rwkv_head_dim, lw_pad=lw_pad),
        grid=(bsz, s // ts),
        in_specs=[row(c0), row(c0 + 1), row(c0 + 2), lrow, vec(d), vec(d), vec(d), vec(lo_w),
                  vec(d), mat(lw_pad), vec(d), mat(lw_pad), mat(cfg.gate_lora), vec(d), vec(d)],
        out_specs=[orow] * 7,
        out_shape=[jax.ShapeDtypeStruct((bsz, s, d), F32)] * 7,
        scratch_shapes=[pltpu.VMEM((V7X_SUBLANES, d), F32)] * 3 + [pltpu.VMEM((V7X_SUBLANES, lo_w), F32)],
        compiler_params=_cparams(cfg, "parallel", "arbitrary"),
        name="rwkv_token_shift_lora",
    )(proj, proj, proj, lora, mu_r, mu_k, mu_v, mu_l, w0.reshape(1, d), pad_rows(w_up),
      a0.reshape(1, d), pad_rows(a_up), g_up, k_k.reshape(1, d), k_a.reshape(1, d))


def _scan_chunk(r, lw, k, v, a, b, st, consts):
    tri_incl, tri_strict, eye, lane_a, bd = consts
    L = r.shape[0]
    sel = lambda x, y: jnp.where(lane_a, x, y)
    cum = _dot(jnp.where(tri_incl, 1.0, 0.0), lw, precision=HIGHEST)
    cl = cum[L - 1:L, :]
    g_in, g_ex, g_inv, tail = jnp.exp(cum), jnp.exp(cum - lw), jnp.exp(-cum), jnp.exp(cl - cum)
    at, rt, bt, kt = a * g_ex, r * g_in, b * g_inv, k * g_inv
    lhs = jnp.concatenate([sel(at, 0.0), sel(0.0, at), sel(rt, 0.0), sel(0.0, rt)], axis=0)
    xb = _dot_nt(lhs, bt, precision=HIGHEST)
    xk = _dot_nt(lhs, kt, precision=HIGHEST)
    low = lambda x, i: jnp.where(tri_strict, x[i * L:(i + 1) * L], 0.0)
    lowi = lambda x, i: jnp.where(tri_incl, x[i * L:(i + 1) * L], 0.0)

    def inv_unit_lower(n):
        x, p = eye + n, n
        for _ in range(int(math.log2(L)) - 1):
            p = _dot(p, p, precision=HIGHEST)
            x = x + _dot(x, p, precision=HIGHEST)
        return x

    rhs = (_dot_nt(at, st, precision=HIGHEST)
           + sel(_dot(low(xk, 0), v, precision=HIGHEST), _dot(low(xk, 1), v, precision=HIGHEST)))
    u = sel(_dot(inv_unit_lower(low(xb, 0)), rhs, precision=HIGHEST),
            _dot(inv_unit_lower(low(xb, 1)), rhs, precision=HIGHEST))
    y = (_dot_nt(rt, st, precision=HIGHEST)
         + sel(_dot(lowi(xb, 2), u, precision=HIGHEST), _dot(lowi(xb, 3), u, precision=HIGHEST))
         + sel(_dot(lowi(xk, 2), v, precision=HIGHEST), _dot(lowi(xk, 3), v, precision=HIGHEST)))
    upd = _dot_tn(jnp.concatenate([u, v], axis=0),
                  jnp.concatenate([b * tail, k * tail], axis=0), precision=HIGHEST)
    return y, st * jnp.exp(cl) + jnp.where(bd, upd, 0.0)


def _scan_kernel(r_ref, lw_ref, k_ref, v_ref, a_ref, b_ref, y_ref, st_ref, *, ts, L, pairs, hd):
    @pl.when(pl.program_id(2) == 0)
    def _():
        st_ref[...] = jnp.zeros_like(st_ref)

    ri = lax.broadcasted_iota(jnp.int32, (L, L), 0)
    ci = lax.broadcasted_iota(jnp.int32, (L, L), 1)
    lane = lax.broadcasted_iota(jnp.int32, (1, V7X_LANES), 1)
    r2 = lax.broadcasted_iota(jnp.int32, (V7X_LANES, V7X_LANES), 0)
    c2 = lax.broadcasted_iota(jnp.int32, (V7X_LANES, V7X_LANES), 1)
    consts = (ri >= ci, ri > ci, jnp.where(ri == ci, 1.0, 0.0), lane < hd, (r2 < hd) == (c2 < hd))

    def body(c, carry):
        rows = pl.ds(pl.multiple_of(c * L, L), L)
        for p in range(pairs):
            cols = slice(p * V7X_LANES, (p + 1) * V7X_LANES)
            y, st = _scan_chunk(r_ref[rows, cols], lw_ref[rows, cols], k_ref[rows, cols],
                                v_ref[rows, cols], a_ref[rows, cols], b_ref[rows, cols],
                                st_ref[p], consts)
            y_ref[rows, cols] = y
            st_ref[p] = st
        return carry

    lax.fori_loop(0, ts // L, body, 0)


def _rwkv_scan(cfg, r, lw, k, v, a, b):
    bsz, s, d = r.shape
    hd = cfg.rwkv_head_dim
    assert 2 * hd == V7X_LANES
    ts, L = min(cfg.scan_rows, s), cfg.scan_chunk
    npairs = d // V7X_LANES
    pairs = min(cfg.scan_pairs, npairs)
    assert s % ts == 0 and ts % L == 0 and npairs % pairs == 0
    blk = pl.BlockSpec((None, ts, pairs * V7X_LANES), lambda bb, p, i: (bb, i, p))
    return pl.pallas_call(
        functools.partial(_scan_kernel, ts=ts, L=L, pairs=pairs, hd=hd),
        grid=(bsz, npairs // pairs, s // ts),
        in_specs=[blk] * 6,
        out_specs=blk,
        out_shape=jax.ShapeDtypeStruct((bsz, s, d), F32),
        scratch_shapes=[pltpu.VMEM((pairs, V7X_LANES, V7X_LANES), F32)],
        compiler_params=_cparams(cfg, "parallel", "parallel", "arbitrary"),
        name="rwkv7_scan",
    )(r, lw, k, v, a, b)


def _rwkv_post_kernel(y_ref, r_ref, k_ref, v_ref, g_ref, rk_ref, lg_ref, lb_ref, o_ref, *, hd):
    gmat = _group_ones(V7X_LANES, hd)
    y = y_ref[...]
    mu = _group_sum(y, gmat) * (1.0 / hd)
    d = y - mu
    var = _group_sum(d * d, gmat) * (1.0 / hd)
    yn = d * lax.rsqrt(var + RWKV_LNX_EPS) * lg_ref[...] + lb_ref[...]
    bonus = _group_sum(r_ref[...] * k_ref[...] * rk_ref[...], gmat) * v_ref[...]
    o_ref[...] = ((yn + bonus) * g_ref[...]).astype(o_ref.dtype)


def _rwkv_post(cfg, y, r, k, v, g, r_k, lnx_g, lnx_b):
    bsz, s, d = y.shape
    ts = min(cfg.row_tile, s)
    row = pl.BlockSpec((None, ts, d), lambda b, i: (b, i, 0))
    vec = pl.BlockSpec((1, d), lambda b, i: (0, 0))
    return pl.pallas_call(
        functools.partial(_rwkv_post_kernel, hd=cfg.rwkv_head_dim),
        grid=(bsz, s // ts),
        in_specs=[row] * 5 + [vec] * 3,
        out_specs=row,
        out_shape=jax.ShapeDtypeStruct((bsz, s, d), BF16),
        compiler_params=_cparams(cfg, "parallel", "parallel"),
        name="rwkv_groupnorm_gate",
    )(y, r, k, v, g, r_k.reshape(1, d), lnx_g.reshape(1, d), lnx_b.reshape(1, d))


def _even_mixer(cfg, h, w_in, w_out):
    bsz, s, d = h.shape
    proj = _matmul(cfg, h.reshape(bsz * s, d), w_in, 0, 0, cfg.even_in, F32,
                   cfg.mm_tm, cfg.mm_tn, d).reshape(bsz, s, cfg.even_in)
    merged = jnp.concatenate([_moba(cfg, proj), _retention(cfg, proj)], axis=-1)
    k_out = merged.shape[-1]
    tk = k_out // 3 if k_out % 3 == 0 else k_out
    return _matmul(cfg, merged.reshape(bsz * s, k_out), w_out, 0, 0, d, F32,
                   cfg.mm_tm, cfg.mm_tn, tk).reshape(bsz, s, d)


def _odd_mixer(cfg, h, w_in, w_out, conv_w, conv_b, conv_ln_g, conv_ln_b, mu, w0, w_up, a0, a_up,
               g_up, k_k, k_a, r_k, lnx_g, lnx_b):
    bsz, s, d = h.shape
    h2 = h.reshape(bsz * s, d)
    proj = _matmul(cfg, h2, w_in, 0, 0, cfg.odd_main, F32,
                   cfg.mm_tm, cfg.mm_tn, d).reshape(bsz, s, cfg.odd_main)
    lw_pad = V7X_LANES
    wl = w_in[0, :, cfg.odd_main:]
    zc = lambda n: jnp.zeros((d, n), F32)
    o1, o2 = cfg.decay_lora, cfg.decay_lora + cfg.iclr_lora
    wl = jnp.concatenate([wl[:, :o1], zc(lw_pad - cfg.decay_lora), wl[:, o1:o2],
                          zc(lw_pad - cfg.iclr_lora), wl[:, o2:]], axis=1)[None]
    lo_w = wl.shape[2]
    lora = _matmul(cfg, h2, wl, 0, 0, lo_w, F32, cfg.mm_tm, lo_w, d).reshape(bsz, s, lo_w)

    u = _ln_silu(cfg, _conv_glu(cfg, proj, conv_w, conv_b), conv_ln_g, conv_ln_b)
    r, lw, k, v, a, b, g = _rwkv_pre(cfg, proj, lora, mu, w0, w_up, a0, a_up, g_up, k_k, k_a)
    y = _rwkv_scan(cfg, r, lw, k, v, a, b)
    y = _rwkv_post(cfg, y, r, k, v, g, r_k.reshape(-1), lnx_g, lnx_b)
    merged = jnp.concatenate([u, y], axis=-1)
    k_out = merged.shape[-1]
    return _matmul(cfg, merged.reshape(bsz * s, k_out), w_out, 0, 0, d, F32,
                   cfg.mm_tm, cfg.mm_tn, k_out // 2).reshape(bsz, s, d)


def _forward(cfg, x, c, w_ada, b_ada, norm_g, w_ffn_in, w_ffn_out, even_w_in, even_w_out, odd_w_in,
             odd_w_out, conv_w, conv_b, conv_ln_g, conv_ln_b, rwkv_mu, rwkv_w0, rwkv_w_up, rwkv_a0,
             rwkv_a_up, rwkv_g_up, rwkv_k_k, rwkv_k_a, rwkv_r_k, rwkv_lnx_g, rwkv_lnx_b):
    bsz, s, d = x.shape
    depth = w_ada.shape[0]
    mods = _modulation(cfg, c, w_ada, b_ada)
    w_ffn_out_bf = w_ffn_out.astype(BF16)
    sh_m, sc_m = mods[0, :, 0], mods[0, :, 1]
    h = _norm_mod(cfg, x, norm_g[0, 0], sc_m, sh_m)
    for layer in range(depth):
        g_m, sh_f, sc_f, g_f = (mods[layer, :, i] for i in (2, 3, 4, 5))
        j = layer // 2
        if layer % 2 == 0:
            o = _even_mixer(cfg, h, even_w_in[j:j + 1], even_w_out[j:j + 1])
        else:
            o = _odd_mixer(cfg, h, odd_w_in[j:j + 1], odd_w_out[j:j + 1], conv_w[j], conv_b[j],
                           conv_ln_g[j], conv_ln_b[j], rwkv_mu[j], rwkv_w0[j], rwkv_w_up[j],
                           rwkv_a0[j], rwkv_a_up[j], rwkv_g_up[j], rwkv_k_k[j], rwkv_k_a[j],
                           rwkv_r_k[j], rwkv_lnx_g[j], rwkv_lnx_b[j])
        x, h = _resid(cfg, x, o, norm_g[layer, 1], g_m, (norm_g[layer, 2], sc_f, sh_f))
        act = _ffn_in(cfg, h.reshape(bsz * s, d), w_ffn_in, layer)
        f = _matmul(cfg, act, w_ffn_out_bf, layer, 0, d, F32, cfg.ffn_out_tm, cfg.mm_tn,
                    act.shape[1]).reshape(bsz, s, d)
        if layer + 1 < depth:
            nxt = (norm_g[layer + 1, 0], mods[layer + 1, :, 1], mods[layer + 1, :, 0])
            x, h = _resid(cfg, x, f, norm_g[layer, 3], g_f, nxt)
        else:
            x = _resid(cfg, x, f, norm_g[layer, 3], g_f)
    return x


def kernel(x, c, w_ada, b_ada, norm_g, w_ffn_in, w_ffn_out, even_w_in, even_w_out, odd_w_in, odd_w_out, conv_w, conv_b, conv_ln_g, conv_ln_b, rwkv_mu, rwkv_w0, rwkv_w_up, rwkv_a0, rwkv_a_up, rwkv_g_up, rwkv_k_k, rwkv_k_a, rwkv_r_k, rwkv_lnx_g, rwkv_lnx_b):
    return _forward(Config(), x, c, w_ada, b_ada, norm_g, w_ffn_in, w_ffn_out, even_w_in, even_w_out,
                    odd_w_in, odd_w_out, conv_w, conv_b, conv_ln_g, conv_ln_b, rwkv_mu, rwkv_w0,
                    rwkv_w_up, rwkv_a0, rwkv_a_up, rwkv_g_up, rwkv_k_k, rwkv_k_a, rwkv_r_k,
                    rwkv_lnx_g, rwkv_lnx_b)
```

```python
import dataclasses
import functools
import math

import jax
import jax.numpy as jnp
from jax import lax
from jax.experimental import pallas as pl
from jax.experimental.pallas import tpu as pltpu

F32 = jnp.float32
BF16 = jnp.bfloat16
HIGHEST = lax.Precision.HIGHEST

V7X_LANES = 128
V7X_SUBLANES = 8
MIB = 1024 * 1024
NORM_EPS = 1e-6
ROPE_THETA = 10000.0
CONV_LN_EPS = 1e-5
RWKV_LNX_EPS = 64e-5


@dataclasses.dataclass(frozen=True)
class Config:
    d_model: int = 4096
    moba_heads: int = 16
    moba_head_dim: int = 128
    moba_block: int = 256
    moba_topk: int = 3
    ret_heads: int = 8
    ret_key_dim: int = 256
    ret_val_dim: int = 512
    ret_chunk: int = 128
    conv_ch: int = 2048
    conv_width: int = 31
    rwkv_dim: int = 2048
    rwkv_head_dim: int = 64
    decay_lora: int = 96
    iclr_lora: int = 96
    gate_lora: int = 256
    ffn_hidden: int = 11008
    row_tile: int = 256
    mm_tm: int = 1024
    mm_tn: int = 512
    ffn_tn: int = 256
    ffn_out_tm: int = 512
    ret_rows: int = 512
    conv_rows: int = 256
    conv_cols: int = 256
    scan_rows: int = 512
    scan_chunk: int = 64
    scan_pairs: int = 8
    moba_group: int = 4
    vmem_mib: int = 56

    @property
    def dm(self):
        return self.moba_heads * self.moba_head_dim

    @property
    def dk(self):
        return self.ret_heads * self.ret_key_dim

    @property
    def dv(self):
        return self.ret_heads * self.ret_val_dim

    @property
    def even_in(self):
        return 3 * self.dm + 2 * self.dk + 2 * self.dv

    @property
    def lora_in(self):
        return self.decay_lora + self.iclr_lora + self.gate_lora

    @property
    def odd_main(self):
        return 2 * self.conv_ch + 3 * self.rwkv_dim


def _cparams(cfg, *sem):
    return pltpu.CompilerParams(dimension_semantics=sem, vmem_limit_bytes=cfg.vmem_mib * MIB)


def _silu(x):
    return x * jax.nn.sigmoid(x)


def _dot(a, b, **kw):
    return jnp.dot(a, b, preferred_element_type=F32, **kw)


def _dot_nt(a, b, **kw):
    return lax.dot_general(a, b, (((1,), (1,)), ((), ())), preferred_element_type=F32, **kw)


def _dot_tn(a, b, **kw):
    return lax.dot_general(a, b, (((0,), (0,)), ((), ())), preferred_element_type=F32, **kw)


def _ada_kernel(c_ref, w_ref, b_ref, o_ref):
    s = _silu(c_ref[...])
    o_ref[...] = _dot(s, w_ref[...], precision=HIGHEST) + b_ref[...]


def _modulation(cfg, c, w_ada, b_ada, tn=512):
    depth, d, n = w_ada.shape
    bsz = c.shape[0]
    cp = jnp.zeros((V7X_SUBLANES, d), F32).at[:bsz].set(c)
    out = pl.pallas_call(
        _ada_kernel,
        grid=(depth, n // tn),
        in_specs=[
            pl.BlockSpec((V7X_SUBLANES, d), lambda l, j: (0, 0)),
            pl.BlockSpec((None, d, tn), lambda l, j: (l, 0, j)),
            pl.BlockSpec((None, 1, tn), lambda l, j: (l, 0, j)),
        ],
        out_specs=pl.BlockSpec((None, V7X_SUBLANES, tn), lambda l, j: (l, 0, j)),
        out_shape=jax.ShapeDtypeStruct((depth, V7X_SUBLANES, n), F32),
        compiler_params=_cparams(cfg, "parallel", "parallel"),
        name="adaln_modulation",
    )(cp, w_ada, b_ada.reshape(depth, 1, n))
    return out[:, :bsz].reshape(depth, bsz, 6, 1, d)


def _rms(x, g):
    return x * lax.rsqrt(jnp.mean(x * x, axis=-1, keepdims=True) + NORM_EPS) * g


def _norm_mod_kernel(x_ref, g_ref, sc_ref, sh_ref, o_ref):
    y = _rms(x_ref[...], g_ref[...])
    o_ref[...] = (y * (1.0 + sc_ref[...]) + sh_ref[...]).astype(o_ref.dtype)


def _norm_mod(cfg, x, g, sc, sh):
    bsz, s, d = x.shape
    ts = min(cfg.row_tile, s)
    row = pl.BlockSpec((None, ts, d), lambda b, i: (b, i, 0))
    vec = pl.BlockSpec((1, d), lambda b, i: (0, 0))
    mod = pl.BlockSpec((None, 1, d), lambda b, i: (b, 0, 0))
    return pl.pallas_call(
        _norm_mod_kernel,
        grid=(bsz, s // ts),
        in_specs=[row, vec, mod, mod],
        out_specs=row,
        out_shape=jax.ShapeDtypeStruct((bsz, s, d), BF16),
        compiler_params=_cparams(cfg, "parallel", "parallel"),
        name="norm_modulate",
    )(x, g.reshape(1, d), sc, sh)


def _resid_kernel(x_ref, o_ref, ga_ref, gate_ref, *rest, with_h):
    xn = x_ref[...] + gate_ref[...] * _rms(o_ref[...], ga_ref[...])
    if with_h:
        gb_ref, sc_ref, sh_ref, xn_ref, h_ref = rest
        xn_ref[...] = xn
        h_ref[...] = (_rms(xn, gb_ref[...]) * (1.0 + sc_ref[...]) + sh_ref[...]).astype(h_ref.dtype)
    else:
        (xn_ref,) = rest
        xn_ref[...] = xn


def _resid(cfg, x, o, ga, gate, nxt=None):
    bsz, s, d = x.shape
    ts = min(cfg.row_tile, s)
    row = pl.BlockSpec((None, ts, d), lambda b, i: (b, i, 0))
    vec = pl.BlockSpec((1, d), lambda b, i: (0, 0))
    mod = pl.BlockSpec((None, 1, d), lambda b, i: (b, 0, 0))
    with_h = nxt is not None
    in_specs = [row, row, vec, mod]
    args = [x, o, ga.reshape(1, d), gate]
    out_specs = [row]
    out_shape = [jax.ShapeDtypeStruct((bsz, s, d), F32)]
    if with_h:
        gb, sc, sh = nxt
        in_specs += [vec, mod, mod]
        args += [gb.reshape(1, d), sc, sh]
        out_specs.append(row)
        out_shape.append(jax.ShapeDtypeStruct((bsz, s, d), BF16))
    outs = pl.pallas_call(
        functools.partial(_resid_kernel, with_h=with_h),
        grid=(bsz, s // ts),
        in_specs=in_specs,
        out_specs=out_specs,
        out_shape=out_shape,
        compiler_params=_cparams(cfg, "parallel", "parallel"),
        name="residual_norm",
    )(*args)
    return outs if with_h else outs[0]


def _mm_kernel(x_ref, w_ref, o_ref, *acc, nk):
    part = _dot(x_ref[...], w_ref[...].astype(BF16))
    if nk == 1:
        o_ref[...] = part.astype(o_ref.dtype)
        return
    (acc_ref,) = acc
    k = pl.program_id(2)

    @pl.when(k == 0)
    def _():
        acc_ref[...] = part

    @pl.when(k > 0)
    def _():
        acc_ref[...] += part

    @pl.when(k == nk - 1)
    def _():
        o_ref[...] = acc_ref[...].astype(o_ref.dtype)


def _matmul(cfg, x, w, layer, col0, n, out_dtype, tm, tn, tk):
    m, kdim = x.shape
    tm, tn, tk = min(tm, m), min(tn, n), min(tk, kdim)
    nk = kdim // tk
    cb = col0 // tn
    assert m % tm == 0 and n % tn == 0 and kdim % tk == 0 and col0 % tn == 0
    scratch = [] if nk == 1 else [pltpu.VMEM((tm, tn), F32)]
    return pl.pallas_call(
        functools.partial(_mm_kernel, nk=nk),
        grid=(m // tm, n // tn, nk),
        in_specs=[
            pl.BlockSpec((tm, tk), lambda i, j, k: (i, k)),
            pl.BlockSpec((None, tk, tn), lambda i, j, k: (layer, k, cb + j)),
        ],
        out_specs=pl.BlockSpec((tm, tn), lambda i, j, k: (i, j)),
        out_shape=jax.ShapeDtypeStruct((m, n), out_dtype),
        scratch_shapes=scratch,
        compiler_params=_cparams(cfg, "parallel", "parallel", "arbitrary"),
        name="matmul",
    )(x, w)


def _ffn_in_kernel(x_ref, wg_ref, wu_ref, o_ref):
    x = x_ref[...]
    gate = _dot(x, wg_ref[...].astype(BF16))
    up = _dot(x, wu_ref[...].astype(BF16))
    o_ref[...] = (_silu(gate) * up).astype(o_ref.dtype)


def _ffn_in(cfg, x, w, layer):
    m, kdim = x.shape
    hid = w.shape[2] // 2
    tm, tn = min(cfg.mm_tm, m), min(cfg.ffn_tn, hid)
    nt = hid // tn
    assert m % tm == 0 and hid % tn == 0
    return pl.pallas_call(
        _ffn_in_kernel,
        grid=(m // tm, nt),
        in_specs=[
            pl.BlockSpec((tm, kdim), lambda i, j: (i, 0)),
            pl.BlockSpec((None, kdim, tn), lambda i, j: (layer, 0, j)),
            pl.BlockSpec((None, kdim, tn), lambda i, j: (layer, 0, nt + j)),
        ],
        out_specs=pl.BlockSpec((tm, tn), lambda i, j: (i, j)),
        out_shape=jax.ShapeDtypeStruct((m, hid), BF16),
        compiler_params=_cparams(cfg, "parallel", "parallel"),
        name="ffn_in_swiglu",
    )(x, w, w)


def _moba_kernel(q_ref, k_ref, v_ref, cq_ref, sq_ref, ck_ref, sk_ref, o_ref,
                 kr_ref, vt_ref, km_ref, sel_ref, *, nb, blk, dh, topk, group):
    qi = pl.program_id(2)
    half = dh // 2

    @pl.when(qi == 0)
    def _():
        for j in range(nb):
            rows = slice(j * blk, (j + 1) * blk)
            kb = k_ref[rows, :]
            kr = kb * ck_ref[rows, :] + pltpu.roll(kb, half, 1) * sk_ref[rows, :]
            km_ref[j:j + 1, :] = jnp.mean(kr, axis=0, keepdims=True)
            kr_ref[j] = kr.astype(BF16)
            vt_ref[j] = v_ref[rows, :].T.astype(BF16)

    q = q_ref[...]
    qr = q * cq_ref[...] + pltpu.roll(q, half, 1) * sq_ref[...]

    gate = _dot_nt(km_ref[...], qr, precision=HIGHEST)
    brow = lax.broadcasted_iota(jnp.int32, (nb, blk), 0)
    gm = jnp.where(brow < qi, gate, -jnp.inf)
    for j in range(nb):
        gj = gm[j:j + 1, :]
        gt = jnp.where(gm > gj, 1.0, 0.0)
        eq = jnp.where(gm == gj, jnp.where(brow < j, 1.0, 0.0), 0.0)
        cnt = jnp.sum(gt + eq, axis=0, keepdims=True)
        keep = jnp.where(cnt < topk, 1.0, 0.0)
        keep = jnp.where(j < qi, keep, 0.0)
        sel_ref[j] = jnp.broadcast_to(keep, (V7X_SUBLANES, blk))

    qs = (qr * (dh ** -0.5)).T.astype(BF16)
    kpos = lax.broadcasted_iota(jnp.int32, (blk, blk), 0)
    qpos = lax.broadcasted_iota(jnp.int32, (blk, blk), 1)
    s = _dot(kr_ref[qi], qs)
    s = jnp.where(kpos <= qpos, s, -jnp.inf)
    m = jnp.max(s, axis=0, keepdims=True)
    p = jnp.exp(s - m)
    l = jnp.sum(p, axis=0, keepdims=True)
    acc = _dot(vt_ref[qi], p.astype(BF16))

    def body(g, carry):
        m, l, acc = carry
        ss = []
        for u in range(group):
            j = g * group + u
            s = _dot(kr_ref[j], qs)
            ss.append(jnp.where(sel_ref[j][0:1, :] > 0.0, s, -jnp.inf))
        m_new = m
        for s in ss:
            m_new = jnp.maximum(m_new, jnp.max(s, axis=0, keepdims=True))
        alpha = jnp.exp(m - m_new)
        l = alpha * l
        acc = alpha * acc
        for u in range(group):
            p = jnp.exp(ss[u] - m_new)
            l = l + jnp.sum(p, axis=0, keepdims=True)
            acc = acc + _dot(vt_ref[g * group + u], p.astype(BF16))
        return m_new, l, acc

    m, l, acc = lax.fori_loop(0, (qi + group - 1) // group, body, (m, l, acc))
    o_ref[...] = (acc / l).T.astype(o_ref.dtype)


def _rope_tables(seq, dim):
    inv = 1.0 / (ROPE_THETA ** (jnp.arange(0, dim, 2, dtype=F32) / dim))
    ang = jnp.arange(seq, dtype=F32)[:, None] * inv[None, :]
    return jnp.cos(ang), jnp.sin(ang)


def _moba(cfg, proj):
    bsz, s, _ = proj.shape
    h, dh, blk = cfg.moba_heads, cfg.moba_head_dim, cfg.moba_block
    assert dh == V7X_LANES and s % blk == 0
    nb = s // blk
    group = math.gcd(cfg.moba_group, nb)
    cos, sin = _rope_tables(s, dh)
    cosf = jnp.concatenate([cos, cos], axis=1)
    sinf = jnp.concatenate([-sin, sin], axis=1)
    qspec = pl.BlockSpec((None, blk, dh), lambda b, hh, i: (b, i, hh))
    kspec = pl.BlockSpec((None, s, dh), lambda b, hh, i: (b, 0, h + hh))
    vspec = pl.BlockSpec((None, s, dh), lambda b, hh, i: (b, 0, 2 * h + hh))
    tq = pl.BlockSpec((blk, dh), lambda b, hh, i: (i, 0))
    tk = pl.BlockSpec((s, dh), lambda b, hh, i: (0, 0))
    return pl.pallas_call(
        functools.partial(_moba_kernel, nb=nb, blk=blk, dh=dh, topk=cfg.moba_topk, group=group),
        grid=(bsz, h, nb),
        in_specs=[qspec, kspec, vspec, tq, tq, tk, tk],
        out_specs=pl.BlockSpec((None, blk, dh), lambda b, hh, i: (b, i, hh)),
        out_shape=jax.ShapeDtypeStruct((bsz, s, h * dh), BF16),
        scratch_shapes=[
            pltpu.VMEM((nb, blk, dh), BF16),
            pltpu.VMEM((nb, dh, blk), BF16),
            pltpu.VMEM((nb, dh), F32),
            pltpu.VMEM((nb, V7X_SUBLANES, blk), F32),
        ],
        compiler_params=_cparams(cfg, "parallel", "parallel", "arbitrary"),
        name="moba_attention",
    )(proj, proj, proj, cosf, sinf, cosf, sinf)


def _ret_kernel(q_ref, k_ref, v_ref, g_ref, cos_ref, sin_ref, dm_ref, qd_ref, kd_ref, cd_ref,
                o_ref, st_ref, *, c, nsub, dk):
    @pl.when(pl.program_id(2) == 0)
    def _():
        st_ref[...] = jnp.zeros_like(st_ref)

    half = dk // 2
    for sidx in range(nsub):
        rows = slice(sidx * c, (sidx + 1) * c)
        cos = cos_ref[rows, :]
        sin = sin_ref[rows, :]

        def rope(x):
            x1, x2 = x[:, :half], x[:, half:]
            return jnp.concatenate([x1 * cos - x2 * sin, x2 * cos + x1 * sin], axis=1)

        q = rope(q_ref[rows, :])
        k = rope(k_ref[rows, :]) * (dk ** -0.5)
        qb, kb = q.astype(BF16), k.astype(BF16)
        vb = v_ref[rows, :].astype(BF16)
        st = st_ref[...]
        inner = _dot_nt(qb, kb) * dm_ref[...]
        o = _dot(inner.astype(BF16), vb) + _dot(qb, st.astype(BF16)) * qd_ref[...]
        kd = (k * kd_ref[...]).astype(BF16)
        st_ref[...] = st * cd_ref[...] + _dot_tn(kd, vb)
        on = o * lax.rsqrt(jnp.mean(o * o, axis=-1, keepdims=True) + NORM_EPS)
        o_ref[rows, :] = (on * _silu(g_ref[rows, :])).astype(o_ref.dtype)


def _retention(cfg, proj):
    bsz, s, _ = proj.shape
    h, dk, dv, c = cfg.ret_heads, cfg.ret_key_dim, cfg.ret_val_dim, cfg.ret_chunk
    ts = min(cfg.ret_rows, s)
    assert s % ts == 0 and ts % c == 0
    q0 = 3 * cfg.dm // dk
    k0 = (3 * cfg.dm + cfg.dk) // dk
    v0 = (3 * cfg.dm + 2 * cfg.dk) // dv
    g0 = (3 * cfg.dm + 2 * cfg.dk + cfg.dv) // dv
    assert (3 * cfg.dm) % dk == 0 and (3 * cfg.dm + 2 * cfg.dk) % dv == 0
    cos, sin = _rope_tables(s, dk)
    log_g = jnp.log1p(-jnp.exp2(-5.0 - jnp.arange(h, dtype=F32)))
    idx = jnp.arange(c, dtype=F32)
    diff = idx[:, None] - idx[None, :]
    dmask = jnp.where(diff >= 0, jnp.exp(jnp.maximum(diff, 0.0) * log_g[:, None, None]), 0.0)
    qdec = jnp.exp((idx + 1.0) * log_g[:, None])[..., None]
    kdec = jnp.exp((c - 1.0 - idx) * log_g[:, None])[..., None]
    cdec = jnp.broadcast_to(jnp.exp(c * log_g)[:, None, None], (h, 1, dv))
    rowspec = lambda w, c0: pl.BlockSpec((None, ts, w), lambda b, hh, i: (b, i, c0 + hh))
    tab = pl.BlockSpec((ts, dk // 2), lambda b, hh, i: (i, 0))
    return pl.pallas_call(
        functools.partial(_ret_kernel, c=c, nsub=ts // c, dk=dk),
        grid=(bsz, h, s // ts),
        in_specs=[
            rowspec(dk, q0), rowspec(dk, k0), rowspec(dv, v0), rowspec(dv, g0), tab, tab,
            pl.BlockSpec((None, c, c), lambda b, hh, i: (hh, 0, 0)),
            pl.BlockSpec((None, c, 1), lambda b, hh, i: (hh, 0, 0)),
            pl.BlockSpec((None, c, 1), lambda b, hh, i: (hh, 0, 0)),
            pl.BlockSpec((None, 1, dv), lambda b, hh, i: (hh, 0, 0)),
        ],
        out_specs=pl.BlockSpec((None, ts, dv), lambda b, hh, i: (b, i, hh)),
        out_shape=jax.ShapeDtypeStruct((bsz, s, h * dv), BF16),
        scratch_shapes=[pltpu.VMEM((dk, dv), F32)],
        compiler_params=_cparams(cfg, "parallel", "parallel", "arbitrary"),
        name="retention",
    )(proj, proj, proj, proj, cos, sin, dmask, qdec, kdec, cdec)


CONV_HALO = 32


def _conv_kernel(a_ref, g_ref, w_ref, b_ref, o_ref, buf_ref, *, ts, kw, rc):
    @pl.when(pl.program_id(2) == 0)
    def _():
        buf_ref[0:CONV_HALO, :] = jnp.zeros((CONV_HALO, buf_ref.shape[1]), F32)

    buf_ref[CONV_HALO:CONV_HALO + ts, :] = a_ref[...] * jax.nn.sigmoid(g_ref[...])
    off = CONV_HALO - (kw - 1)
    bias = b_ref[...]
    for r0 in range(0, ts, rc):
        acc = jnp.broadcast_to(bias, (rc, bias.shape[1]))
        for j in range(kw):
            acc = acc + w_ref[j:j + 1, :] * buf_ref[off + r0 + j:off + r0 + j + rc, :]
        o_ref[r0:r0 + rc, :] = acc
    buf_ref[0:CONV_HALO, :] = buf_ref[ts:ts + CONV_HALO, :]


def _conv_glu(cfg, proj, conv_w, conv_b):
    bsz, s, _ = proj.shape
    ch, kw = cfg.conv_ch, cfg.conv_width
    ts, tc = min(cfg.conv_rows, s), min(cfg.conv_cols, ch)
    assert kw - 1 <= CONV_HALO <= ts and s % ts == 0 and ch % tc == 0
    nct = ch // tc
    wp = jnp.zeros((CONV_HALO, ch), F32).at[:kw].set(conv_w)
    return pl.pallas_call(
        functools.partial(_conv_kernel, ts=ts, kw=kw, rc=32),
        grid=(bsz, nct, s // ts),
        in_specs=[
            pl.BlockSpec((None, ts, tc), lambda b, c, i: (b, i, c)),
            pl.BlockSpec((None, ts, tc), lambda b, c, i: (b, i, nct + c)),
            pl.BlockSpec((CONV_HALO, tc), lambda b, c, i: (0, c)),
            pl.BlockSpec((1, tc), lambda b, c, i: (0, c)),
        ],
        out_specs=pl.BlockSpec((None, ts, tc), lambda b, c, i: (b, i, c)),
        out_shape=jax.ShapeDtypeStruct((bsz, s, ch), F32),
        scratch_shapes=[pltpu.VMEM((CONV_HALO + ts, tc), F32)],
        compiler_params=_cparams(cfg, "parallel", "parallel", "arbitrary"),
        name="glu_causal_conv",
    )(proj, proj, wp, conv_b.reshape(1, ch))


def _ln_silu_kernel(x_ref, g_ref, b_ref, o_ref):
    x = x_ref[...]
    mu = jnp.mean(x, axis=-1, keepdims=True)
    d = x - mu
    var = jnp.mean(d * d, axis=-1, keepdims=True)
    y = d * lax.rsqrt(var + CONV_LN_EPS) * g_ref[...] + b_ref[...]
    o_ref[...] = _silu(y).astype(o_ref.dtype)


def _ln_silu(cfg, x, g, b):
    bsz, s, d = x.shape
    ts = min(cfg.row_tile, s)
    row = pl.BlockSpec((None, ts, d), lambda bb, i: (bb, i, 0))
    vec = pl.BlockSpec((1, d), lambda bb, i: (0, 0))
    return pl.pallas_call(
        _ln_silu_kernel,
        grid=(bsz, s // ts),
        in_specs=[row, vec, vec],
        out_specs=row,
        out_shape=jax.ShapeDtypeStruct((bsz, s, d), BF16),
        compiler_params=_cparams(cfg, "parallel", "parallel"),
        name="layernorm_swish",
    )(x, g.reshape(1, d), b.reshape(1, d))


def _group_ones(n, group):
    r = lax.broadcasted_iota(jnp.int32, (n, n), 0)
    c = lax.broadcasted_iota(jnp.int32, (n, n), 1)
    shift = int(math.log2(group))
    return jnp.where((r >> shift) == (c >> shift), 1.0, 0.0).astype(F32)


def _group_sum(x, gmat):
    n = x.shape[1]
    parts = [_dot(x[:, s0:s0 + V7X_LANES], gmat, precision=HIGHEST) for s0 in range(0, n, V7X_LANES)]
    return parts[0] if len(parts) == 1 else jnp.concatenate(parts, axis=1)


def _rwkv_pre_kernel(r_ref, k_ref, v_ref, lo_ref, mur_ref, muk_ref, muv_ref, mul_ref,
                     w0_ref, wup_ref, a0_ref, aup_ref, gup_ref, kk_ref, ka_ref,
                     ro_ref, lw_ref, ko_ref, vo_ref, ao_ref, bo_ref, go_ref,
                     lr_ref, lk_ref, lv_ref, ll_ref, *, ts, hd, lw_pad):
    first = pl.program_id(1) == 0

    def shift(x_ref, last_ref, mu_ref):
        @pl.when(first)
        def _():
            last_ref[...] = jnp.zeros_like(last_ref)

        x = x_ref[...]
        row = lax.broadcasted_iota(jnp.int32, x.shape, 0)
        prev = jnp.where(row == 0, last_ref[V7X_SUBLANES - 1:V7X_SUBLANES, :], pltpu.roll(x, 1, 0))
        last_ref[...] = x[ts - V7X_SUBLANES:ts, :]
        return x + (prev - x) * mu_ref[...]

    r = shift(r_ref, lr_ref, mur_ref)
    k = shift(k_ref, lk_ref, muk_ref)
    v = shift(v_ref, lv_ref, muv_ref)
    lo = shift(lo_ref, ll_ref, mul_ref)
    xw, xa, xg = lo[:, :lw_pad], lo[:, lw_pad:2 * lw_pad], lo[:, 2 * lw_pad:]

    z = w0_ref[...] + _dot(jnp.tanh(xw), wup_ref[...], precision=HIGHEST)
    softplus = jnp.maximum(-z, 0.0) + jnp.log(1.0 + jnp.exp(-jnp.abs(z)))
    lw_ref[...] = -jnp.exp(-softplus - 0.5)
    a = jax.nn.sigmoid(a0_ref[...] + _dot(xa, aup_ref[...], precision=HIGHEST))
    go_ref[...] = _dot(jax.nn.sigmoid(xg).astype(BF16), gup_ref[...].astype(BF16))

    kkr = k * kk_ref[...]
    ss = _group_sum(kkr * kkr, _group_ones(V7X_LANES, hd))
    kk = kkr / jnp.maximum(jnp.sqrt(ss), 1e-12)
    ro_ref[...] = r
    vo_ref[...] = v
    ko_ref[...] = k * (1.0 + (a - 1.0) * ka_ref[...])
    ao_ref[...] = -kk
    bo_ref[...] = kk * a


def _rwkv_pre(cfg, proj, lora, mu, w0, w_up, a0, a_up, g_up, k_k, k_a):
    bsz, s, _ = proj.shape
    d = cfg.rwkv_dim
    ts = min(cfg.row_tile // 2, s)
    lw_pad = V7X_LANES
    assert cfg.decay_lora <= lw_pad and cfg.iclr_lora <= lw_pad and (2 * cfg.conv_ch) % d == 0
    lo_w = lora.shape[2]
    c0 = 2 * cfg.conv_ch // d
    pad_rows = lambda w: jnp.zeros((lw_pad, d), F32).at[:w.shape[0]].set(w)
    pad_vec = lambda vv, n: jnp.zeros((1, n), F32).at[0, :vv.shape[0]].set(vv)
    mu_r, mu_k, mu_v = (mu[i * d:(i + 1) * d].reshape(1, d) for i in range(3))
    o = 3 * d
    mu_l = jnp.concatenate([
        pad_vec(mu[o:o + cfg.decay_lora], lw_pad),
        pad_vec(mu[o + cfg.decay_lora:o + cfg.decay_lora + cfg.iclr_lora], lw_pad),
        mu[o + cfg.decay_lora + cfg.iclr_lora:].reshape(1, -1)], axis=1)
    row = lambda cb: pl.BlockSpec((None, ts, d), lambda b, i: (b, i, cb))
    lrow = pl.BlockSpec((None, ts, lo_w), lambda b, i: (b, i, 0))
    vec = lambda n: pl.BlockSpec((1, n), lambda b, i: (0, 0))
    mat = lambda rws: pl.BlockSpec((rws, d), lambda b, i: (0, 0))
    orow = pl.BlockSpec((None, ts, d), lambda b, i: (b, i, 0))
    return pl.pallas_call(
        functools.partial(_rwkv_pre_kernel, ts=ts, hd=cfg.rwkv_head_dim, lw_pad=lw_pad),
        grid=(bsz, s // ts),
        in_specs=[row(c0), row(c0 + 1), row(c0 + 2), lrow, vec(d), vec(d), vec(d), vec(lo_w),
                  vec(d), mat(lw_pad), vec(d), mat(lw_pad), mat(cfg.gate_lora), vec(d), vec(d)],
        out_specs=[orow] * 7,
        out_shape=[jax.ShapeDtypeStruct((bsz, s, d), F32)] * 7,
        scratch_shapes=[pltpu.VMEM((V7X_SUBLANES, d), F32)] * 3 + [pltpu.VMEM((V7X_SUBLANES, lo_w), F32)],
        compiler_params=_cparams(cfg, "parallel", "arbitrary"),
        name="rwkv_token_shift_lora",
    )(proj, proj, proj, lora, mu_r, mu_k, mu_v, mu_l, w0.reshape(1, d), pad_rows(w_up),
      a0.reshape(1, d), pad_rows(a_up), g_up, k_k.reshape(1, d), k_a.reshape(1, d))


def _scan_chunk(r, lw, k, v, a, b, st, consts):
    tri, strict_bd, incl_bd, eye, lane_a, bd = consts
    L = r[0].shape[0]
    each = lambda f, *ls: [f(*xs) for xs in zip(*ls)]
    bf = lambda x: x.astype(BF16)
    stack = lambda x: jnp.concatenate([x, x], axis=0)
    unstack = lambda x: jnp.where(lane_a, x[:L], x[L:])
    left, right = (lambda x: x[:, :V7X_LANES]), (lambda x: x[:, V7X_LANES:])

    def split3(x):
        w1 = bf(x)
        e1 = x - w1.astype(F32)
        w2 = bf(e1)
        return jnp.concatenate([w1, w2, bf(e1 - w2.astype(F32))], axis=1)

    cs = each(lambda x: _dot(tri, split3(x)), lw)
    cum = each(lambda x: x[:, :V7X_LANES] + x[:, V7X_LANES:2 * V7X_LANES] + x[:, 2 * V7X_LANES:], cs)
    cl = each(lambda x: x[L - 1:L, :], cum)
    tail = each(lambda x, y: jnp.exp(x - y), cl, cum)
    at = each(lambda x, c, w: x * jnp.exp(c - w), a, cum, lw)
    rt = each(lambda x, c: x * jnp.exp(c), r, cum)
    g_inv = each(lambda c: jnp.exp(-c), cum)
    bt = each(lambda x, g: stack(bf(x * g)), b, g_inv)
    kt = each(lambda x, g: stack(bf(x * g)), k, g_inv)
    lhs = each(lambda x, y: jnp.concatenate(
        [bf(jnp.where(lane_a, x, 0.0)), bf(jnp.where(lane_a, 0.0, x)),
         bf(jnp.where(lane_a, y, 0.0)), bf(jnp.where(lane_a, 0.0, y))], axis=0), at, rt)
    xb = each(_dot_nt, lhs, bt)
    xk = each(_dot_nt, lhs, kt)
    n = each(lambda x: jnp.where(strict_bd, x[:2 * L], 0.0), xb)
    m = each(lambda x: bf(jnp.where(strict_bd, x[:2 * L], 0.0)), xk)
    p = each(lambda x: bf(jnp.where(incl_bd, x[2 * L:], 0.0)), xb)
    q = each(lambda x: bf(jnp.where(incl_bd, x[2 * L:], 0.0)), xk)
    vb = each(bf, v)
    v_st = each(stack, vb)
    mv = each(_dot, m, v_st)
    w = each(lambda x: eye + x, n)
    pw = each(bf, n)
    for _ in range(int(math.log2(L)) - 1):
        pw = each(lambda x: bf(_dot(x, x)), pw)
        w = each(lambda x, y: x + _dot(bf(x), y), w, pw)
    au = each(lambda ww, x, y: _dot(bf(ww), jnp.concatenate([stack(bf(x)), bf(y)], axis=1)),
              w, at, mv)
    pau = each(lambda x, y: _dot(x, bf(y)), p, au)
    qv = each(_dot, q, v_st)
    rbar = each(lambda x, y: bf(x + unstack(left(y))), rt, pau)
    ybar = each(lambda x, y: unstack(right(x) + y), pau, qv)
    bh = each(lambda x, t: stack(bf(x * t)), b, tail)
    kh = each(lambda x, t: bf(x * t), k, tail)
    abar = each(lambda x: bf(jnp.where(bd, left(x), 0.0)), au)
    ubar = each(lambda x: bf(jnp.where(bd, right(x), 0.0)), au)
    tt = each(lambda x, y: bf(jnp.where(bd, _dot_tn(x, y), 0.0)), abar, bh)
    z = each(lambda u, vv, x, y: jnp.where(bd, _dot_tn(jnp.concatenate([u, vv], axis=0),
                                                       jnp.concatenate([x, y], axis=0)), 0.0),
             ubar, vb, bh, kh)
    sb = each(bf, st)
    y = each(lambda x, s, yb: _dot_nt(x, s) + yb, rbar, sb, ybar)
    st_new = each(lambda s, c, s16, t, zz: s * jnp.exp(c) + _dot(s16, t) + zz, st, cl, sb, tt, z)
    return y, st_new


def _scan_kernel(r_ref, lw_ref, k_ref, v_ref, a_ref, b_ref, y_ref, st_ref, *, ts, L, pairs, hd):
    @pl.when(pl.program_id(2) == 0)
    def _():
        st_ref[...] = jnp.zeros_like(st_ref)

    ri = lax.broadcasted_iota(jnp.int32, (L, L), 0)
    ci = lax.broadcasted_iota(jnp.int32, (L, L), 1)
    lane = lax.broadcasted_iota(jnp.int32, (1, V7X_LANES), 1)
    r2 = lax.broadcasted_iota(jnp.int32, (2 * L, 2 * L), 0)
    c2 = lax.broadcasted_iota(jnp.int32, (2 * L, 2 * L), 1)
    bd = (r2 < L) == (c2 < L)
    rl, cl2 = r2 & (L - 1), c2 & (L - 1)
    consts = (jnp.where(ri >= ci, 1.0, 0.0).astype(BF16), bd & (rl > cl2), bd & (rl >= cl2),
              jnp.where(r2 == c2, 1.0, 0.0), lane < hd, bd)

    def body(c, carry):
        rows = pl.ds(pl.multiple_of(c * L, L), L)
        cols = [slice(p * V7X_LANES, (p + 1) * V7X_LANES) for p in range(pairs)]
        load = lambda ref: [ref[rows, cs] for cs in cols]
        ys, sts = _scan_chunk(load(r_ref), load(lw_ref), load(k_ref), load(v_ref), load(a_ref),
                              load(b_ref), [st_ref[p] for p in range(pairs)], consts)
        for p in range(pairs):
            y_ref[rows, cols[p]] = ys[p]
            st_ref[p] = sts[p]
        return carry

    lax.fori_loop(0, ts // L, body, 0)


def _rwkv_scan(cfg, r, lw, k, v, a, b):
    bsz, s, d = r.shape
    hd = cfg.rwkv_head_dim
    assert 2 * hd == V7X_LANES
    ts, L = min(cfg.scan_rows, s), cfg.scan_chunk
    npairs = d // V7X_LANES
    pairs = min(cfg.scan_pairs, npairs)
    assert s % ts == 0 and ts % L == 0 and npairs % pairs == 0
    blk = pl.BlockSpec((None, ts, pairs * V7X_LANES), lambda bb, p, i: (bb, i, p))
    return pl.pallas_call(
        functools.partial(_scan_kernel, ts=ts, L=L, pairs=pairs, hd=hd),
        grid=(bsz, npairs // pairs, s // ts),
        in_specs=[blk] * 6,
        out_specs=blk,
        out_shape=jax.ShapeDtypeStruct((bsz, s, d), F32),
        scratch_shapes=[pltpu.VMEM((pairs, V7X_LANES, V7X_LANES), F32)],
        compiler_params=_cparams(cfg, "parallel", "parallel", "arbitrary"),
        name="rwkv7_scan",
    )(r, lw, k, v, a, b)


def _rwkv_post_kernel(y_ref, r_ref, k_ref, v_ref, g_ref, rk_ref, lg_ref, lb_ref, o_ref, *, hd):
    gmat = _group_ones(V7X_LANES, hd)
    y = y_ref[...]
    mu = _group_sum(y, gmat) * (1.0 / hd)
    d = y - mu
    var = _group_sum(d * d, gmat) * (1.0 / hd)
    yn = d * lax.rsqrt(var + RWKV_LNX_EPS) * lg_ref[...] + lb_ref[...]
    bonus = _group_sum(r_ref[...] * k_ref[...] * rk_ref[...], gmat) * v_ref[...]
    o_ref[...] = ((yn + bonus) * g_ref[...]).astype(o_ref.dtype)


def _rwkv_post(cfg, y, r, k, v, g, r_k, lnx_g, lnx_b):
    bsz, s, d = y.shape
    ts = min(cfg.row_tile, s)
    row = pl.BlockSpec((None, ts, d), lambda b, i: (b, i, 0))
    vec = pl.BlockSpec((1, d), lambda b, i: (0, 0))
    return pl.pallas_call(
        functools.partial(_rwkv_post_kernel, hd=cfg.rwkv_head_dim),
        grid=(bsz, s // ts),
        in_specs=[row] * 5 + [vec] * 3,
        out_specs=row,
        out_shape=jax.ShapeDtypeStruct((bsz, s, d), BF16),
        compiler_params=_cparams(cfg, "parallel", "parallel"),
        name="rwkv_groupnorm_gate",
    )(y, r, k, v, g, r_k.reshape(1, d), lnx_g.reshape(1, d), lnx_b.reshape(1, d))


def _even_mixer(cfg, h, w_in, w_out):
    bsz, s, d = h.shape
    proj = _matmul(cfg, h.reshape(bsz * s, d), w_in, 0, 0, cfg.even_in, F32,
                   cfg.mm_tm, cfg.mm_tn, d).reshape(bsz, s, cfg.even_in)
    merged = jnp.concatenate([_moba(cfg, proj), _retention(cfg, proj)], axis=-1)
    k_out = merged.shape[-1]
    return _matmul(cfg, merged.reshape(bsz * s, k_out), w_out, 0, 0, d, F32,
                   cfg.mm_tm, cfg.mm_tn // 2, k_out).reshape(bsz, s, d)


def _odd_mixer(cfg, h, w_in, w_out, conv_w, conv_b, conv_ln_g, conv_ln_b, mu, w0, w_up, a0, a_up,
               g_up, k_k, k_a, r_k, lnx_g, lnx_b):
    bsz, s, d = h.shape
    h2 = h.reshape(bsz * s, d)
    proj = _matmul(cfg, h2, w_in, 0, 0, cfg.odd_main, F32,
                   cfg.mm_tm, cfg.mm_tn, d).reshape(bsz, s, cfg.odd_main)
    lw_pad = V7X_LANES
    wl = w_in[0, :, cfg.odd_main:]
    zc = lambda n: jnp.zeros((d, n), F32)
    o1, o2 = cfg.decay_lora, cfg.decay_lora + cfg.iclr_lora
    wl = jnp.concatenate([wl[:, :o1], zc(lw_pad - cfg.decay_lora), wl[:, o1:o2],
                          zc(lw_pad - cfg.iclr_lora), wl[:, o2:]], axis=1)[None]
    lo_w = wl.shape[2]
    lora = _matmul(cfg, h2, wl, 0, 0, lo_w, F32, cfg.mm_tm, lo_w, d).reshape(bsz, s, lo_w)

    u = _ln_silu(cfg, _conv_glu(cfg, proj, conv_w, conv_b), conv_ln_g, conv_ln_b)
    r, lw, k, v, a, b, g = _rwkv_pre(cfg, proj, lora, mu, w0, w_up, a0, a_up, g_up, k_k, k_a)
    y = _rwkv_scan(cfg, r, lw, k, v, a, b)
    y = _rwkv_post(cfg, y, r, k, v, g, r_k.reshape(-1), lnx_g, lnx_b)
    merged = jnp.concatenate([u, y], axis=-1)
    k_out = merged.shape[-1]
    return _matmul(cfg, merged.reshape(bsz * s, k_out), w_out, 0, 0, d, F32,
                   cfg.mm_tm, cfg.mm_tn, k_out).reshape(bsz, s, d)


def _forward(cfg, x, c, w_ada, b_ada, norm_g, w_ffn_in, w_ffn_out, even_w_in, even_w_out, odd_w_in,
             odd_w_out, conv_w, conv_b, conv_ln_g, conv_ln_b, rwkv_mu, rwkv_w0, rwkv_w_up, rwkv_a0,
             rwkv_a_up, rwkv_g_up, rwkv_k_k, rwkv_k_a, rwkv_r_k, rwkv_lnx_g, rwkv_lnx_b):
    bsz, s, d = x.shape
    depth = w_ada.shape[0]
    mods = _modulation(cfg, c, w_ada, b_ada)
    w_ffn_out_bf = w_ffn_out.astype(BF16)
    sh_m, sc_m = mods[0, :, 0], mods[0, :, 1]
    h = _norm_mod(cfg, x, norm_g[0, 0], sc_m, sh_m)
    for layer in range(depth):
        g_m, sh_f, sc_f, g_f = (mods[layer, :, i] for i in (2, 3, 4, 5))
        j = layer // 2
        if layer % 2 == 0:
            o = _even_mixer(cfg, h, even_w_in[j:j + 1], even_w_out[j:j + 1])
        else:
            o = _odd_mixer(cfg, h, odd_w_in[j:j + 1], odd_w_out[j:j + 1], conv_w[j], conv_b[j],
                           conv_ln_g[j], conv_ln_b[j], rwkv_mu[j], rwkv_w0[j], rwkv_w_up[j],
                           rwkv_a0[j], rwkv_a_up[j], rwkv_g_up[j], rwkv_k_k[j], rwkv_k_a[j],
                           rwkv_r_k[j], rwkv_lnx_g[j], rwkv_lnx_b[j])
        x, h = _resid(cfg, x, o, norm_g[layer, 1], g_m, (norm_g[layer, 2], sc_f, sh_f))
        act = _ffn_in(cfg, h.reshape(bsz * s, d), w_ffn_in, layer)
        f = _matmul(cfg, act, w_ffn_out_bf, layer, 0, d, F32, cfg.ffn_out_tm, cfg.mm_tn,
                    act.shape[1]).reshape(bsz, s, d)
        if layer + 1 < depth:
            nxt = (norm_g[layer + 1, 0], mods[layer + 1, :, 1], mods[layer + 1, :, 0])
            x, h = _resid(cfg, x, f, norm_g[layer, 3], g_f, nxt)
        else:
            x = _resid(cfg, x, f, norm_g[layer, 3], g_f)
    return x


def kernel(x, c, w_ada, b_ada, norm_g, w_ffn_in, w_ffn_out, even_w_in, even_w_out, odd_w_in, odd_w_out, conv_w, conv_b, conv_ln_g, conv_ln_b, rwkv_mu, rwkv_w0, rwkv_w_up, rwkv_a0, rwkv_a_up, rwkv_g_up, rwkv_k_k, rwkv_k_a, rwkv_r_k, rwkv_lnx_g, rwkv_lnx_b):
    return _forward(Config(), x, c, w_ada, b_ada, norm_g, w_ffn_in, w_ffn_out, even_w_in, even_w_out,
                    odd_w_in, odd_w_out, conv_w, conv_b, conv_ln_g, conv_ln_b, rwkv_mu, rwkv_w0,
                    rwkv_w_up, rwkv_a0, rwkv_a_up, rwkv_g_up, rwkv_k_k, rwkv_k_a, rwkv_r_k,
                    rwkv_lnx_g, rwkv_lnx_b)
```

```python
import dataclasses
import functools
import math

import jax
import jax.numpy as jnp
from jax import lax
from jax.experimental import pallas as pl
from jax.experimental.pallas import tpu as pltpu

F32 = jnp.float32
BF16 = jnp.bfloat16
HIGHEST = lax.Precision.HIGHEST

V7X_LANES = 128
V7X_SUBLANES = 8
MIB = 1024 * 1024
NORM_EPS = 1e-6
ROPE_THETA = 10000.0
CONV_LN_EPS = 1e-5
RWKV_LNX_EPS = 64e-5


@dataclasses.dataclass(frozen=True)
class Config:
    d_model: int = 4096
    moba_heads: int = 16
    moba_head_dim: int = 128
    moba_block: int = 256
    moba_topk: int = 3
    ret_heads: int = 8
    ret_key_dim: int = 256
    ret_val_dim: int = 512
    ret_chunk: int = 128
    conv_ch: int = 2048
    conv_width: int = 31
    rwkv_dim: int = 2048
    rwkv_head_dim: int = 64
    decay_lora: int = 96
    iclr_lora: int = 96
    gate_lora: int = 256
    ffn_hidden: int = 11008
    row_tile: int = 256
    mm_tm: int = 1024
    mm_tn: int = 512
    ffn_tn: int = 256
    ffn_out_tm: int = 512
    ret_rows: int = 512
    conv_rows: int = 256
    conv_cols: int = 256
    scan_rows: int = 512
    scan_chunk: int = 64
    scan_pairs: int = 8
    moba_group: int = 4
    moba_heads_per_step: int = 4
    vmem_mib: int = 56

    @property
    def dm(self):
        return self.moba_heads * self.moba_head_dim

    @property
    def dk(self):
        return self.ret_heads * self.ret_key_dim

    @property
    def dv(self):
        return self.ret_heads * self.ret_val_dim

    @property
    def even_in(self):
        return 3 * self.dm + 2 * self.dk + 2 * self.dv

    @property
    def lora_in(self):
        return self.decay_lora + self.iclr_lora + self.gate_lora

    @property
    def odd_main(self):
        return 2 * self.conv_ch + 3 * self.rwkv_dim


def _cparams(cfg, *sem):
    return pltpu.CompilerParams(dimension_semantics=sem, vmem_limit_bytes=cfg.vmem_mib * MIB)


def _silu(x):
    return x * jax.nn.sigmoid(x)


def _dot(a, b, **kw):
    return jnp.dot(a, b, preferred_element_type=F32, **kw)


def _dot_nt(a, b, **kw):
    return lax.dot_general(a, b, (((1,), (1,)), ((), ())), preferred_element_type=F32, **kw)


def _dot_tn(a, b, **kw):
    return lax.dot_general(a, b, (((0,), (0,)), ((), ())), preferred_element_type=F32, **kw)


def _ada_kernel(c_ref, w_ref, b_ref, o_ref):
    s = _silu(c_ref[...])
    hi = s.astype(BF16).astype(F32)
    parts = _dot(jnp.concatenate([hi, s - hi], axis=0).astype(BF16), w_ref[...].astype(BF16))
    o_ref[...] = parts[:V7X_SUBLANES] + parts[V7X_SUBLANES:] + b_ref[...]


def _modulation(cfg, c, w_ada, b_ada, tn=512):
    depth, d, n = w_ada.shape
    bsz = c.shape[0]
    cp = jnp.zeros((V7X_SUBLANES, d), F32).at[:bsz].set(c)
    out = pl.pallas_call(
        _ada_kernel,
        grid=(depth, n // tn),
        in_specs=[
            pl.BlockSpec((V7X_SUBLANES, d), lambda l, j: (0, 0)),
            pl.BlockSpec((None, d, tn), lambda l, j: (l, 0, j)),
            pl.BlockSpec((None, 1, tn), lambda l, j: (l, 0, j)),
        ],
        out_specs=pl.BlockSpec((None, V7X_SUBLANES, tn), lambda l, j: (l, 0, j)),
        out_shape=jax.ShapeDtypeStruct((depth, V7X_SUBLANES, n), F32),
        compiler_params=_cparams(cfg, "parallel", "parallel"),
        name="adaln_modulation",
    )(cp, w_ada, b_ada.reshape(depth, 1, n))
    return out[:, :bsz].reshape(depth, bsz, 6, 1, d)


def _rms(x, g):
    return x * lax.rsqrt(jnp.mean(x * x, axis=-1, keepdims=True) + NORM_EPS) * g


def _norm_mod_kernel(x_ref, g_ref, sc_ref, sh_ref, o_ref):
    y = _rms(x_ref[...], g_ref[...])
    o_ref[...] = (y * (1.0 + sc_ref[...]) + sh_ref[...]).astype(o_ref.dtype)


def _norm_mod(cfg, x, g, sc, sh):
    bsz, s, d = x.shape
    ts = min(cfg.row_tile, s)
    row = pl.BlockSpec((None, ts, d), lambda b, i: (b, i, 0))
    vec = pl.BlockSpec((1, d), lambda b, i: (0, 0))
    mod = pl.BlockSpec((None, 1, d), lambda b, i: (b, 0, 0))
    return pl.pallas_call(
        _norm_mod_kernel,
        grid=(bsz, s // ts),
        in_specs=[row, vec, mod, mod],
        out_specs=row,
        out_shape=jax.ShapeDtypeStruct((bsz, s, d), BF16),
        compiler_params=_cparams(cfg, "parallel", "parallel"),
        name="norm_modulate",
    )(x, g.reshape(1, d), sc, sh)


def _resid_kernel(x_ref, o_ref, ga_ref, gate_ref, *rest, with_h):
    xn = x_ref[...] + gate_ref[...] * _rms(o_ref[...], ga_ref[...])
    if with_h:
        gb_ref, sc_ref, sh_ref, xn_ref, h_ref = rest
        xn_ref[...] = xn
        h_ref[...] = (_rms(xn, gb_ref[...]) * (1.0 + sc_ref[...]) + sh_ref[...]).astype(h_ref.dtype)
    else:
        (xn_ref,) = rest
        xn_ref[...] = xn


def _resid(cfg, x, o, ga, gate, nxt=None):
    bsz, s, d = x.shape
    ts = min(cfg.row_tile, s)
    row = pl.BlockSpec((None, ts, d), lambda b, i: (b, i, 0))
    vec = pl.BlockSpec((1, d), lambda b, i: (0, 0))
    mod = pl.BlockSpec((None, 1, d), lambda b, i: (b, 0, 0))
    with_h = nxt is not None
    in_specs = [row, row, vec, mod]
    args = [x, o, ga.reshape(1, d), gate]
    out_specs = [row]
    out_shape = [jax.ShapeDtypeStruct((bsz, s, d), F32)]
    if with_h:
        gb, sc, sh = nxt
        in_specs += [vec, mod, mod]
        args += [gb.reshape(1, d), sc, sh]
        out_specs.append(row)
        out_shape.append(jax.ShapeDtypeStruct((bsz, s, d), BF16))
    outs = pl.pallas_call(
        functools.partial(_resid_kernel, with_h=with_h),
        grid=(bsz, s // ts),
        in_specs=in_specs,
        out_specs=out_specs,
        out_shape=out_shape,
        compiler_params=_cparams(cfg, "parallel", "parallel"),
        name="residual_norm",
    )(*args)
    return outs if with_h else outs[0]


def _mm_kernel(*refs, widths):
    x_refs, (w_ref, o_ref) = refs[:len(widths)], refs[len(widths):]
    acc, off = None, 0
    for x_ref, wd in zip(x_refs, widths):
        part = _dot(x_ref[...], w_ref[off:off + wd, :].astype(BF16))
        acc = part if acc is None else acc + part
        off += wd
    o_ref[...] = acc.astype(o_ref.dtype)


def _matmul(cfg, xs, w, layer, n, out_dtype, tm, tn):
    m = xs[0].shape[0]
    widths = tuple(x.shape[1] for x in xs)
    kdim = sum(widths)
    tm, tn = min(tm, m), min(tn, n)
    assert m % tm == 0 and n % tn == 0 and w.shape[1] == kdim
    return pl.pallas_call(
        functools.partial(_mm_kernel, widths=widths),
        grid=(m // tm, n // tn),
        in_specs=[pl.BlockSpec((tm, wd), lambda i, j: (i, 0)) for wd in widths]
        + [pl.BlockSpec((None, kdim, tn), lambda i, j: (layer, 0, j))],
        out_specs=pl.BlockSpec((tm, tn), lambda i, j: (i, j)),
        out_shape=jax.ShapeDtypeStruct((m, n), out_dtype),
        compiler_params=_cparams(cfg, "parallel", "parallel"),
        name="matmul",
    )(*xs, w)


def _ffn_in_kernel(x_ref, wg_ref, wu_ref, o_ref):
    x = x_ref[...]
    gate = _dot(x, wg_ref[...].astype(BF16))
    up = _dot(x, wu_ref[...].astype(BF16))
    o_ref[...] = (_silu(gate) * up).astype(o_ref.dtype)


def _ffn_in(cfg, x, w, layer):
    m, kdim = x.shape
    hid = w.shape[2] // 2
    tm, tn = min(cfg.mm_tm, m), min(cfg.ffn_tn, hid)
    nt = hid // tn
    assert m % tm == 0 and hid % tn == 0
    return pl.pallas_call(
        _ffn_in_kernel,
        grid=(m // tm, nt),
        in_specs=[
            pl.BlockSpec((tm, kdim), lambda i, j: (i, 0)),
            pl.BlockSpec((None, kdim, tn), lambda i, j: (layer, 0, j)),
            pl.BlockSpec((None, kdim, tn), lambda i, j: (layer, 0, nt + j)),
        ],
        out_specs=pl.BlockSpec((tm, tn), lambda i, j: (i, j)),
        out_shape=jax.ShapeDtypeStruct((m, hid), BF16),
        compiler_params=_cparams(cfg, "parallel", "parallel"),
        name="ffn_in_swiglu",
    )(x, w, w)


def _moba_kernel(q_ref, k_ref, v_ref, cq_ref, sq_ref, ck_ref, sk_ref, o_ref,
                 kr_ref, vt_ref, km_ref, sel_ref, *, nb, blk, dh, topk, group, hp):
    qi = pl.program_id(2)
    half = dh // 2
    heads = range(hp)
    lanes = [slice(n * dh, (n + 1) * dh) for n in heads]
    each = lambda f, *ls: [f(*xs) for xs in zip(*ls)]

    @pl.when(qi == 0)
    def _():
        for j in range(nb):
            rows = slice(j * blk, (j + 1) * blk)
            for n in heads:
                kb = k_ref[rows, lanes[n]]
                kr = kb * ck_ref[rows, :] + pltpu.roll(kb, half, 1) * sk_ref[rows, :]
                km_ref[n, j:j + 1, :] = jnp.mean(kr, axis=0, keepdims=True)
                kr_ref[n, j] = kr.astype(BF16)
                vt_ref[n, j] = v_ref[rows, lanes[n]].T.astype(BF16)

    cq, sq = cq_ref[...], sq_ref[...]
    qr = [(lambda q: q * cq + pltpu.roll(q, half, 1) * sq)(q_ref[:, lanes[n]]) for n in heads]

    gate = [_dot_nt(km_ref[n], qr[n], precision=HIGHEST) for n in heads]
    brow = lax.broadcasted_iota(jnp.int32, (nb, blk), 0)
    gm = each(lambda g: jnp.where(brow < qi, g, -jnp.inf), gate)
    for j in range(nb):
        below = jnp.where(brow < j, 1.0, 0.0)

        def keep_row(g):
            gj = g[j:j + 1, :]
            beats = jnp.where(g > gj, 1.0, 0.0) + jnp.where(g == gj, below, 0.0)
            keep = jnp.where(jnp.sum(beats, axis=0, keepdims=True) < topk, 1.0, 0.0)
            return jnp.broadcast_to(jnp.where(j < qi, keep, 0.0), (V7X_SUBLANES, blk))

        for n, row in enumerate(each(keep_row, gm)):
            sel_ref[n, j] = row

    qs = each(lambda x: (x * (dh ** -0.5)).T.astype(BF16), qr)
    kpos = lax.broadcasted_iota(jnp.int32, (blk, blk), 0)
    qpos = lax.broadcasted_iota(jnp.int32, (blk, blk), 1)
    colmax = lambda x: jnp.max(x, axis=0, keepdims=True)
    colsum = lambda x: jnp.sum(x, axis=0, keepdims=True)
    s = [jnp.where(kpos <= qpos, _dot(kr_ref[n, qi], qs[n]), -jnp.inf) for n in heads]
    m = each(colmax, s)
    p = each(lambda x, y: jnp.exp(x - y), s, m)
    l = each(colsum, p)
    acc = [_dot(vt_ref[n, qi], p[n].astype(BF16)) for n in heads]

    def body(g, carry):
        m, l, acc = carry
        js = [g * group + u for u in range(group)]
        ss = [[jnp.where(sel_ref[n, j][0:1, :] > 0.0, _dot(kr_ref[n, j], qs[n]), -jnp.inf)
               for n in heads] for j in js]
        m_new = list(m)
        for su in ss:
            m_new = each(lambda x, y: jnp.maximum(x, colmax(y)), m_new, su)
        alpha = each(lambda x, y: jnp.exp(x - y), m, m_new)
        l = each(lambda x, y: x * y, alpha, l)
        acc = each(lambda x, y: x * y, alpha, acc)
        for j, su in zip(js, ss):
            p = each(lambda x, y: jnp.exp(x - y), su, m_new)
            l = each(lambda x, y: x + colsum(y), l, p)
            acc = [acc[n] + _dot(vt_ref[n, j], p[n].astype(BF16)) for n in heads]
        return tuple(m_new), tuple(l), tuple(acc)

    m, l, acc = lax.fori_loop(0, (qi + group - 1) // group, body, (tuple(m), tuple(l), tuple(acc)))
    for n in heads:
        o_ref[:, lanes[n]] = (acc[n] / l[n]).T.astype(o_ref.dtype)


def _rope_tables(seq, dim):
    inv = 1.0 / (ROPE_THETA ** (jnp.arange(0, dim, 2, dtype=F32) / dim))
    ang = jnp.arange(seq, dtype=F32)[:, None] * inv[None, :]
    return jnp.cos(ang), jnp.sin(ang)


def _moba(cfg, proj):
    bsz, s, _ = proj.shape
    h, dh, blk = cfg.moba_heads, cfg.moba_head_dim, cfg.moba_block
    assert dh == V7X_LANES and s % blk == 0
    nb = s // blk
    group = math.gcd(cfg.moba_group, nb)
    hp = math.gcd(cfg.moba_heads_per_step, h)
    hg = h // hp
    cos, sin = _rope_tables(s, dh)
    cosf = jnp.concatenate([cos, cos], axis=1)
    sinf = jnp.concatenate([-sin, sin], axis=1)
    qspec = pl.BlockSpec((None, blk, hp * dh), lambda b, hh, i: (b, i, hh))
    kspec = pl.BlockSpec((None, s, hp * dh), lambda b, hh, i: (b, 0, hg + hh))
    vspec = pl.BlockSpec((None, s, hp * dh), lambda b, hh, i: (b, 0, 2 * hg + hh))
    tq = pl.BlockSpec((blk, dh), lambda b, hh, i: (i, 0))
    tk = pl.BlockSpec((s, dh), lambda b, hh, i: (0, 0))
    return pl.pallas_call(
        functools.partial(_moba_kernel, nb=nb, blk=blk, dh=dh, topk=cfg.moba_topk, group=group,
                          hp=hp),
        grid=(bsz, hg, nb),
        in_specs=[qspec, kspec, vspec, tq, tq, tk, tk],
        out_specs=pl.BlockSpec((None, blk, hp * dh), lambda b, hh, i: (b, i, hh)),
        out_shape=jax.ShapeDtypeStruct((bsz, s, h * dh), BF16),
        scratch_shapes=[
            pltpu.VMEM((hp, nb, blk, dh), BF16),
            pltpu.VMEM((hp, nb, dh, blk), BF16),
            pltpu.VMEM((hp, nb, dh), F32),
            pltpu.VMEM((hp, nb, V7X_SUBLANES, blk), F32),
        ],
        compiler_params=_cparams(cfg, "parallel", "parallel", "arbitrary"),
        name="moba_attention",
    )(proj, proj, proj, cosf, sinf, cosf, sinf)


def _ret_kernel(q_ref, k_ref, v_ref, g_ref, cos_ref, sin_ref, dm_ref, qd_ref, kd_ref, cd_ref,
                o_ref, st_ref, *, c, nsub, dk):
    @pl.when(pl.program_id(2) == 0)
    def _():
        st_ref[...] = jnp.zeros_like(st_ref)

    half = dk // 2
    for sidx in range(nsub):
        rows = slice(sidx * c, (sidx + 1) * c)
        cos = cos_ref[rows, :]
        sin = sin_ref[rows, :]

        def rope(x):
            x1, x2 = x[:, :half], x[:, half:]
            return jnp.concatenate([x1 * cos - x2 * sin, x2 * cos + x1 * sin], axis=1)

        q = rope(q_ref[rows, :])
        k = rope(k_ref[rows, :]) * (dk ** -0.5)
        qb, kb = q.astype(BF16), k.astype(BF16)
        vb = v_ref[rows, :].astype(BF16)
        st = st_ref[...]
        inner = _dot_nt(qb, kb) * dm_ref[...]
        o = _dot(inner.astype(BF16), vb) + _dot(qb, st.astype(BF16)) * qd_ref[...]
        kd = (k * kd_ref[...]).astype(BF16)
        st_ref[...] = st * cd_ref[...] + _dot_tn(kd, vb)
        on = o * lax.rsqrt(jnp.mean(o * o, axis=-1, keepdims=True) + NORM_EPS)
        o_ref[rows, :] = (on * _silu(g_ref[rows, :])).astype(o_ref.dtype)


def _retention(cfg, proj):
    bsz, s, _ = proj.shape
    h, dk, dv, c = cfg.ret_heads, cfg.ret_key_dim, cfg.ret_val_dim, cfg.ret_chunk
    ts = min(cfg.ret_rows, s)
    assert s % ts == 0 and ts % c == 0
    q0 = 3 * cfg.dm // dk
    k0 = (3 * cfg.dm + cfg.dk) // dk
    v0 = (3 * cfg.dm + 2 * cfg.dk) // dv
    g0 = (3 * cfg.dm + 2 * cfg.dk + cfg.dv) // dv
    assert (3 * cfg.dm) % dk == 0 and (3 * cfg.dm + 2 * cfg.dk) % dv == 0
    cos, sin = _rope_tables(s, dk)
    log_g = jnp.log1p(-jnp.exp2(-5.0 - jnp.arange(h, dtype=F32)))
    idx = jnp.arange(c, dtype=F32)
    diff = idx[:, None] - idx[None, :]
    dmask = jnp.where(diff >= 0, jnp.exp(jnp.maximum(diff, 0.0) * log_g[:, None, None]), 0.0)
    qdec = jnp.exp((idx + 1.0) * log_g[:, None])[..., None]
    kdec = jnp.exp((c - 1.0 - idx) * log_g[:, None])[..., None]
    cdec = jnp.broadcast_to(jnp.exp(c * log_g)[:, None, None], (h, 1, dv))
    rowspec = lambda w, c0: pl.BlockSpec((None, ts, w), lambda b, hh, i: (b, i, c0 + hh))
    tab = pl.BlockSpec((ts, dk // 2), lambda b, hh, i: (i, 0))
    return pl.pallas_call(
        functools.partial(_ret_kernel, c=c, nsub=ts // c, dk=dk),
        grid=(bsz, h, s // ts),
        in_specs=[
            rowspec(dk, q0), rowspec(dk, k0), rowspec(dv, v0), rowspec(dv, g0), tab, tab,
            pl.BlockSpec((None, c, c), lambda b, hh, i: (hh, 0, 0)),
            pl.BlockSpec((None, c, 1), lambda b, hh, i: (hh, 0, 0)),
            pl.BlockSpec((None, c, 1), lambda b, hh, i: (hh, 0, 0)),
            pl.BlockSpec((None, 1, dv), lambda b, hh, i: (hh, 0, 0)),
        ],
        out_specs=pl.BlockSpec((None, ts, dv), lambda b, hh, i: (b, i, hh)),
        out_shape=jax.ShapeDtypeStruct((bsz, s, h * dv), BF16),
        scratch_shapes=[pltpu.VMEM((dk, dv), F32)],
        compiler_params=_cparams(cfg, "parallel", "parallel", "arbitrary"),
        name="retention",
    )(proj, proj, proj, proj, cos, sin, dmask, qdec, kdec, cdec)


CONV_HALO = 32


def _conv_kernel(a_ref, g_ref, w_ref, b_ref, o_ref, buf_ref, sh_ref, *, ts, kw, rc):
    @pl.when(pl.program_id(2) == 0)
    def _():
        buf_ref[0:CONV_HALO, :] = jnp.zeros((CONV_HALO, buf_ref.shape[1]), F32)

    buf_ref[CONV_HALO:CONV_HALO + ts, :] = a_ref[...] * jax.nn.sigmoid(g_ref[...])
    nshift = CONV_HALO + ts - V7X_SUBLANES
    for ph in range(1, V7X_SUBLANES):
        sh_ref[ph, 0:nshift, :] = buf_ref[ph:ph + nshift, :]
    off = CONV_HALO - (kw - 1)
    bias = b_ref[...]
    for r0 in range(0, ts, rc):
        acc = jnp.broadcast_to(bias, (rc, bias.shape[1]))
        for j in range(kw):
            ph = (off + j) % V7X_SUBLANES
            base = off + j - ph + r0
            rows = buf_ref[base:base + rc, :] if ph == 0 else sh_ref[ph, base:base + rc, :]
            acc = acc + w_ref[j:j + 1, :] * rows
        o_ref[r0:r0 + rc, :] = acc
    buf_ref[0:CONV_HALO, :] = buf_ref[ts:ts + CONV_HALO, :]


def _conv_glu(cfg, proj, conv_w, conv_b):
    bsz, s, _ = proj.shape
    ch, kw = cfg.conv_ch, cfg.conv_width
    ts, tc = min(cfg.conv_rows, s), min(cfg.conv_cols, ch)
    assert kw - 1 <= CONV_HALO <= ts and s % ts == 0 and ch % tc == 0
    nct = ch // tc
    wp = jnp.zeros((CONV_HALO, ch), F32).at[:kw].set(conv_w)
    return pl.pallas_call(
        functools.partial(_conv_kernel, ts=ts, kw=kw, rc=32),
        grid=(bsz, nct, s // ts),
        in_specs=[
            pl.BlockSpec((None, ts, tc), lambda b, c, i: (b, i, c)),
            pl.BlockSpec((None, ts, tc), lambda b, c, i: (b, i, nct + c)),
            pl.BlockSpec((CONV_HALO, tc), lambda b, c, i: (0, c)),
            pl.BlockSpec((1, tc), lambda b, c, i: (0, c)),
        ],
        out_specs=pl.BlockSpec((None, ts, tc), lambda b, c, i: (b, i, c)),
        out_shape=jax.ShapeDtypeStruct((bsz, s, ch), F32),
        scratch_shapes=[pltpu.VMEM((CONV_HALO + ts, tc), F32),
                        pltpu.VMEM((V7X_SUBLANES, CONV_HALO + ts, tc), F32)],
        compiler_params=_cparams(cfg, "parallel", "parallel", "arbitrary"),
        name="glu_causal_conv",
    )(proj, proj, wp, conv_b.reshape(1, ch))


def _ln_silu_kernel(x_ref, g_ref, b_ref, o_ref):
    x = x_ref[...]
    mu = jnp.mean(x, axis=-1, keepdims=True)
    d = x - mu
    var = jnp.mean(d * d, axis=-1, keepdims=True)
    y = d * lax.rsqrt(var + CONV_LN_EPS) * g_ref[...] + b_ref[...]
    o_ref[...] = _silu(y).astype(o_ref.dtype)


def _ln_silu(cfg, x, g, b):
    bsz, s, d = x.shape
    ts = min(cfg.row_tile, s)
    row = pl.BlockSpec((None, ts, d), lambda bb, i: (bb, i, 0))
    vec = pl.BlockSpec((1, d), lambda bb, i: (0, 0))
    return pl.pallas_call(
        _ln_silu_kernel,
        grid=(bsz, s // ts),
        in_specs=[row, vec, vec],
        out_specs=row,
        out_shape=jax.ShapeDtypeStruct((bsz, s, d), BF16),
        compiler_params=_cparams(cfg, "parallel", "parallel"),
        name="layernorm_swish",
    )(x, g.reshape(1, d), b.reshape(1, d))


def _group_ones(n, group):
    r = lax.broadcasted_iota(jnp.int32, (n, n), 0)
    c = lax.broadcasted_iota(jnp.int32, (n, n), 1)
    shift = int(math.log2(group))
    return jnp.where((r >> shift) == (c >> shift), 1.0, 0.0).astype(F32)


def _group_sum(x, gmat):
    n = x.shape[1]
    parts = [_dot(x[:, s0:s0 + V7X_LANES], gmat, precision=HIGHEST) for s0 in range(0, n, V7X_LANES)]
    return parts[0] if len(parts) == 1 else jnp.concatenate(parts, axis=1)


def _rwkv_pre_kernel(r_ref, k_ref, v_ref, lo_ref, mur_ref, muk_ref, muv_ref, mul_ref,
                     w0_ref, wup_ref, a0_ref, aup_ref, gup_ref, kk_ref, ka_ref,
                     ro_ref, lw_ref, ko_ref, vo_ref, ao_ref, bo_ref, go_ref,
                     lr_ref, lk_ref, lv_ref, ll_ref, *, ts, hd, lw_pad):
    first = pl.program_id(1) == 0

    def shift(x_ref, last_ref, mu_ref):
        @pl.when(first)
        def _():
            last_ref[...] = jnp.zeros_like(last_ref)

        x = x_ref[...]
        row = lax.broadcasted_iota(jnp.int32, x.shape, 0)
        prev = jnp.where(row == 0, last_ref[V7X_SUBLANES - 1:V7X_SUBLANES, :], pltpu.roll(x, 1, 0))
        last_ref[...] = x[ts - V7X_SUBLANES:ts, :]
        return x + (prev - x) * mu_ref[...]

    r = shift(r_ref, lr_ref, mur_ref)
    k = shift(k_ref, lk_ref, muk_ref)
    v = shift(v_ref, lv_ref, muv_ref)
    lo = shift(lo_ref, ll_ref, mul_ref)
    xw, xa, xg = lo[:, :lw_pad], lo[:, lw_pad:2 * lw_pad], lo[:, 2 * lw_pad:]

    z = w0_ref[...] + _dot(jnp.tanh(xw), wup_ref[...], precision=HIGHEST)
    softplus = jnp.maximum(-z, 0.0) + jnp.log(1.0 + jnp.exp(-jnp.abs(z)))
    lw_ref[...] = -jnp.exp(-softplus - 0.5)
    a = jax.nn.sigmoid(a0_ref[...] + _dot(xa, aup_ref[...], precision=HIGHEST))
    go_ref[...] = _dot(jax.nn.sigmoid(xg).astype(BF16), gup_ref[...].astype(BF16))

    kkr = k * kk_ref[...]
    ss = _group_sum(kkr * kkr, _group_ones(V7X_LANES, hd))
    kk = kkr / jnp.maximum(jnp.sqrt(ss), 1e-12)
    ro_ref[...] = r
    vo_ref[...] = v
    ko_ref[...] = k * (1.0 + (a - 1.0) * ka_ref[...])
    ao_ref[...] = -kk
    bo_ref[...] = kk * a


def _rwkv_pre(cfg, proj, lora, mu, w0, w_up, a0, a_up, g_up, k_k, k_a):
    bsz, s, _ = proj.shape
    d = cfg.rwkv_dim
    ts = min(cfg.row_tile // 2, s)
    lw_pad = V7X_LANES
    assert cfg.decay_lora <= lw_pad and cfg.iclr_lora <= lw_pad and (2 * cfg.conv_ch) % d == 0
    lo_w = lora.shape[2]
    c0 = 2 * cfg.conv_ch // d
    pad_rows = lambda w: jnp.zeros((lw_pad, d), F32).at[:w.shape[0]].set(w)
    pad_vec = lambda vv, n: jnp.zeros((1, n), F32).at[0, :vv.shape[0]].set(vv)
    mu_r, mu_k, mu_v = (mu[i * d:(i + 1) * d].reshape(1, d) for i in range(3))
    o = 3 * d
    mu_l = jnp.concatenate([
        pad_vec(mu[o:o + cfg.decay_lora], lw_pad),
        pad_vec(mu[o + cfg.decay_lora:o + cfg.decay_lora + cfg.iclr_lora], lw_pad),
        mu[o + cfg.decay_lora + cfg.iclr_lora:].reshape(1, -1)], axis=1)
    row = lambda cb: pl.BlockSpec((None, ts, d), lambda b, i: (b, i, cb))
    lrow = pl.BlockSpec((None, ts, lo_w), lambda b, i: (b, i, 0))
    vec = lambda n: pl.BlockSpec((1, n), lambda b, i: (0, 0))
    mat = lambda rws: pl.BlockSpec((rws, d), lambda b, i: (0, 0))
    orow = pl.BlockSpec((None, ts, d), lambda b, i: (b, i, 0))
    return pl.pallas_call(
        functools.partial(_rwkv_pre_kernel, ts=ts, hd=cfg.rwkv_head_dim, lw_pad=lw_pad),
        grid=(bsz, s // ts),
        in_specs=[row(c0), row(c0 + 1), row(c0 + 2), lrow, vec(d), vec(d), vec(d), vec(lo_w),
                  vec(d), mat(lw_pad), vec(d), mat(lw_pad), mat(cfg.gate_lora), vec(d), vec(d)],
        out_specs=[orow] * 7,
        out_shape=[jax.ShapeDtypeStruct((bsz, s, d), F32)] * 7,
        scratch_shapes=[pltpu.VMEM((V7X_SUBLANES, d), F32)] * 3 + [pltpu.VMEM((V7X_SUBLANES, lo_w), F32)],
        compiler_params=_cparams(cfg, "parallel", "arbitrary"),
        name="rwkv_token_shift_lora",
    )(proj, proj, proj, lora, mu_r, mu_k, mu_v, mu_l, w0.reshape(1, d), pad_rows(w_up),
      a0.reshape(1, d), pad_rows(a_up), g_up, k_k.reshape(1, d), k_a.reshape(1, d))


def _scan_chunk(r, lw, k, v, a, b, st, consts):
    tri, strict_bd, incl_bd, eye, lane_a, bd = consts
    L = r[0].shape[0]
    each = lambda f, *ls: [f(*xs) for xs in zip(*ls)]
    bf = lambda x: x.astype(BF16)
    stack = lambda x: jnp.concatenate([x, x], axis=0)
    unstack = lambda x: jnp.where(lane_a, x[:L], x[L:])
    left, right = (lambda x: x[:, :V7X_LANES]), (lambda x: x[:, V7X_LANES:])

    def split3(x):
        w1 = bf(x)
        e1 = x - w1.astype(F32)
        w2 = bf(e1)
        return jnp.concatenate([w1, w2, bf(e1 - w2.astype(F32))], axis=1)

    cs = each(lambda x: _dot(tri, split3(x)), lw)
    cum = each(lambda x: x[:, :V7X_LANES] + x[:, V7X_LANES:2 * V7X_LANES] + x[:, 2 * V7X_LANES:], cs)
    cl = each(lambda x: x[L - 1:L, :], cum)
    tail = each(lambda x, y: jnp.exp(x - y), cl, cum)
    at = each(lambda x, c, w: x * jnp.exp(c - w), a, cum, lw)
    rt = each(lambda x, c: x * jnp.exp(c), r, cum)
    g_inv = each(lambda c: jnp.exp(-c), cum)
    bt = each(lambda x, g: stack(bf(x * g)), b, g_inv)
    kt = each(lambda x, g: stack(bf(x * g)), k, g_inv)
    lhs = each(lambda x, y: jnp.concatenate(
        [bf(jnp.where(lane_a, x, 0.0)), bf(jnp.where(lane_a, 0.0, x)),
         bf(jnp.where(lane_a, y, 0.0)), bf(jnp.where(lane_a, 0.0, y))], axis=0), at, rt)
    xb = each(_dot_nt, lhs, bt)
    xk = each(_dot_nt, lhs, kt)
    n = each(lambda x: jnp.where(strict_bd, x[:2 * L], 0.0), xb)
    m = each(lambda x: bf(jnp.where(strict_bd, x[:2 * L], 0.0)), xk)
    p = each(lambda x: bf(jnp.where(incl_bd, x[2 * L:], 0.0)), xb)
    q = each(lambda x: bf(jnp.where(incl_bd, x[2 * L:], 0.0)), xk)
    vb = each(bf, v)
    v_st = each(stack, vb)
    mv = each(_dot, m, v_st)
    w = each(lambda x: eye + x, n)
    pw = each(bf, n)
    for _ in range(int(math.log2(L)) - 1):
        pw = each(lambda x: bf(_dot(x, x)), pw)
        w = each(lambda x, y: x + _dot(bf(x), y), w, pw)
    au = each(lambda ww, x, y: _dot(bf(ww), jnp.concatenate([stack(bf(x)), bf(y)], axis=1)),
              w, at, mv)
    pau = each(lambda x, y: _dot(x, bf(y)), p, au)
    qv = each(_dot, q, v_st)
    rbar = each(lambda x, y: bf(x + unstack(left(y))), rt, pau)
    ybar = each(lambda x, y: unstack(right(x) + y), pau, qv)
    bh = each(lambda x, t: stack(bf(x * t)), b, tail)
    kh = each(lambda x, t: bf(x * t), k, tail)
    abar = each(lambda x: bf(jnp.where(bd, left(x), 0.0)), au)
    ubar = each(lambda x: bf(jnp.where(bd, right(x), 0.0)), au)
    tt = each(lambda x, y: bf(jnp.where(bd, _dot_tn(x, y), 0.0)), abar, bh)
    z = each(lambda u, vv, x, y: jnp.where(bd, _dot_tn(jnp.concatenate([u, vv], axis=0),
                                                       jnp.concatenate([x, y], axis=0)), 0.0),
             ubar, vb, bh, kh)
    sb = each(bf, st)
    y = each(lambda x, s, yb: _dot_nt(x, s) + yb, rbar, sb, ybar)
    st_new = each(lambda s, c, s16, t, zz: s * jnp.exp(c) + _dot(s16, t) + zz, st, cl, sb, tt, z)
    return y, st_new


def _scan_kernel(r_ref, lw_ref, k_ref, v_ref, a_ref, b_ref, y_ref, st_ref, *, ts, L, pairs, hd):
    @pl.when(pl.program_id(2) == 0)
    def _():
        st_ref[...] = jnp.zeros_like(st_ref)

    ri = lax.broadcasted_iota(jnp.int32, (L, L), 0)
    ci = lax.broadcasted_iota(jnp.int32, (L, L), 1)
    lane = lax.broadcasted_iota(jnp.int32, (1, V7X_LANES), 1)
    r2 = lax.broadcasted_iota(jnp.int32, (2 * L, 2 * L), 0)
    c2 = lax.broadcasted_iota(jnp.int32, (2 * L, 2 * L), 1)
    bd = (r2 < L) == (c2 < L)
    rl, cl2 = r2 & (L - 1), c2 & (L - 1)
    consts = (jnp.where(ri >= ci, 1.0, 0.0).astype(BF16), bd & (rl > cl2), bd & (rl >= cl2),
              jnp.where(r2 == c2, 1.0, 0.0), lane < hd, bd)

    def body(c, carry):
        rows = pl.ds(pl.multiple_of(c * L, L), L)
        cols = [slice(p * V7X_LANES, (p + 1) * V7X_LANES) for p in range(pairs)]
        load = lambda ref: [ref[rows, cs] for cs in cols]
        ys, sts = _scan_chunk(load(r_ref), load(lw_ref), load(k_ref), load(v_ref), load(a_ref),
                              load(b_ref), [st_ref[p] for p in range(pairs)], consts)
        for p in range(pairs):
            y_ref[rows, cols[p]] = ys[p]
            st_ref[p] = sts[p]
        return carry

    lax.fori_loop(0, ts // L, body, 0)


def _rwkv_scan(cfg, r, lw, k, v, a, b):
    bsz, s, d = r.shape
    hd = cfg.rwkv_head_dim
    assert 2 * hd == V7X_LANES
    ts, L = min(cfg.scan_rows, s), cfg.scan_chunk
    npairs = d // V7X_LANES
    pairs = min(cfg.scan_pairs, npairs)
    assert s % ts == 0 and ts % L == 0 and npairs % pairs == 0
    blk = pl.BlockSpec((None, ts, pairs * V7X_LANES), lambda bb, p, i: (bb, i, p))
    return pl.pallas_call(
        functools.partial(_scan_kernel, ts=ts, L=L, pairs=pairs, hd=hd),
        grid=(bsz, npairs // pairs, s // ts),
        in_specs=[blk] * 6,
        out_specs=blk,
        out_shape=jax.ShapeDtypeStruct((bsz, s, d), F32),
        scratch_shapes=[pltpu.VMEM((pairs, V7X_LANES, V7X_LANES), F32)],
        compiler_params=_cparams(cfg, "parallel", "parallel", "arbitrary"),
        name="rwkv7_scan",
    )(r, lw, k, v, a, b)


def _rwkv_post_kernel(y_ref, r_ref, k_ref, v_ref, g_ref, rk_ref, lg_ref, lb_ref, o_ref, *, hd):
    gmat = _group_ones(V7X_LANES, hd)
    y = y_ref[...]
    mu = _group_sum(y, gmat) * (1.0 / hd)
    d = y - mu
    var = _group_sum(d * d, gmat) * (1.0 / hd)
    yn = d * lax.rsqrt(var + RWKV_LNX_EPS) * lg_ref[...] + lb_ref[...]
    bonus = _group_sum(r_ref[...] * k_ref[...] * rk_ref[...], gmat) * v_ref[...]
    o_ref[...] = ((yn + bonus) * g_ref[...]).astype(o_ref.dtype)


def _rwkv_post(cfg, y, r, k, v, g, r_k, lnx_g, lnx_b):
    bsz, s, d = y.shape
    ts = min(cfg.row_tile, s)
    row = pl.BlockSpec((None, ts, d), lambda b, i: (b, i, 0))
    vec = pl.BlockSpec((1, d), lambda b, i: (0, 0))
    return pl.pallas_call(
        functools.partial(_rwkv_post_kernel, hd=cfg.rwkv_head_dim),
        grid=(bsz, s // ts),
        in_specs=[row] * 5 + [vec] * 3,
        out_specs=row,
        out_shape=jax.ShapeDtypeStruct((bsz, s, d), BF16),
        compiler_params=_cparams(cfg, "parallel", "parallel"),
        name="rwkv_groupnorm_gate",
    )(y, r, k, v, g, r_k.reshape(1, d), lnx_g.reshape(1, d), lnx_b.reshape(1, d))


def _even_mixer(cfg, h, w_in, w_out):
    bsz, s, d = h.shape
    proj = _matmul(cfg, [h.reshape(bsz * s, d)], w_in, 0, cfg.even_in, F32,
                   cfg.mm_tm, cfg.mm_tn).reshape(bsz, s, cfg.even_in)
    o_m = _moba(cfg, proj).reshape(bsz * s, cfg.dm)
    o_r = _retention(cfg, proj).reshape(bsz * s, cfg.dv)
    return _matmul(cfg, [o_m, o_r], w_out, 0, d, F32, cfg.mm_tm, cfg.mm_tn // 2).reshape(bsz, s, d)


def _odd_mixer(cfg, h, w_in, w_out, conv_w, conv_b, conv_ln_g, conv_ln_b, mu, w0, w_up, a0, a_up,
               g_up, k_k, k_a, r_k, lnx_g, lnx_b):
    bsz, s, d = h.shape
    h2 = h.reshape(bsz * s, d)
    proj = _matmul(cfg, [h2], w_in, 0, cfg.odd_main, F32,
                   cfg.mm_tm, cfg.mm_tn).reshape(bsz, s, cfg.odd_main)
    lw_pad = V7X_LANES
    wl = w_in[0, :, cfg.odd_main:]
    zc = lambda n: jnp.zeros((d, n), F32)
    o1, o2 = cfg.decay_lora, cfg.decay_lora + cfg.iclr_lora
    wl = jnp.concatenate([wl[:, :o1], zc(lw_pad - cfg.decay_lora), wl[:, o1:o2],
                          zc(lw_pad - cfg.iclr_lora), wl[:, o2:]], axis=1)[None]
    lo_w = wl.shape[2]
    lora = _matmul(cfg, [h2], wl, 0, lo_w, F32, cfg.mm_tm, lo_w).reshape(bsz, s, lo_w)

    u = _ln_silu(cfg, _conv_glu(cfg, proj, conv_w, conv_b), conv_ln_g, conv_ln_b)
    r, lw, k, v, a, b, g = _rwkv_pre(cfg, proj, lora, mu, w0, w_up, a0, a_up, g_up, k_k, k_a)
    y = _rwkv_scan(cfg, r, lw, k, v, a, b)
    y = _rwkv_post(cfg, y, r, k, v, g, r_k.reshape(-1), lnx_g, lnx_b)
    return _matmul(cfg, [u.reshape(bsz * s, -1), y.reshape(bsz * s, -1)], w_out, 0, d, F32,
                   cfg.mm_tm, cfg.mm_tn).reshape(bsz, s, d)


def _forward(cfg, x, c, w_ada, b_ada, norm_g, w_ffn_in, w_ffn_out, even_w_in, even_w_out, odd_w_in,
             odd_w_out, conv_w, conv_b, conv_ln_g, conv_ln_b, rwkv_mu, rwkv_w0, rwkv_w_up, rwkv_a0,
             rwkv_a_up, rwkv_g_up, rwkv_k_k, rwkv_k_a, rwkv_r_k, rwkv_lnx_g, rwkv_lnx_b):
    bsz, s, d = x.shape
    depth = w_ada.shape[0]
    mods = _modulation(cfg, c, w_ada, b_ada)
    w_ffn_out_bf = w_ffn_out.astype(BF16)
    sh_m, sc_m = mods[0, :, 0], mods[0, :, 1]
    h = _norm_mod(cfg, x, norm_g[0, 0], sc_m, sh_m)
    for layer in range(depth):
        g_m, sh_f, sc_f, g_f = (mods[layer, :, i] for i in (2, 3, 4, 5))
        j = layer // 2
        if layer % 2 == 0:
            o = _even_mixer(cfg, h, even_w_in[j:j + 1], even_w_out[j:j + 1])
        else:
            o = _odd_mixer(cfg, h, odd_w_in[j:j + 1], odd_w_out[j:j + 1], conv_w[j], conv_b[j],
                           conv_ln_g[j], conv_ln_b[j], rwkv_mu[j], rwkv_w0[j], rwkv_w_up[j],
                           rwkv_a0[j], rwkv_a_up[j], rwkv_g_up[j], rwkv_k_k[j], rwkv_k_a[j],
                           rwkv_r_k[j], rwkv_lnx_g[j], rwkv_lnx_b[j])
        x, h = _resid(cfg, x, o, norm_g[layer, 1], g_m, (norm_g[layer, 2], sc_f, sh_f))
        act = _ffn_in(cfg, h.reshape(bsz * s, d), w_ffn_in, layer)
        f = _matmul(cfg, [act], w_ffn_out_bf, layer, d, F32, cfg.ffn_out_tm,
                    cfg.mm_tn).reshape(bsz, s, d)
        if layer + 1 < depth:
            nxt = (norm_g[layer + 1, 0], mods[layer + 1, :, 1], mods[layer + 1, :, 0])
            x, h = _resid(cfg, x, f, norm_g[layer, 3], g_f, nxt)
        else:
            x = _resid(cfg, x, f, norm_g[layer, 3], g_f)
    return x


def kernel(x, c, w_ada, b_ada, norm_g, w_ffn_in, w_ffn_out, even_w_in, even_w_out, odd_w_in, odd_w_out, conv_w, conv_b, conv_ln_g, conv_ln_b, rwkv_mu, rwkv_w0, rwkv_w_up, rwkv_a0, rwkv_a_up, rwkv_g_up, rwkv_k_k, rwkv_k_a, rwkv_r_k, rwkv_lnx_g, rwkv_lnx_b):
    return _forward(Config(), x, c, w_ada, b_ada, norm_g, w_ffn_in, w_ffn_out, even_w_in, even_w_out,
                    odd_w_in, odd_w_out, conv_w, conv_b, conv_ln_g, conv_ln_b, rwkv_mu, rwkv_w0,
                    rwkv_w_up, rwkv_a0, rwkv_a_up, rwkv_g_up, rwkv_k_k, rwkv_k_a, rwkv_r_k,
                    rwkv_lnx_g, rwkv_lnx_b)
```

```python
import dataclasses
import functools
import math

import jax
import jax.numpy as jnp
from jax import lax
from jax.experimental import pallas as pl
from jax.experimental.pallas import tpu as pltpu

F32 = jnp.float32
BF16 = jnp.bfloat16
HIGHEST = lax.Precision.HIGHEST

V7X_LANES = 128
V7X_SUBLANES = 8
MIB = 1024 * 1024
NORM_EPS = 1e-6
ROPE_THETA = 10000.0
CONV_LN_EPS = 1e-5
RWKV_LNX_EPS = 64e-5


@dataclasses.dataclass(frozen=True)
class Config:
    d_model: int = 4096
    moba_heads: int = 16
    moba_head_dim: int = 128
    moba_block: int = 256
    moba_topk: int = 3
    ret_heads: int = 8
    ret_key_dim: int = 256
    ret_val_dim: int = 512
    ret_chunk: int = 128
    conv_ch: int = 2048
    conv_width: int = 31
    rwkv_dim: int = 2048
    rwkv_head_dim: int = 64
    decay_lora: int = 96
    iclr_lora: int = 96
    gate_lora: int = 256
    ffn_hidden: int = 11008
    row_tile: int = 256
    mm_tm: int = 1024
    mm_tn: int = 512
    ffn_tn: int = 256
    ffn_out_tm: int = 512
    ret_rows: int = 512
    conv_rows: int = 256
    conv_cols: int = 256
    scan_rows: int = 512
    scan_chunk: int = 64
    scan_pairs: int = 8
    moba_group: int = 4
    moba_heads_per_step: int = 4
    vmem_mib: int = 56

    @property
    def dm(self):
        return self.moba_heads * self.moba_head_dim

    @property
    def dk(self):
        return self.ret_heads * self.ret_key_dim

    @property
    def dv(self):
        return self.ret_heads * self.ret_val_dim

    @property
    def even_in(self):
        return 3 * self.dm + 2 * self.dk + 2 * self.dv

    @property
    def lora_in(self):
        return self.decay_lora + self.iclr_lora + self.gate_lora

    @property
    def odd_main(self):
        return 2 * self.conv_ch + 3 * self.rwkv_dim


def _cparams(cfg, *sem):
    return pltpu.CompilerParams(dimension_semantics=sem, vmem_limit_bytes=cfg.vmem_mib * MIB)


def _silu(x):
    return x * jax.nn.sigmoid(x)


def _dot(a, b, **kw):
    return jnp.dot(a, b, preferred_element_type=F32, **kw)


def _dot_nt(a, b, **kw):
    return lax.dot_general(a, b, (((1,), (1,)), ((), ())), preferred_element_type=F32, **kw)


def _dot_tn(a, b, **kw):
    return lax.dot_general(a, b, (((0,), (0,)), ((), ())), preferred_element_type=F32, **kw)


def _ada_kernel(c_ref, w_ref, b_ref, o_ref):
    s = _silu(c_ref[...])
    hi = s.astype(BF16).astype(F32)
    parts = _dot(jnp.concatenate([hi, s - hi], axis=0).astype(BF16), w_ref[...].astype(BF16))
    o_ref[...] = parts[:V7X_SUBLANES] + parts[V7X_SUBLANES:] + b_ref[...]


def _modulation(cfg, c, w_ada, b_ada, tn=512):
    depth, d, n = w_ada.shape
    bsz = c.shape[0]
    cp = jnp.zeros((V7X_SUBLANES, d), F32).at[:bsz].set(c)
    out = pl.pallas_call(
        _ada_kernel,
        grid=(depth, n // tn),
        in_specs=[
            pl.BlockSpec((V7X_SUBLANES, d), lambda l, j: (0, 0)),
            pl.BlockSpec((None, d, tn), lambda l, j: (l, 0, j)),
            pl.BlockSpec((None, 1, tn), lambda l, j: (l, 0, j)),
        ],
        out_specs=pl.BlockSpec((None, V7X_SUBLANES, tn), lambda l, j: (l, 0, j)),
        out_shape=jax.ShapeDtypeStruct((depth, V7X_SUBLANES, n), F32),
        compiler_params=_cparams(cfg, "parallel", "parallel"),
        name="adaln_modulation",
    )(cp, w_ada, b_ada.reshape(depth, 1, n))
    return out[:, :bsz].reshape(depth, bsz, 6, 1, d)


def _rms(x, g):
    return x * lax.rsqrt(jnp.mean(x * x, axis=-1, keepdims=True) + NORM_EPS) * g


def _norm_mod_kernel(x_ref, g_ref, sc_ref, sh_ref, o_ref):
    y = _rms(x_ref[...], g_ref[...])
    o_ref[...] = (y * (1.0 + sc_ref[...]) + sh_ref[...]).astype(o_ref.dtype)


def _norm_mod(cfg, x, g, sc, sh):
    bsz, s, d = x.shape
    ts = min(cfg.row_tile, s)
    row = pl.BlockSpec((None, ts, d), lambda b, i: (b, i, 0))
    vec = pl.BlockSpec((1, d), lambda b, i: (0, 0))
    mod = pl.BlockSpec((None, 1, d), lambda b, i: (b, 0, 0))
    return pl.pallas_call(
        _norm_mod_kernel,
        grid=(bsz, s // ts),
        in_specs=[row, vec, mod, mod],
        out_specs=row,
        out_shape=jax.ShapeDtypeStruct((bsz, s, d), BF16),
        compiler_params=_cparams(cfg, "parallel", "parallel"),
        name="norm_modulate",
    )(x, g.reshape(1, d), sc, sh)


def _resid_kernel(x_ref, o_ref, ga_ref, gate_ref, *rest, with_h):
    xn = x_ref[...] + gate_ref[...] * _rms(o_ref[...], ga_ref[...])
    if with_h:
        gb_ref, sc_ref, sh_ref, xn_ref, h_ref = rest
        xn_ref[...] = xn
        h_ref[...] = (_rms(xn, gb_ref[...]) * (1.0 + sc_ref[...]) + sh_ref[...]).astype(h_ref.dtype)
    else:
        (xn_ref,) = rest
        xn_ref[...] = xn


def _resid(cfg, x, o, ga, gate, nxt=None):
    bsz, s, d = x.shape
    ts = min(cfg.row_tile, s)
    row = pl.BlockSpec((None, ts, d), lambda b, i: (b, i, 0))
    vec = pl.BlockSpec((1, d), lambda b, i: (0, 0))
    mod = pl.BlockSpec((None, 1, d), lambda b, i: (b, 0, 0))
    with_h = nxt is not None
    in_specs = [row, row, vec, mod]
    args = [x, o, ga.reshape(1, d), gate]
    out_specs = [row]
    out_shape = [jax.ShapeDtypeStruct((bsz, s, d), F32)]
    if with_h:
        gb, sc, sh = nxt
        in_specs += [vec, mod, mod]
        args += [gb.reshape(1, d), sc, sh]
        out_specs.append(row)
        out_shape.append(jax.ShapeDtypeStruct((bsz, s, d), BF16))
    outs = pl.pallas_call(
        functools.partial(_resid_kernel, with_h=with_h),
        grid=(bsz, s // ts),
        in_specs=in_specs,
        out_specs=out_specs,
        out_shape=out_shape,
        compiler_params=_cparams(cfg, "parallel", "parallel"),
        name="residual_norm",
    )(*args)
    return outs if with_h else outs[0]


def _mm_kernel(*refs, widths, w_is_nk):
    x_refs, (w_ref, o_ref) = refs[:len(widths)], refs[len(widths):]
    acc, off = None, 0
    for x_ref, wd in zip(x_refs, widths):
        if w_is_nk:
            part = _dot_nt(x_ref[...], w_ref[:, off:off + wd].astype(BF16))
        else:
            part = _dot(x_ref[...], w_ref[off:off + wd, :].astype(BF16))
        acc = part if acc is None else acc + part
        off += wd
    o_ref[...] = acc.astype(o_ref.dtype)


def _matmul(cfg, xs, w, layer, n, out_dtype, tm, tn, w_is_nk=False):
    m = xs[0].shape[0]
    widths = tuple(x.shape[1] for x in xs)
    kdim = sum(widths)
    tm, tn = min(tm, m), min(tn, n)
    assert m % tm == 0 and n % tn == 0 and w.shape[2 if w_is_nk else 1] == kdim
    wspec = (pl.BlockSpec((None, tn, kdim), lambda i, j: (layer, j, 0)) if w_is_nk
             else pl.BlockSpec((None, kdim, tn), lambda i, j: (layer, 0, j)))
    return pl.pallas_call(
        functools.partial(_mm_kernel, widths=widths, w_is_nk=w_is_nk),
        grid=(m // tm, n // tn),
        in_specs=[pl.BlockSpec((tm, wd), lambda i, j: (i, 0)) for wd in widths] + [wspec],
        out_specs=pl.BlockSpec((tm, tn), lambda i, j: (i, j)),
        out_shape=jax.ShapeDtypeStruct((m, n), out_dtype),
        compiler_params=_cparams(cfg, "parallel", "parallel"),
        name="matmul",
    )(*xs, w)


def _ffn_in_kernel(x_ref, wg_ref, wu_ref, o_ref):
    x = x_ref[...]
    gate = _dot(x, wg_ref[...].astype(BF16))
    up = _dot(x, wu_ref[...].astype(BF16))
    o_ref[...] = (_silu(gate) * up).astype(o_ref.dtype)


def _ffn_in(cfg, x, w, layer):
    m, kdim = x.shape
    hid = w.shape[2] // 2
    tm, tn = min(cfg.mm_tm, m), min(cfg.ffn_tn, hid)
    nt = hid // tn
    assert m % tm == 0 and hid % tn == 0
    return pl.pallas_call(
        _ffn_in_kernel,
        grid=(m // tm, nt),
        in_specs=[
            pl.BlockSpec((tm, kdim), lambda i, j: (i, 0)),
            pl.BlockSpec((None, kdim, tn), lambda i, j: (layer, 0, j)),
            pl.BlockSpec((None, kdim, tn), lambda i, j: (layer, 0, nt + j)),
        ],
        out_specs=pl.BlockSpec((tm, tn), lambda i, j: (i, j)),
        out_shape=jax.ShapeDtypeStruct((m, hid), BF16),
        compiler_params=_cparams(cfg, "parallel", "parallel"),
        name="ffn_in_swiglu",
    )(x, w, w)


def _moba_kernel(q_ref, k_ref, v_ref, cq_ref, sq_ref, ck_ref, sk_ref, o_ref,
                 kr_ref, vt_ref, km_ref, sel_ref, *, nb, blk, dh, topk, group, hp):
    qi = pl.program_id(2)
    half = dh // 2
    heads = range(hp)
    lanes = [slice(n * dh, (n + 1) * dh) for n in heads]
    each = lambda f, *ls: [f(*xs) for xs in zip(*ls)]

    @pl.when(qi == 0)
    def _():
        for j in range(nb):
            rows = slice(j * blk, (j + 1) * blk)
            for n in heads:
                kb = k_ref[rows, lanes[n]]
                kr = kb * ck_ref[rows, :] + pltpu.roll(kb, half, 1) * sk_ref[rows, :]
                km_ref[n, j:j + 1, :] = jnp.mean(kr, axis=0, keepdims=True)
                kr_ref[n, j] = kr.astype(BF16)
                vt_ref[n, j] = v_ref[rows, lanes[n]].T.astype(BF16)

    cq, sq = cq_ref[...], sq_ref[...]
    qr = [(lambda q: q * cq + pltpu.roll(q, half, 1) * sq)(q_ref[:, lanes[n]]) for n in heads]

    gate = [_dot_nt(km_ref[n], qr[n], precision=HIGHEST) for n in heads]
    brow = lax.broadcasted_iota(jnp.int32, (nb, blk), 0)
    gm = each(lambda g: jnp.where(brow < qi, g, -jnp.inf), gate)
    for j in range(nb):
        below = jnp.where(brow < j, 1.0, 0.0)

        def keep_row(g):
            gj = g[j:j + 1, :]
            beats = jnp.where(g > gj, 1.0, 0.0) + jnp.where(g == gj, below, 0.0)
            keep = jnp.where(jnp.sum(beats, axis=0, keepdims=True) < topk, 1.0, 0.0)
            return jnp.broadcast_to(jnp.where(j < qi, keep, 0.0), (V7X_SUBLANES, blk))

        for n, row in enumerate(each(keep_row, gm)):
            sel_ref[n, j] = row

    qs = each(lambda x: (x * (dh ** -0.5)).T.astype(BF16), qr)
    kpos = lax.broadcasted_iota(jnp.int32, (blk, blk), 0)
    qpos = lax.broadcasted_iota(jnp.int32, (blk, blk), 1)
    colmax = lambda x: jnp.max(x, axis=0, keepdims=True)
    colsum = lambda x: jnp.sum(x, axis=0, keepdims=True)
    s = [jnp.where(kpos <= qpos, _dot(kr_ref[n, qi], qs[n]), -jnp.inf) for n in heads]
    m = each(colmax, s)
    p = each(lambda x, y: jnp.exp(x - y), s, m)
    l = each(colsum, p)
    acc = [_dot(vt_ref[n, qi], p[n].astype(BF16)) for n in heads]

    def body(g, carry):
        m, l, acc = carry
        js = [g * group + u for u in range(group)]
        ss = [[jnp.where(sel_ref[n, j][0:1, :] > 0.0, _dot(kr_ref[n, j], qs[n]), -jnp.inf)
               for n in heads] for j in js]
        m_new = list(m)
        for su in ss:
            m_new = each(lambda x, y: jnp.maximum(x, colmax(y)), m_new, su)
        alpha = each(lambda x, y: jnp.exp(x - y), m, m_new)
        l = each(lambda x, y: x * y, alpha, l)
        acc = each(lambda x, y: x * y, alpha, acc)
        for j, su in zip(js, ss):
            p = each(lambda x, y: jnp.exp(x - y), su, m_new)
            l = each(lambda x, y: x + colsum(y), l, p)
            acc = [acc[n] + _dot(vt_ref[n, j], p[n].astype(BF16)) for n in heads]
        return tuple(m_new), tuple(l), tuple(acc)

    m, l, acc = lax.fori_loop(0, (qi + group - 1) // group, body, (tuple(m), tuple(l), tuple(acc)))
    for n in heads:
        o_ref[:, lanes[n]] = (acc[n] / l[n]).T.astype(o_ref.dtype)


def _rope_tables(seq, dim):
    inv = 1.0 / (ROPE_THETA ** (jnp.arange(0, dim, 2, dtype=F32) / dim))
    ang = jnp.arange(seq, dtype=F32)[:, None] * inv[None, :]
    return jnp.cos(ang), jnp.sin(ang)


def _moba(cfg, proj):
    bsz, s, _ = proj.shape
    h, dh, blk = cfg.moba_heads, cfg.moba_head_dim, cfg.moba_block
    assert dh == V7X_LANES and s % blk == 0
    nb = s // blk
    group = math.gcd(cfg.moba_group, nb)
    hp = math.gcd(cfg.moba_heads_per_step, h)
    hg = h // hp
    cos, sin = _rope_tables(s, dh)
    cosf = jnp.concatenate([cos, cos], axis=1)
    sinf = jnp.concatenate([-sin, sin], axis=1)
    qspec = pl.BlockSpec((None, blk, hp * dh), lambda b, hh, i: (b, i, hh))
    kspec = pl.BlockSpec((None, s, hp * dh), lambda b, hh, i: (b, 0, hg + hh))
    vspec = pl.BlockSpec((None, s, hp * dh), lambda b, hh, i: (b, 0, 2 * hg + hh))
    tq = pl.BlockSpec((blk, dh), lambda b, hh, i: (i, 0))
    tk = pl.BlockSpec((s, dh), lambda b, hh, i: (0, 0))
    return pl.pallas_call(
        functools.partial(_moba_kernel, nb=nb, blk=blk, dh=dh, topk=cfg.moba_topk, group=group,
                          hp=hp),
        grid=(bsz, hg, nb),
        in_specs=[qspec, kspec, vspec, tq, tq, tk, tk],
        out_specs=pl.BlockSpec((None, blk, hp * dh), lambda b, hh, i: (b, i, hh)),
        out_shape=jax.ShapeDtypeStruct((bsz, s, h * dh), BF16),
        scratch_shapes=[
            pltpu.VMEM((hp, nb, blk, dh), BF16),
            pltpu.VMEM((hp, nb, dh, blk), BF16),
            pltpu.VMEM((hp, nb, dh), F32),
            pltpu.VMEM((hp, nb, V7X_SUBLANES, blk), F32),
        ],
        compiler_params=_cparams(cfg, "parallel", "parallel", "arbitrary"),
        name="moba_attention",
    )(proj, proj, proj, cosf, sinf, cosf, sinf)


def _ret_kernel(q_ref, k_ref, v_ref, g_ref, cos_ref, sin_ref, dm_ref, qd_ref, kd_ref, cd_ref,
                o_ref, st_ref, *, c, nsub, dk):
    @pl.when(pl.program_id(2) == 0)
    def _():
        st_ref[...] = jnp.zeros_like(st_ref)

    half = dk // 2
    for sidx in range(nsub):
        rows = slice(sidx * c, (sidx + 1) * c)
        cos = cos_ref[rows, :]
        sin = sin_ref[rows, :]

        def rope(x):
            x1, x2 = x[:, :half], x[:, half:]
            return jnp.concatenate([x1 * cos - x2 * sin, x2 * cos + x1 * sin], axis=1)

        q = rope(q_ref[rows, :])
        k = rope(k_ref[rows, :]) * (dk ** -0.5)
        qb, kb = q.astype(BF16), k.astype(BF16)
        vb = v_ref[rows, :].astype(BF16)
        st = st_ref[...]
        inner = _dot_nt(qb, kb) * dm_ref[...]
        o = _dot(inner.astype(BF16), vb) + _dot(qb, st.astype(BF16)) * qd_ref[...]
        kd = (k * kd_ref[...]).astype(BF16)
        st_ref[...] = st * cd_ref[...] + _dot_tn(kd, vb)
        on = o * lax.rsqrt(jnp.mean(o * o, axis=-1, keepdims=True) + NORM_EPS)
        o_ref[rows, :] = (on * _silu(g_ref[rows, :])).astype(o_ref.dtype)


def _retention(cfg, proj):
    bsz, s, _ = proj.shape
    h, dk, dv, c = cfg.ret_heads, cfg.ret_key_dim, cfg.ret_val_dim, cfg.ret_chunk
    ts = min(cfg.ret_rows, s)
    assert s % ts == 0 and ts % c == 0
    q0 = 3 * cfg.dm // dk
    k0 = (3 * cfg.dm + cfg.dk) // dk
    v0 = (3 * cfg.dm + 2 * cfg.dk) // dv
    g0 = (3 * cfg.dm + 2 * cfg.dk + cfg.dv) // dv
    assert (3 * cfg.dm) % dk == 0 and (3 * cfg.dm + 2 * cfg.dk) % dv == 0
    cos, sin = _rope_tables(s, dk)
    log_g = jnp.log1p(-jnp.exp2(-5.0 - jnp.arange(h, dtype=F32)))
    idx = jnp.arange(c, dtype=F32)
    diff = idx[:, None] - idx[None, :]
    dmask = jnp.where(diff >= 0, jnp.exp(jnp.maximum(diff, 0.0) * log_g[:, None, None]), 0.0)
    qdec = jnp.exp((idx + 1.0) * log_g[:, None])[..., None]
    kdec = jnp.exp((c - 1.0 - idx) * log_g[:, None])[..., None]
    cdec = jnp.broadcast_to(jnp.exp(c * log_g)[:, None, None], (h, 1, dv))
    rowspec = lambda w, c0: pl.BlockSpec((None, ts, w), lambda b, hh, i: (b, i, c0 + hh))
    tab = pl.BlockSpec((ts, dk // 2), lambda b, hh, i: (i, 0))
    return pl.pallas_call(
        functools.partial(_ret_kernel, c=c, nsub=ts // c, dk=dk),
        grid=(bsz, h, s // ts),
        in_specs=[
            rowspec(dk, q0), rowspec(dk, k0), rowspec(dv, v0), rowspec(dv, g0), tab, tab,
            pl.BlockSpec((None, c, c), lambda b, hh, i: (hh, 0, 0)),
            pl.BlockSpec((None, c, 1), lambda b, hh, i: (hh, 0, 0)),
            pl.BlockSpec((None, c, 1), lambda b, hh, i: (hh, 0, 0)),
            pl.BlockSpec((None, 1, dv), lambda b, hh, i: (hh, 0, 0)),
        ],
        out_specs=pl.BlockSpec((None, ts, dv), lambda b, hh, i: (b, i, hh)),
        out_shape=jax.ShapeDtypeStruct((bsz, s, h * dv), BF16),
        scratch_shapes=[pltpu.VMEM((dk, dv), F32)],
        compiler_params=_cparams(cfg, "parallel", "parallel", "arbitrary"),
        name="retention",
    )(proj, proj, proj, proj, cos, sin, dmask, qdec, kdec, cdec)


CONV_HALO = 32


def _conv_kernel(a_ref, g_ref, w_ref, b_ref, o_ref, buf_ref, sh_ref, *, ts, kw, rc):
    @pl.when(pl.program_id(2) == 0)
    def _():
        buf_ref[0:CONV_HALO, :] = jnp.zeros((CONV_HALO, buf_ref.shape[1]), F32)

    buf_ref[CONV_HALO:CONV_HALO + ts, :] = a_ref[...] * jax.nn.sigmoid(g_ref[...])
    nshift = CONV_HALO + ts - V7X_SUBLANES
    for ph in range(1, V7X_SUBLANES):
        sh_ref[ph, 0:nshift, :] = buf_ref[ph:ph + nshift, :]
    off = CONV_HALO - (kw - 1)
    bias = b_ref[...]
    for r0 in range(0, ts, rc):
        acc = jnp.broadcast_to(bias, (rc, bias.shape[1]))
        for j in range(kw):
            ph = (off + j) % V7X_SUBLANES
            base = off + j - ph + r0
            rows = buf_ref[base:base + rc, :] if ph == 0 else sh_ref[ph, base:base + rc, :]
            acc = acc + w_ref[j:j + 1, :] * rows
        o_ref[r0:r0 + rc, :] = acc
    buf_ref[0:CONV_HALO, :] = buf_ref[ts:ts + CONV_HALO, :]


def _conv_glu(cfg, proj, conv_w, conv_b):
    bsz, s, _ = proj.shape
    ch, kw = cfg.conv_ch, cfg.conv_width
    ts, tc = min(cfg.conv_rows, s), min(cfg.conv_cols, ch)
    assert kw - 1 <= CONV_HALO <= ts and s % ts == 0 and ch % tc == 0
    nct = ch // tc
    wp = jnp.zeros((CONV_HALO, ch), F32).at[:kw].set(conv_w)
    return pl.pallas_call(
        functools.partial(_conv_kernel, ts=ts, kw=kw, rc=32),
        grid=(bsz, nct, s // ts),
        in_specs=[
            pl.BlockSpec((None, ts, tc), lambda b, c, i: (b, i, c)),
            pl.BlockSpec((None, ts, tc), lambda b, c, i: (b, i, nct + c)),
            pl.BlockSpec((CONV_HALO, tc), lambda b, c, i: (0, c)),
            pl.BlockSpec((1, tc), lambda b, c, i: (0, c)),
        ],
        out_specs=pl.BlockSpec((None, ts, tc), lambda b, c, i: (b, i, c)),
        out_shape=jax.ShapeDtypeStruct((bsz, s, ch), F32),
        scratch_shapes=[pltpu.VMEM((CONV_HALO + ts, tc), F32),
                        pltpu.VMEM((V7X_SUBLANES, CONV_HALO + ts, tc), F32)],
        compiler_params=_cparams(cfg, "parallel", "parallel", "arbitrary"),
        name="glu_causal_conv",
    )(proj, proj, wp, conv_b.reshape(1, ch))


def _ln_silu_kernel(x_ref, g_ref, b_ref, o_ref):
    x = x_ref[...]
    mu = jnp.mean(x, axis=-1, keepdims=True)
    d = x - mu
    var = jnp.mean(d * d, axis=-1, keepdims=True)
    y = d * lax.rsqrt(var + CONV_LN_EPS) * g_ref[...] + b_ref[...]
    o_ref[...] = _silu(y).astype(o_ref.dtype)


def _ln_silu(cfg, x, g, b):
    bsz, s, d = x.shape
    ts = min(cfg.row_tile, s)
    row = pl.BlockSpec((None, ts, d), lambda bb, i: (bb, i, 0))
    vec = pl.BlockSpec((1, d), lambda bb, i: (0, 0))
    return pl.pallas_call(
        _ln_silu_kernel,
        grid=(bsz, s // ts),
        in_specs=[row, vec, vec],
        out_specs=row,
        out_shape=jax.ShapeDtypeStruct((bsz, s, d), BF16),
        compiler_params=_cparams(cfg, "parallel", "parallel"),
        name="layernorm_swish",
    )(x, g.reshape(1, d), b.reshape(1, d))


def _group_ones(n, group):
    r = lax.broadcasted_iota(jnp.int32, (n, n), 0)
    c = lax.broadcasted_iota(jnp.int32, (n, n), 1)
    shift = int(math.log2(group))
    return jnp.where((r >> shift) == (c >> shift), 1.0, 0.0).astype(BF16)


def _split2(x):
    hi = x.astype(BF16)
    return hi, (x - hi.astype(F32)).astype(BF16)


def _dot_split(x, w):
    xh, xl = _split2(x)
    wh, wl = _split2(w)
    return _dot(xh, wh) + (_dot(xh, wl) + _dot(xl, wh))


def _group_sum(x, gmat):
    n = x.shape[1]
    hi, lo = _split2(x)
    parts = [_dot(hi[:, s0:s0 + V7X_LANES], gmat) + _dot(lo[:, s0:s0 + V7X_LANES], gmat)
             for s0 in range(0, n, V7X_LANES)]
    return parts[0] if len(parts) == 1 else jnp.concatenate(parts, axis=1)


def _rwkv_pre_kernel(r_ref, k_ref, v_ref, lo_ref, mur_ref, muk_ref, muv_ref, mul_ref,
                     w0_ref, wup_ref, a0_ref, aup_ref, gup_ref, kk_ref, ka_ref,
                     ro_ref, lw_ref, ko_ref, vo_ref, ao_ref, bo_ref, go_ref,
                     lr_ref, lk_ref, lv_ref, ll_ref, *, ts, hd, lw_pad):
    first = pl.program_id(1) == 0

    def shift(x_ref, last_ref, mu_ref):
        @pl.when(first)
        def _():
            last_ref[...] = jnp.zeros_like(last_ref)

        x = x_ref[...]
        row = lax.broadcasted_iota(jnp.int32, x.shape, 0)
        prev = jnp.where(row == 0, last_ref[V7X_SUBLANES - 1:V7X_SUBLANES, :], pltpu.roll(x, 1, 0))
        last_ref[...] = x[ts - V7X_SUBLANES:ts, :]
        return x + (prev - x) * mu_ref[...]

    r = shift(r_ref, lr_ref, mur_ref)
    k = shift(k_ref, lk_ref, muk_ref)
    v = shift(v_ref, lv_ref, muv_ref)
    lo = shift(lo_ref, ll_ref, mul_ref)
    xw, xa, xg = lo[:, :lw_pad], lo[:, lw_pad:2 * lw_pad], lo[:, 2 * lw_pad:]

    z = w0_ref[...] + _dot_split(jnp.tanh(xw), wup_ref[...])
    softplus = jnp.maximum(-z, 0.0) + jnp.log(1.0 + jnp.exp(-jnp.abs(z)))
    lw_ref[...] = -jnp.exp(-softplus - 0.5)
    a = jax.nn.sigmoid(a0_ref[...] + _dot_split(xa, aup_ref[...]))
    go_ref[...] = _dot(jax.nn.sigmoid(xg).astype(BF16), gup_ref[...].astype(BF16))

    kkr = k * kk_ref[...]
    ss = _group_sum(kkr * kkr, _group_ones(V7X_LANES, hd))
    kk = kkr / jnp.maximum(jnp.sqrt(ss), 1e-12)
    ro_ref[...] = r
    vo_ref[...] = v
    ko_ref[...] = k * (1.0 + (a - 1.0) * ka_ref[...])
    ao_ref[...] = -kk
    bo_ref[...] = kk * a


def _rwkv_pre(cfg, proj, lora, mu, w0, w_up, a0, a_up, g_up, k_k, k_a):
    bsz, s, _ = proj.shape
    d = cfg.rwkv_dim
    ts = min(cfg.row_tile // 2, s)
    lw_pad = V7X_LANES
    assert cfg.decay_lora <= lw_pad and cfg.iclr_lora <= lw_pad and (2 * cfg.conv_ch) % d == 0
    lo_w = lora.shape[2]
    c0 = 2 * cfg.conv_ch // d
    pad_rows = lambda w: jnp.zeros((lw_pad, d), F32).at[:w.shape[0]].set(w)
    pad_vec = lambda vv, n: jnp.zeros((1, n), F32).at[0, :vv.shape[0]].set(vv)
    mu_r, mu_k, mu_v = (mu[i * d:(i + 1) * d].reshape(1, d) for i in range(3))
    o = 3 * d
    mu_l = jnp.concatenate([
        pad_vec(mu[o:o + cfg.decay_lora], lw_pad),
        pad_vec(mu[o + cfg.decay_lora:o + cfg.decay_lora + cfg.iclr_lora], lw_pad),
        mu[o + cfg.decay_lora + cfg.iclr_lora:].reshape(1, -1)], axis=1)
    row = lambda cb: pl.BlockSpec((None, ts, d), lambda b, i: (b, i, cb))
    lrow = pl.BlockSpec((None, ts, lo_w), lambda b, i: (b, i, 0))
    vec = lambda n: pl.BlockSpec((1, n), lambda b, i: (0, 0))
    mat = lambda rws: pl.BlockSpec((rws, d), lambda b, i: (0, 0))
    orow = pl.BlockSpec((None, ts, d), lambda b, i: (b, i, 0))
    return pl.pallas_call(
        functools.partial(_rwkv_pre_kernel, ts=ts, hd=cfg.rwkv_head_dim, lw_pad=lw_pad),
        grid=(bsz, s // ts),
        in_specs=[row(c0), row(c0 + 1), row(c0 + 2), lrow, vec(d), vec(d), vec(d), vec(lo_w),
                  vec(d), mat(lw_pad), vec(d), mat(lw_pad), mat(cfg.gate_lora), vec(d), vec(d)],
        out_specs=[orow] * 7,
        out_shape=[jax.ShapeDtypeStruct((bsz, s, d), F32)] * 7,
        scratch_shapes=[pltpu.VMEM((V7X_SUBLANES, d), F32)] * 3 + [pltpu.VMEM((V7X_SUBLANES, lo_w), F32)],
        compiler_params=_cparams(cfg, "parallel", "arbitrary"),
        name="rwkv_token_shift_lora",
    )(proj, proj, proj, lora, mu_r, mu_k, mu_v, mu_l, w0.reshape(1, d), pad_rows(w_up),
      a0.reshape(1, d), pad_rows(a_up), g_up, k_k.reshape(1, d), k_a.reshape(1, d))


def _scan_chunk(r, lw, k, v, a, b, st, consts):
    tri, strict_bd, incl_bd, eye, lane_a, bd = consts
    L = r[0].shape[0]
    each = lambda f, *ls: [f(*xs) for xs in zip(*ls)]
    bf = lambda x: x.astype(BF16)
    stack = lambda x: jnp.concatenate([x, x], axis=0)
    unstack = lambda x: jnp.where(lane_a, x[:L], x[L:])
    left, right = (lambda x: x[:, :V7X_LANES]), (lambda x: x[:, V7X_LANES:])

    def split3(x):
        w1 = bf(x)
        e1 = x - w1.astype(F32)
        w2 = bf(e1)
        return jnp.concatenate([w1, w2, bf(e1 - w2.astype(F32))], axis=1)

    cs = each(lambda x: _dot(tri, split3(x)), lw)
    cum = each(lambda x: x[:, :V7X_LANES] + x[:, V7X_LANES:2 * V7X_LANES] + x[:, 2 * V7X_LANES:], cs)
    cl = each(lambda x: x[L - 1:L, :], cum)
    tail = each(lambda x, y: jnp.exp(x - y), cl, cum)
    at = each(lambda x, c, w: x * jnp.exp(c - w), a, cum, lw)
    rt = each(lambda x, c: x * jnp.exp(c), r, cum)
    g_inv = each(lambda c: jnp.exp(-c), cum)
    bt = each(lambda x, g: stack(bf(x * g)), b, g_inv)
    kt = each(lambda x, g: stack(bf(x * g)), k, g_inv)
    lhs = each(lambda x, y: jnp.concatenate(
        [bf(jnp.where(lane_a, x, 0.0)), bf(jnp.where(lane_a, 0.0, x)),
         bf(jnp.where(lane_a, y, 0.0)), bf(jnp.where(lane_a, 0.0, y))], axis=0), at, rt)
    xb = each(_dot_nt, lhs, bt)
    xk = each(_dot_nt, lhs, kt)
    n = each(lambda x: jnp.where(strict_bd, x[:2 * L], 0.0), xb)
    m = each(lambda x: bf(jnp.where(strict_bd, x[:2 * L], 0.0)), xk)
    p = each(lambda x: bf(jnp.where(incl_bd, x[2 * L:], 0.0)), xb)
    q = each(lambda x: bf(jnp.where(incl_bd, x[2 * L:], 0.0)), xk)
    vb = each(bf, v)
    v_st = each(stack, vb)
    mv = each(_dot, m, v_st)
    w = each(lambda x: eye + x, n)
    pw = each(bf, n)
    for _ in range(int(math.log2(L)) - 1):
        pw = each(lambda x: bf(_dot(x, x)), pw)
        w = each(lambda x, y: x + _dot(bf(x), y), w, pw)
    au = each(lambda ww, x, y: _dot(bf(ww), jnp.concatenate([stack(bf(x)), bf(y)], axis=1)),
              w, at, mv)
    pau = each(lambda x, y: _dot(x, bf(y)), p, au)
    qv = each(_dot, q, v_st)
    rbar = each(lambda x, y: bf(x + unstack(left(y))), rt, pau)
    ybar = each(lambda x, y: unstack(right(x) + y), pau, qv)
    bh = each(lambda x, t: stack(bf(x * t)), b, tail)
    kh = each(lambda x, t: bf(x * t), k, tail)
    abar = each(lambda x: bf(jnp.where(bd, left(x), 0.0)), au)
    ubar = each(lambda x: bf(jnp.where(bd, right(x), 0.0)), au)
    tt = each(lambda x, y: bf(jnp.where(bd, _dot_tn(x, y), 0.0)), abar, bh)
    z = each(lambda u, vv, x, y: jnp.where(bd, _dot_tn(jnp.concatenate([u, vv], axis=0),
                                                       jnp.concatenate([x, y], axis=0)), 0.0),
             ubar, vb, bh, kh)
    sb = each(bf, st)
    y = each(lambda x, s, yb: _dot_nt(x, s) + yb, rbar, sb, ybar)
    st_new = each(lambda s, c, s16, t, zz: s * jnp.exp(c) + _dot(s16, t) + zz, st, cl, sb, tt, z)
    return y, st_new


def _scan_kernel(r_ref, lw_ref, k_ref, v_ref, a_ref, b_ref, y_ref, st_ref, *, ts, L, pairs, hd):
    @pl.when(pl.program_id(2) == 0)
    def _():
        st_ref[...] = jnp.zeros_like(st_ref)

    ri = lax.broadcasted_iota(jnp.int32, (L, L), 0)
    ci = lax.broadcasted_iota(jnp.int32, (L, L), 1)
    lane = lax.broadcasted_iota(jnp.int32, (1, V7X_LANES), 1)
    r2 = lax.broadcasted_iota(jnp.int32, (2 * L, 2 * L), 0)
    c2 = lax.broadcasted_iota(jnp.int32, (2 * L, 2 * L), 1)
    bd = (r2 < L) == (c2 < L)
    rl, cl2 = r2 & (L - 1), c2 & (L - 1)
    consts = (jnp.where(ri >= ci, 1.0, 0.0).astype(BF16), bd & (rl > cl2), bd & (rl >= cl2),
              jnp.where(r2 == c2, 1.0, 0.0), lane < hd, bd)

    def body(c, carry):
        rows = pl.ds(pl.multiple_of(c * L, L), L)
        cols = [slice(p * V7X_LANES, (p + 1) * V7X_LANES) for p in range(pairs)]
        load = lambda ref: [ref[rows, cs] for cs in cols]
        ys, sts = _scan_chunk(load(r_ref), load(lw_ref), load(k_ref), load(v_ref), load(a_ref),
                              load(b_ref), [st_ref[p] for p in range(pairs)], consts)
        for p in range(pairs):
            y_ref[rows, cols[p]] = ys[p]
            st_ref[p] = sts[p]
        return carry

    lax.fori_loop(0, ts // L, body, 0)


def _rwkv_scan(cfg, r, lw, k, v, a, b):
    bsz, s, d = r.shape
    hd = cfg.rwkv_head_dim
    assert 2 * hd == V7X_LANES
    ts, L = min(cfg.scan_rows, s), cfg.scan_chunk
    npairs = d // V7X_LANES
    pairs = min(cfg.scan_pairs, npairs)
    assert s % ts == 0 and ts % L == 0 and npairs % pairs == 0
    blk = pl.BlockSpec((None, ts, pairs * V7X_LANES), lambda bb, p, i: (bb, i, p))
    return pl.pallas_call(
        functools.partial(_scan_kernel, ts=ts, L=L, pairs=pairs, hd=hd),
        grid=(bsz, npairs // pairs, s // ts),
        in_specs=[blk] * 6,
        out_specs=blk,
        out_shape=jax.ShapeDtypeStruct((bsz, s, d), F32),
        scratch_shapes=[pltpu.VMEM((pairs, V7X_LANES, V7X_LANES), F32)],
        compiler_params=_cparams(cfg, "parallel", "parallel", "arbitrary"),
        name="rwkv7_scan",
    )(r, lw, k, v, a, b)


def _rwkv_post_kernel(y_ref, r_ref, k_ref, v_ref, g_ref, rk_ref, lg_ref, lb_ref, o_ref, *, hd):
    gmat = _group_ones(V7X_LANES, hd)
    y = y_ref[...]
    mu = _group_sum(y, gmat) * (1.0 / hd)
    d = y - mu
    var = _group_sum(d * d, gmat) * (1.0 / hd)
    yn = d * lax.rsqrt(var + RWKV_LNX_EPS) * lg_ref[...] + lb_ref[...]
    bonus = _group_sum(r_ref[...] * k_ref[...] * rk_ref[...], gmat) * v_ref[...]
    o_ref[...] = ((yn + bonus) * g_ref[...]).astype(o_ref.dtype)


def _rwkv_post(cfg, y, r, k, v, g, r_k, lnx_g, lnx_b):
    bsz, s, d = y.shape
    ts = min(cfg.row_tile, s)
    row = pl.BlockSpec((None, ts, d), lambda b, i: (b, i, 0))
    vec = pl.BlockSpec((1, d), lambda b, i: (0, 0))
    return pl.pallas_call(
        functools.partial(_rwkv_post_kernel, hd=cfg.rwkv_head_dim),
        grid=(bsz, s // ts),
        in_specs=[row] * 5 + [vec] * 3,
        out_specs=row,
        out_shape=jax.ShapeDtypeStruct((bsz, s, d), BF16),
        compiler_params=_cparams(cfg, "parallel", "parallel"),
        name="rwkv_groupnorm_gate",
    )(y, r, k, v, g, r_k.reshape(1, d), lnx_g.reshape(1, d), lnx_b.reshape(1, d))


def _even_mixer(cfg, h, w_in, w_out):
    bsz, s, d = h.shape
    proj = _matmul(cfg, [h.reshape(bsz * s, d)], w_in, 0, cfg.even_in, F32,
                   cfg.mm_tm, cfg.mm_tn).reshape(bsz, s, cfg.even_in)
    o_m = _moba(cfg, proj).reshape(bsz * s, cfg.dm)
    o_r = _retention(cfg, proj).reshape(bsz * s, cfg.dv)
    return _matmul(cfg, [o_m, o_r], w_out, 0, d, F32, cfg.mm_tm, cfg.mm_tn // 2).reshape(bsz, s, d)


def _odd_mixer(cfg, h, w_in, w_out, conv_w, conv_b, conv_ln_g, conv_ln_b, mu, w0, w_up, a0, a_up,
               g_up, k_k, k_a, r_k, lnx_g, lnx_b):
    bsz, s, d = h.shape
    h2 = h.reshape(bsz * s, d)
    w_nk = jnp.swapaxes(w_in, 1, 2)
    proj = _matmul(cfg, [h2], w_nk, 0, cfg.odd_main, F32,
                   cfg.mm_tm, cfg.mm_tn, w_is_nk=True).reshape(bsz, s, cfg.odd_main)
    lw_pad = V7X_LANES
    wl = w_nk[0, cfg.odd_main:]
    zr = lambda n: jnp.zeros((n, d), F32)
    o1, o2 = cfg.decay_lora, cfg.decay_lora + cfg.iclr_lora
    wl = jnp.concatenate([wl[:o1], zr(lw_pad - cfg.decay_lora), wl[o1:o2],
                          zr(lw_pad - cfg.iclr_lora), wl[o2:]], axis=0)[None]
    lo_w = wl.shape[1]
    lora = _matmul(cfg, [h2], wl, 0, lo_w, F32, cfg.mm_tm, lo_w, w_is_nk=True).reshape(bsz, s, lo_w)

    u = _ln_silu(cfg, _conv_glu(cfg, proj, conv_w, conv_b), conv_ln_g, conv_ln_b)
    r, lw, k, v, a, b, g = _rwkv_pre(cfg, proj, lora, mu, w0, w_up, a0, a_up, g_up, k_k, k_a)
    y = _rwkv_scan(cfg, r, lw, k, v, a, b)
    y = _rwkv_post(cfg, y, r, k, v, g, r_k.reshape(-1), lnx_g, lnx_b)
    return _matmul(cfg, [u.reshape(bsz * s, -1), y.reshape(bsz * s, -1)], w_out, 0, d, F32,
                   cfg.mm_tm, cfg.mm_tn).reshape(bsz, s, d)


def _forward(cfg, x, c, w_ada, b_ada, norm_g, w_ffn_in, w_ffn_out, even_w_in, even_w_out, odd_w_in,
             odd_w_out, conv_w, conv_b, conv_ln_g, conv_ln_b, rwkv_mu, rwkv_w0, rwkv_w_up, rwkv_a0,
             rwkv_a_up, rwkv_g_up, rwkv_k_k, rwkv_k_a, rwkv_r_k, rwkv_lnx_g, rwkv_lnx_b):
    bsz, s, d = x.shape
    depth = w_ada.shape[0]
    mods = _modulation(cfg, c, w_ada, b_ada)
    w_ffn_out_bf = w_ffn_out.astype(BF16)
    sh_m, sc_m = mods[0, :, 0], mods[0, :, 1]
    h = _norm_mod(cfg, x, norm_g[0, 0], sc_m, sh_m)
    for layer in range(depth):
        g_m, sh_f, sc_f, g_f = (mods[layer, :, i] for i in (2, 3, 4, 5))
        j = layer // 2
        if layer % 2 == 0:
            o = _even_mixer(cfg, h, even_w_in[j:j + 1], even_w_out[j:j + 1])
        else:
            o = _odd_mixer(cfg, h, odd_w_in[j:j + 1], odd_w_out[j:j + 1], conv_w[j], conv_b[j],
                           conv_ln_g[j], conv_ln_b[j], rwkv_mu[j], rwkv_w0[j], rwkv_w_up[j],
                           rwkv_a0[j], rwkv_a_up[j], rwkv_g_up[j], rwkv_k_k[j], rwkv_k_a[j],
                           rwkv_r_k[j], rwkv_lnx_g[j], rwkv_lnx_b[j])
        x, h = _resid(cfg, x, o, norm_g[layer, 1], g_m, (norm_g[layer, 2], sc_f, sh_f))
        act = _ffn_in(cfg, h.reshape(bsz * s, d), w_ffn_in, layer)
        f = _matmul(cfg, [act], w_ffn_out_bf, layer, d, F32, cfg.ffn_out_tm,
                    cfg.mm_tn).reshape(bsz, s, d)
        if layer + 1 < depth:
            nxt = (norm_g[layer + 1, 0], mods[layer + 1, :, 1], mods[layer + 1, :, 0])
            x, h = _resid(cfg, x, f, norm_g[layer, 3], g_f, nxt)
        else:
            x = _resid(cfg, x, f, norm_g[layer, 3], g_f)
    return x


def kernel(x, c, w_ada, b_ada, norm_g, w_ffn_in, w_ffn_out, even_w_in, even_w_out, odd_w_in, odd_w_out, conv_w, conv_b, conv_ln_g, conv_ln_b, rwkv_mu, rwkv_w0, rwkv_w_up, rwkv_a0, rwkv_a_up, rwkv_g_up, rwkv_k_k, rwkv_k_a, rwkv_r_k, rwkv_lnx_g, rwkv_lnx_b):
    return _forward(Config(), x, c, w_ada, b_ada, norm_g, w_ffn_in, w_ffn_out, even_w_in, even_w_out,
                    odd_w_in, odd_w_out, conv_w, conv_b, conv_ln_g, conv_ln_b, rwkv_mu, rwkv_w0,
                    rwkv_w_up, rwkv_a0, rwkv_a_up, rwkv_g_up, rwkv_k_k, rwkv_k_a, rwkv_r_k,
                    rwkv_lnx_g, rwkv_lnx_b)
```

```python
import dataclasses
import functools
import math

import jax
import jax.numpy as jnp
from jax import lax
from jax.experimental import pallas as pl
from jax.experimental.pallas import tpu as pltpu

F32 = jnp.float32
BF16 = jnp.bfloat16
HIGHEST = lax.Precision.HIGHEST

V7X_LANES = 128
V7X_SUBLANES = 8
MIB = 1024 * 1024
NORM_EPS = 1e-6
ROPE_THETA = 10000.0
CONV_LN_EPS = 1e-5
MOBA_ONES_ROWS = 16
RWKV_LNX_EPS = 64e-5


@dataclasses.dataclass(frozen=True)
class Config:
    d_model: int = 4096
    moba_heads: int = 16
    moba_head_dim: int = 128
    moba_block: int = 256
    moba_topk: int = 3
    ret_heads: int = 8
    ret_key_dim: int = 256
    ret_val_dim: int = 512
    ret_chunk: int = 128
    conv_ch: int = 2048
    conv_width: int = 31
    rwkv_dim: int = 2048
    rwkv_head_dim: int = 64
    decay_lora: int = 96
    iclr_lora: int = 96
    gate_lora: int = 256
    ffn_hidden: int = 11008
    row_tile: int = 256
    mm_tm: int = 1024
    mm_tn: int = 512
    ffn_tm: int = 2048
    ffn_tn: int = 256
    ffn_out_tm: int = 512
    ret_rows: int = 512
    ret_heads_per_step: int = 2
    conv_rows: int = 256
    conv_cols: int = 256
    scan_rows: int = 512
    scan_chunk: int = 64
    scan_pairs: int = 8
    moba_group: int = 4
    moba_heads_per_step: int = 4
    vmem_mib: int = 56

    @property
    def dm(self):
        return self.moba_heads * self.moba_head_dim

    @property
    def dk(self):
        return self.ret_heads * self.ret_key_dim

    @property
    def dv(self):
        return self.ret_heads * self.ret_val_dim

    @property
    def even_in(self):
        return 3 * self.dm + 2 * self.dk + 2 * self.dv

    @property
    def lora_in(self):
        return self.decay_lora + self.iclr_lora + self.gate_lora

    @property
    def odd_main(self):
        return 2 * self.conv_ch + 3 * self.rwkv_dim


def _cparams(cfg, *sem):
    return pltpu.CompilerParams(dimension_semantics=sem, vmem_limit_bytes=cfg.vmem_mib * MIB)


def _silu(x):
    return x * jax.nn.sigmoid(x)


def _dot(a, b, **kw):
    return jnp.dot(a, b, preferred_element_type=F32, **kw)


def _dot_nt(a, b, **kw):
    return lax.dot_general(a, b, (((1,), (1,)), ((), ())), preferred_element_type=F32, **kw)


def _dot_tn(a, b, **kw):
    return lax.dot_general(a, b, (((0,), (0,)), ((), ())), preferred_element_type=F32, **kw)


def _ada_kernel(c_ref, w_ref, b_ref, o_ref):
    s = _silu(c_ref[...])
    hi = s.astype(BF16).astype(F32)
    parts = _dot(jnp.concatenate([hi, s - hi], axis=0).astype(BF16), w_ref[...].astype(BF16))
    o_ref[...] = parts[:V7X_SUBLANES] + parts[V7X_SUBLANES:] + b_ref[...]


def _modulation(cfg, c, w_ada, b_ada, tn=512):
    depth, d, n = w_ada.shape
    bsz = c.shape[0]
    cp = jnp.zeros((V7X_SUBLANES, d), F32).at[:bsz].set(c)
    out = pl.pallas_call(
        _ada_kernel,
        grid=(depth, n // tn),
        in_specs=[
            pl.BlockSpec((V7X_SUBLANES, d), lambda l, j: (0, 0)),
            pl.BlockSpec((None, d, tn), lambda l, j: (l, 0, j)),
            pl.BlockSpec((None, 1, tn), lambda l, j: (l, 0, j)),
        ],
        out_specs=pl.BlockSpec((None, V7X_SUBLANES, tn), lambda l, j: (l, 0, j)),
        out_shape=jax.ShapeDtypeStruct((depth, V7X_SUBLANES, n), F32),
        compiler_params=_cparams(cfg, "parallel", "parallel"),
        name="adaln_modulation",
    )(cp, w_ada, b_ada.reshape(depth, 1, n))
    return out[:, :bsz].reshape(depth, bsz, 6, 1, d)


def _rms(x, g):
    return x * lax.rsqrt(jnp.mean(x * x, axis=-1, keepdims=True) + NORM_EPS) * g


def _norm_mod_kernel(x_ref, g_ref, sc_ref, sh_ref, o_ref):
    y = _rms(x_ref[...], g_ref[...])
    o_ref[...] = (y * (1.0 + sc_ref[...]) + sh_ref[...]).astype(o_ref.dtype)


def _norm_mod(cfg, x, g, sc, sh):
    bsz, s, d = x.shape
    ts = min(cfg.row_tile, s)
    row = pl.BlockSpec((None, ts, d), lambda b, i: (b, i, 0))
    vec = pl.BlockSpec((1, d), lambda b, i: (0, 0))
    mod = pl.BlockSpec((None, 1, d), lambda b, i: (b, 0, 0))
    return pl.pallas_call(
        _norm_mod_kernel,
        grid=(bsz, s // ts),
        in_specs=[row, vec, mod, mod],
        out_specs=row,
        out_shape=jax.ShapeDtypeStruct((bsz, s, d), BF16),
        compiler_params=_cparams(cfg, "parallel", "parallel"),
        name="norm_modulate",
    )(x, g.reshape(1, d), sc, sh)


def _resid_kernel(x_ref, o_ref, ga_ref, gate_ref, *rest, with_h):
    xn = x_ref[...] + gate_ref[...] * _rms(o_ref[...], ga_ref[...])
    if with_h:
        gb_ref, sc_ref, sh_ref, xn_ref, h_ref = rest
        xn_ref[...] = xn
        h_ref[...] = (_rms(xn, gb_ref[...]) * (1.0 + sc_ref[...]) + sh_ref[...]).astype(h_ref.dtype)
    else:
        (xn_ref,) = rest
        xn_ref[...] = xn


def _resid(cfg, x, o, ga, gate, nxt=None):
    bsz, s, d = x.shape
    ts = min(cfg.row_tile, s)
    row = pl.BlockSpec((None, ts, d), lambda b, i: (b, i, 0))
    vec = pl.BlockSpec((1, d), lambda b, i: (0, 0))
    mod = pl.BlockSpec((None, 1, d), lambda b, i: (b, 0, 0))
    with_h = nxt is not None
    in_specs = [row, row, vec, mod]
    args = [x, o, ga.reshape(1, d), gate]
    out_specs = [row]
    out_shape = [jax.ShapeDtypeStruct((bsz, s, d), F32)]
    if with_h:
        gb, sc, sh = nxt
        in_specs += [vec, mod, mod]
        args += [gb.reshape(1, d), sc, sh]
        out_specs.append(row)
        out_shape.append(jax.ShapeDtypeStruct((bsz, s, d), BF16))
    outs = pl.pallas_call(
        functools.partial(_resid_kernel, with_h=with_h),
        grid=(bsz, s // ts),
        in_specs=in_specs,
        out_specs=out_specs,
        out_shape=out_shape,
        compiler_params=_cparams(cfg, "parallel", "parallel"),
        name="residual_norm",
    )(*args)
    return outs if with_h else outs[0]


def _mm_kernel(*refs, widths, w_is_nk):
    x_refs, (w_ref, o_ref) = refs[:len(widths)], refs[len(widths):]
    acc, off = None, 0
    for x_ref, wd in zip(x_refs, widths):
        if w_is_nk:
            part = _dot_nt(x_ref[...], w_ref[:, off:off + wd].astype(BF16))
        else:
            part = _dot(x_ref[...], w_ref[off:off + wd, :].astype(BF16))
        acc = part if acc is None else acc + part
        off += wd
    o_ref[...] = acc.astype(o_ref.dtype)


def _matmul(cfg, xs, w, layer, n, out_dtype, tm, tn, w_is_nk=False):
    m = xs[0].shape[0]
    widths = tuple(x.shape[1] for x in xs)
    kdim = sum(widths)
    tm, tn = min(tm, m), min(tn, n)
    assert m % tm == 0 and n % tn == 0 and w.shape[2 if w_is_nk else 1] == kdim
    wspec = (pl.BlockSpec((None, tn, kdim), lambda i, j: (layer, j, 0)) if w_is_nk
             else pl.BlockSpec((None, kdim, tn), lambda i, j: (layer, 0, j)))
    return pl.pallas_call(
        functools.partial(_mm_kernel, widths=widths, w_is_nk=w_is_nk),
        grid=(m // tm, n // tn),
        in_specs=[pl.BlockSpec((tm, wd), lambda i, j: (i, 0)) for wd in widths] + [wspec],
        out_specs=pl.BlockSpec((tm, tn), lambda i, j: (i, j)),
        out_shape=jax.ShapeDtypeStruct((m, n), out_dtype),
        compiler_params=_cparams(cfg, "parallel", "parallel"),
        name="matmul",
    )(*xs, w)


def _ffn_in_kernel(x_ref, wg_ref, wu_ref, o_ref):
    x = x_ref[...]
    gate = _dot(x, wg_ref[...].astype(BF16))
    up = _dot(x, wu_ref[...].astype(BF16))
    o_ref[...] = (_silu(gate) * up).astype(o_ref.dtype)


def _ffn_in(cfg, x, w, layer):
    m, kdim = x.shape
    hid = w.shape[2] // 2
    tm, tn = min(cfg.ffn_tm, m), min(cfg.ffn_tn, hid)
    nt = hid // tn
    assert m % tm == 0 and hid % tn == 0
    return pl.pallas_call(
        _ffn_in_kernel,
        grid=(m // tm, nt),
        in_specs=[
            pl.BlockSpec((tm, kdim), lambda i, j: (i, 0)),
            pl.BlockSpec((None, kdim, tn), lambda i, j: (layer, 0, j)),
            pl.BlockSpec((None, kdim, tn), lambda i, j: (layer, 0, nt + j)),
        ],
        out_specs=pl.BlockSpec((tm, tn), lambda i, j: (i, j)),
        out_shape=jax.ShapeDtypeStruct((m, hid), BF16),
        compiler_params=_cparams(cfg, "parallel", "parallel"),
        name="ffn_in_swiglu",
    )(x, w, w)


def _moba_kernel(q_ref, k_ref, v_ref, cq_ref, sq_ref, ck_ref, sk_ref, o_ref,
                 kr_ref, vt_ref, km_ref, sel_ref, *, nb, blk, dh, topk, group, hp):
    qi = pl.program_id(2)
    half = dh // 2
    heads = range(hp)
    lanes = [slice(n * dh, (n + 1) * dh) for n in heads]
    each = lambda f, *ls: [f(*xs) for xs in zip(*ls)]

    @pl.when(qi == 0)
    def _():
        for j in range(nb):
            rows = slice(j * blk, (j + 1) * blk)
            for n in heads:
                kb = k_ref[rows, lanes[n]]
                kr = kb * ck_ref[rows, :] + pltpu.roll(kb, half, 1) * sk_ref[rows, :]
                km_ref[n, j:j + 1, :] = jnp.mean(kr, axis=0, keepdims=True)
                kr_ref[n, j] = kr.astype(BF16)
                vt_ref[n, j, 0:dh, :] = v_ref[rows, lanes[n]].T.astype(BF16)
                vt_ref[n, j, dh:dh + MOBA_ONES_ROWS, :] = jnp.ones((MOBA_ONES_ROWS, blk), BF16)

    cq, sq = cq_ref[...], sq_ref[...]
    qr = [(lambda q: q * cq + pltpu.roll(q, half, 1) * sq)(q_ref[:, lanes[n]]) for n in heads]

    gate = [_dot_nt(km_ref[n], qr[n], precision=HIGHEST) for n in heads]
    brow = lax.broadcasted_iota(jnp.int32, (nb, blk), 0)
    gm = each(lambda g: jnp.where(brow < qi, g, -jnp.inf), gate)
    for j in range(nb):
        below = jnp.where(brow < j, 1.0, 0.0)

        def keep_row(g):
            gj = g[j:j + 1, :]
            beats = jnp.where(g > gj, 1.0, 0.0) + jnp.where(g == gj, below, 0.0)
            keep = jnp.where(jnp.sum(beats, axis=0, keepdims=True) < topk, 1.0, 0.0)
            return jnp.broadcast_to(jnp.where(j < qi, keep, 0.0), (V7X_SUBLANES, blk))

        for n, row in enumerate(each(keep_row, gm)):
            sel_ref[n, j] = row

    qs = each(lambda x: (x * (dh ** -0.5 * math.log2(math.e))).T.astype(BF16), qr)
    kpos = lax.broadcasted_iota(jnp.int32, (blk, blk), 0)
    qpos = lax.broadcasted_iota(jnp.int32, (blk, blk), 1)
    colmax = lambda x: jnp.max(x, axis=0, keepdims=True)
    s = [jnp.where(kpos <= qpos, _dot(kr_ref[n, qi], qs[n]), -jnp.inf) for n in heads]
    m = each(colmax, s)
    p = each(lambda x, y: jnp.exp2(x - y), s, m)
    acc = [_dot(vt_ref[n, qi], p[n].astype(BF16)) for n in heads]

    def body(g, carry):
        m, acc = carry
        js = [g * group + u for u in range(group)]
        ss = [[jnp.where(sel_ref[n, j][0:1, :] > 0.0, _dot(kr_ref[n, j], qs[n]), -jnp.inf)
               for n in heads] for j in js]
        m_new = list(m)
        for su in ss:
            m_new = each(lambda x, y: jnp.maximum(x, colmax(y)), m_new, su)
        acc = each(lambda x, y, a: jnp.exp2(x - y) * a, m, m_new, acc)
        for j, su in zip(js, ss):
            p = each(lambda x, y: jnp.exp2(x - y), su, m_new)
            acc = [acc[n] + _dot(vt_ref[n, j], p[n].astype(BF16)) for n in heads]
        return tuple(m_new), tuple(acc)

    m, acc = lax.fori_loop(0, (qi + group - 1) // group, body, (tuple(m), tuple(acc)))
    for n in heads:
        o_ref[:, lanes[n]] = (acc[n][:dh] / acc[n][dh:dh + 1]).T.astype(o_ref.dtype)


def _rope_tables(seq, dim):
    inv = 1.0 / (ROPE_THETA ** (jnp.arange(0, dim, 2, dtype=F32) / dim))
    ang = jnp.arange(seq, dtype=F32)[:, None] * inv[None, :]
    return jnp.cos(ang), jnp.sin(ang)


def _moba(cfg, proj):
    bsz, s, _ = proj.shape
    h, dh, blk = cfg.moba_heads, cfg.moba_head_dim, cfg.moba_block
    assert dh == V7X_LANES and s % blk == 0
    nb = s // blk
    group = math.gcd(cfg.moba_group, nb)
    hp = math.gcd(cfg.moba_heads_per_step, h)
    hg = h // hp
    cos, sin = _rope_tables(s, dh)
    cosf = jnp.concatenate([cos, cos], axis=1)
    sinf = jnp.concatenate([-sin, sin], axis=1)
    qspec = pl.BlockSpec((None, blk, hp * dh), lambda b, hh, i: (b, i, hh))
    kspec = pl.BlockSpec((None, s, hp * dh), lambda b, hh, i: (b, 0, hg + hh))
    vspec = pl.BlockSpec((None, s, hp * dh), lambda b, hh, i: (b, 0, 2 * hg + hh))
    tq = pl.BlockSpec((blk, dh), lambda b, hh, i: (i, 0))
    tk = pl.BlockSpec((s, dh), lambda b, hh, i: (0, 0))
    return pl.pallas_call(
        functools.partial(_moba_kernel, nb=nb, blk=blk, dh=dh, topk=cfg.moba_topk, group=group,
                          hp=hp),
        grid=(bsz, hg, nb),
        in_specs=[qspec, kspec, vspec, tq, tq, tk, tk],
        out_specs=pl.BlockSpec((None, blk, hp * dh), lambda b, hh, i: (b, i, hh)),
        out_shape=jax.ShapeDtypeStruct((bsz, s, h * dh), BF16),
        scratch_shapes=[
            pltpu.VMEM((hp, nb, blk, dh), BF16),
            pltpu.VMEM((hp, nb, dh + MOBA_ONES_ROWS, blk), BF16),
            pltpu.VMEM((hp, nb, dh), F32),
            pltpu.VMEM((hp, nb, V7X_SUBLANES, blk), F32),
        ],
        compiler_params=_cparams(cfg, "parallel", "parallel", "arbitrary"),
        name="moba_attention",
    )(proj, proj, proj, cosf, sinf, cosf, sinf)


def _ret_kernel(q_ref, k_ref, v_ref, g_ref, cos_ref, sin_ref, dm_ref, qd_ref, kd_ref, cd_ref,
                o_ref, st_ref, *, c, nsub, dk, dv, hp):
    @pl.when(pl.program_id(2) == 0)
    def _():
        st_ref[...] = jnp.zeros_like(st_ref)

    half = dk // 2
    heads = range(hp)
    each = lambda f, *ls: [f(*xs) for xs in zip(*ls)]
    bf = lambda x: x.astype(BF16)
    dm, qd, kd, cd = ([ref[n] for n in heads] for ref in (dm_ref, qd_ref, kd_ref, cd_ref))
    for sidx in range(nsub):
        rows = slice(sidx * c, (sidx + 1) * c)
        cos = cos_ref[rows, :]
        sin = sin_ref[rows, :]

        def rope(x):
            x1, x2 = x[:, :half], x[:, half:]
            return jnp.concatenate([x1 * cos - x2 * sin, x2 * cos + x1 * sin], axis=1)

        q = [rope(q_ref[rows, n * dk:(n + 1) * dk]) for n in heads]
        k = [rope(k_ref[rows, n * dk:(n + 1) * dk]) * (dk ** -0.5) for n in heads]
        qb, kb = each(bf, q), each(bf, k)
        vb = [bf(v_ref[rows, n * dv:(n + 1) * dv]) for n in heads]
        st = [st_ref[n] for n in heads]
        inner = each(lambda x, y, d: bf(_dot_nt(x, y) * d), qb, kb, dm)
        cross = each(lambda x, s, d: _dot(x, bf(s)) * d, qb, st, qd)
        o = each(lambda i, v, x: _dot(i, v) + x, inner, vb, cross)
        kdb = each(lambda x, d: bf(x * d), k, kd)
        new_st = each(lambda s, d, x, v: s * d + _dot_tn(x, v), st, cd, kdb, vb)
        for n in heads:
            st_ref[n] = new_st[n]
        on = each(lambda x: x * lax.rsqrt(jnp.mean(x * x, axis=-1, keepdims=True) + NORM_EPS), o)
        for n in heads:
            cols = slice(n * dv, (n + 1) * dv)
            o_ref[rows, cols] = (on[n] * _silu(g_ref[rows, cols])).astype(o_ref.dtype)


def _retention(cfg, proj):
    bsz, s, _ = proj.shape
    h, dk, dv, c = cfg.ret_heads, cfg.ret_key_dim, cfg.ret_val_dim, cfg.ret_chunk
    ts = min(cfg.ret_rows, s)
    assert s % ts == 0 and ts % c == 0
    q0 = 3 * cfg.dm // dk
    k0 = (3 * cfg.dm + cfg.dk) // dk
    v0 = (3 * cfg.dm + 2 * cfg.dk) // dv
    g0 = (3 * cfg.dm + 2 * cfg.dk + cfg.dv) // dv
    assert (3 * cfg.dm) % dk == 0 and (3 * cfg.dm + 2 * cfg.dk) % dv == 0
    cos, sin = _rope_tables(s, dk)
    log_g = jnp.log1p(-jnp.exp2(-5.0 - jnp.arange(h, dtype=F32)))
    idx = jnp.arange(c, dtype=F32)
    diff = idx[:, None] - idx[None, :]
    dmask = jnp.where(diff >= 0, jnp.exp(jnp.maximum(diff, 0.0) * log_g[:, None, None]), 0.0)
    qdec = jnp.exp((idx + 1.0) * log_g[:, None])[..., None]
    kdec = jnp.exp((c - 1.0 - idx) * log_g[:, None])[..., None]
    cdec = jnp.broadcast_to(jnp.exp(c * log_g)[:, None, None], (h, 1, dv))
    hp = math.gcd(cfg.ret_heads_per_step, h)
    assert q0 % hp == 0 and k0 % hp == 0 and v0 % hp == 0 and g0 % hp == 0
    rowspec = lambda w, c0: pl.BlockSpec((None, ts, hp * w), lambda b, hh, i: (b, i, c0 // hp + hh))
    tab = pl.BlockSpec((ts, dk // 2), lambda b, hh, i: (i, 0))
    const = lambda r, w: pl.BlockSpec((hp, r, w), lambda b, hh, i: (hh, 0, 0))
    return pl.pallas_call(
        functools.partial(_ret_kernel, c=c, nsub=ts // c, dk=dk, dv=dv, hp=hp),
        grid=(bsz, h // hp, s // ts),
        in_specs=[
            rowspec(dk, q0), rowspec(dk, k0), rowspec(dv, v0), rowspec(dv, g0), tab, tab,
            const(c, c), const(c, 1), const(c, 1), const(1, dv),
        ],
        out_specs=pl.BlockSpec((None, ts, hp * dv), lambda b, hh, i: (b, i, hh)),
        out_shape=jax.ShapeDtypeStruct((bsz, s, h * dv), BF16),
        scratch_shapes=[pltpu.VMEM((hp, dk, dv), F32)],
        compiler_params=_cparams(cfg, "parallel", "parallel", "arbitrary"),
        name="retention",
    )(proj, proj, proj, proj, cos, sin, dmask, qdec, kdec, cdec)


CONV_HALO = 32


def _conv_kernel(a_ref, g_ref, w_ref, b_ref, o_ref, buf_ref, sh_ref, *, ts, kw, rc):
    @pl.when(pl.program_id(2) == 0)
    def _():
        buf_ref[0:CONV_HALO, :] = jnp.zeros((CONV_HALO, buf_ref.shape[1]), F32)

    buf_ref[CONV_HALO:CONV_HALO + ts, :] = a_ref[...] * jax.nn.sigmoid(g_ref[...])
    nshift = CONV_HALO + ts - V7X_SUBLANES
    for ph in range(1, V7X_SUBLANES):
        sh_ref[ph, 0:nshift, :] = buf_ref[ph:ph + nshift, :]
    off = CONV_HALO - (kw - 1)
    bias = b_ref[...]
    for r0 in range(0, ts, rc):
        acc = jnp.broadcast_to(bias, (rc, bias.shape[1]))
        for j in range(kw):
            ph = (off + j) % V7X_SUBLANES
            base = off + j - ph + r0
            rows = buf_ref[base:base + rc, :] if ph == 0 else sh_ref[ph, base:base + rc, :]
            acc = acc + w_ref[j:j + 1, :] * rows
        o_ref[r0:r0 + rc, :] = acc
    buf_ref[0:CONV_HALO, :] = buf_ref[ts:ts + CONV_HALO, :]


def _conv_glu(cfg, proj, conv_w, conv_b):
    bsz, s, _ = proj.shape
    ch, kw = cfg.conv_ch, cfg.conv_width
    ts, tc = min(cfg.conv_rows, s), min(cfg.conv_cols, ch)
    assert kw - 1 <= CONV_HALO <= ts and s % ts == 0 and ch % tc == 0
    nct = ch // tc
    wp = jnp.zeros((CONV_HALO, ch), F32).at[:kw].set(conv_w)
    return pl.pallas_call(
        functools.partial(_conv_kernel, ts=ts, kw=kw, rc=32),
        grid=(bsz, nct, s // ts),
        in_specs=[
            pl.BlockSpec((None, ts, tc), lambda b, c, i: (b, i, c)),
            pl.BlockSpec((None, ts, tc), lambda b, c, i: (b, i, nct + c)),
            pl.BlockSpec((CONV_HALO, tc), lambda b, c, i: (0, c)),
            pl.BlockSpec((1, tc), lambda b, c, i: (0, c)),
        ],
        out_specs=pl.BlockSpec((None, ts, tc), lambda b, c, i: (b, i, c)),
        out_shape=jax.ShapeDtypeStruct((bsz, s, ch), F32),
        scratch_shapes=[pltpu.VMEM((CONV_HALO + ts, tc), F32),
                        pltpu.VMEM((V7X_SUBLANES, CONV_HALO + ts, tc), F32)],
        compiler_params=_cparams(cfg, "parallel", "parallel", "arbitrary"),
        name="glu_causal_conv",
    )(proj, proj, wp, conv_b.reshape(1, ch))


def _ln_silu_kernel(x_ref, g_ref, b_ref, o_ref):
    x = x_ref[...]
    mu = jnp.mean(x, axis=-1, keepdims=True)
    d = x - mu
    var = jnp.mean(d * d, axis=-1, keepdims=True)
    y = d * lax.rsqrt(var + CONV_LN_EPS) * g_ref[...] + b_ref[...]
    o_ref[...] = _silu(y).astype(o_ref.dtype)


def _ln_silu(cfg, x, g, b):
    bsz, s, d = x.shape
    ts = min(cfg.row_tile, s)
    row = pl.BlockSpec((None, ts, d), lambda bb, i: (bb, i, 0))
    vec = pl.BlockSpec((1, d), lambda bb, i: (0, 0))
    return pl.pallas_call(
        _ln_silu_kernel,
        grid=(bsz, s // ts),
        in_specs=[row, vec, vec],
        out_specs=row,
        out_shape=jax.ShapeDtypeStruct((bsz, s, d), BF16),
        compiler_params=_cparams(cfg, "parallel", "parallel"),
        name="layernorm_swish",
    )(x, g.reshape(1, d), b.reshape(1, d))


def _group_ones(n, group):
    r = lax.broadcasted_iota(jnp.int32, (n, n), 0)
    c = lax.broadcasted_iota(jnp.int32, (n, n), 1)
    shift = int(math.log2(group))
    return jnp.where((r >> shift) == (c >> shift), 1.0, 0.0).astype(BF16)


def _split2(x):
    hi = x.astype(BF16)
    return hi, (x - hi.astype(F32)).astype(BF16)


def _dot_split(x, w):
    xh, xl = _split2(x)
    wh, wl = _split2(w)
    return _dot(xh, wh) + (_dot(xh, wl) + _dot(xl, wh))


def _group_sum(x, gmat):
    n = x.shape[1]
    hi, lo = _split2(x)
    parts = [_dot(hi[:, s0:s0 + V7X_LANES], gmat) + _dot(lo[:, s0:s0 + V7X_LANES], gmat)
             for s0 in range(0, n, V7X_LANES)]
    return parts[0] if len(parts) == 1 else jnp.concatenate(parts, axis=1)


def _rwkv_pre_kernel(r_ref, k_ref, v_ref, lo_ref, mur_ref, muk_ref, muv_ref, mul_ref,
                     w0_ref, wup_ref, a0_ref, aup_ref, gup_ref, kk_ref, ka_ref,
                     ro_ref, lw_ref, ko_ref, vo_ref, ao_ref, bo_ref, go_ref,
                     lr_ref, lk_ref, lv_ref, ll_ref, *, ts, hd, lw_pad):
    first = pl.program_id(1) == 0

    def shift(x_ref, last_ref, mu_ref):
        @pl.when(first)
        def _():
            last_ref[...] = jnp.zeros_like(last_ref)

        x = x_ref[...]
        row = lax.broadcasted_iota(jnp.int32, x.shape, 0)
        prev = jnp.where(row == 0, last_ref[V7X_SUBLANES - 1:V7X_SUBLANES, :], pltpu.roll(x, 1, 0))
        last_ref[...] = x[ts - V7X_SUBLANES:ts, :]
        return x + (prev - x) * mu_ref[...]

    r = shift(r_ref, lr_ref, mur_ref)
    k = shift(k_ref, lk_ref, muk_ref)
    v = shift(v_ref, lv_ref, muv_ref)
    lo = shift(lo_ref, ll_ref, mul_ref)
    xw, xa, xg = lo[:, :lw_pad], lo[:, lw_pad:2 * lw_pad], lo[:, 2 * lw_pad:]

    z = w0_ref[...] + _dot_split(jnp.tanh(xw), wup_ref[...])
    softplus = jnp.maximum(-z, 0.0) + jnp.log(1.0 + jnp.exp(-jnp.abs(z)))
    lw_ref[...] = -jnp.exp(-softplus - 0.5)
    a = jax.nn.sigmoid(a0_ref[...] + _dot_split(xa, aup_ref[...]))
    go_ref[...] = _dot(jax.nn.sigmoid(xg).astype(BF16), gup_ref[...].astype(BF16))

    kkr = k * kk_ref[...]
    ss = _group_sum(kkr * kkr, _group_ones(V7X_LANES, hd))
    kk = kkr / jnp.maximum(jnp.sqrt(ss), 1e-12)
    ro_ref[...] = r
    vo_ref[...] = v
    ko_ref[...] = k * (1.0 + (a - 1.0) * ka_ref[...])
    ao_ref[...] = -kk
    bo_ref[...] = kk * a


def _rwkv_pre(cfg, proj, lora, mu, w0, w_up, a0, a_up, g_up, k_k, k_a):
    bsz, s, _ = proj.shape
    d = cfg.rwkv_dim
    ts = min(cfg.row_tile // 2, s)
    lw_pad = V7X_LANES
    assert cfg.decay_lora <= lw_pad and cfg.iclr_lora <= lw_pad and (2 * cfg.conv_ch) % d == 0
    lo_w = lora.shape[2]
    c0 = 2 * cfg.conv_ch // d
    pad_rows = lambda w: jnp.zeros((lw_pad, d), F32).at[:w.shape[0]].set(w)
    pad_vec = lambda vv, n: jnp.zeros((1, n), F32).at[0, :vv.shape[0]].set(vv)
    mu_r, mu_k, mu_v = (mu[i * d:(i + 1) * d].reshape(1, d) for i in range(3))
    o = 3 * d
    mu_l = jnp.concatenate([
        pad_vec(mu[o:o + cfg.decay_lora], lw_pad),
        pad_vec(mu[o + cfg.decay_lora:o + cfg.decay_lora + cfg.iclr_lora], lw_pad),
        mu[o + cfg.decay_lora + cfg.iclr_lora:].reshape(1, -1)], axis=1)
    row = lambda cb: pl.BlockSpec((None, ts, d), lambda b, i: (b, i, cb))
    lrow = pl.BlockSpec((None, ts, lo_w), lambda b, i: (b, i, 0))
    vec = lambda n: pl.BlockSpec((1, n), lambda b, i: (0, 0))
    mat = lambda rws: pl.BlockSpec((rws, d), lambda b, i: (0, 0))
    orow = pl.BlockSpec((None, ts, d), lambda b, i: (b, i, 0))
    return pl.pallas_call(
        functools.partial(_rwkv_pre_kernel, ts=ts, hd=cfg.rwkv_head_dim, lw_pad=lw_pad),
        grid=(bsz, s // ts),
        in_specs=[row(c0), row(c0 + 1), row(c0 + 2), lrow, vec(d), vec(d), vec(d), vec(lo_w),
                  vec(d), mat(lw_pad), vec(d), mat(lw_pad), mat(cfg.gate_lora), vec(d), vec(d)],
        out_specs=[orow] * 7,
        out_shape=[jax.ShapeDtypeStruct((bsz, s, d), F32)] * 7,
        scratch_shapes=[pltpu.VMEM((V7X_SUBLANES, d), F32)] * 3 + [pltpu.VMEM((V7X_SUBLANES, lo_w), F32)],
        compiler_params=_cparams(cfg, "parallel", "arbitrary"),
        name="rwkv_token_shift_lora",
    )(proj, proj, proj, lora, mu_r, mu_k, mu_v, mu_l, w0.reshape(1, d), pad_rows(w_up),
      a0.reshape(1, d), pad_rows(a_up), g_up, k_k.reshape(1, d), k_a.reshape(1, d))


def _scan_chunk(r, lw, k, v, a, b, st, consts):
    tri, strict_bd, incl_bd, eye, lane_a, bd = consts
    L = r[0].shape[0]
    each = lambda f, *ls: [f(*xs) for xs in zip(*ls)]
    bf = lambda x: x.astype(BF16)
    stack = lambda x: jnp.concatenate([x, x], axis=0)
    unstack = lambda x: jnp.where(lane_a, x[:L], x[L:])
    left, right = (lambda x: x[:, :V7X_LANES]), (lambda x: x[:, V7X_LANES:])

    def split3(x):
        w1 = bf(x)
        e1 = x - w1.astype(F32)
        w2 = bf(e1)
        return jnp.concatenate([w1, w2, bf(e1 - w2.astype(F32))], axis=1)

    cs = each(lambda x: _dot(tri, split3(x)), lw)
    cum = each(lambda x: x[:, :V7X_LANES] + x[:, V7X_LANES:2 * V7X_LANES] + x[:, 2 * V7X_LANES:], cs)
    cl = each(lambda x: x[L - 1:L, :], cum)
    tail = each(lambda x, y: jnp.exp(x - y), cl, cum)
    at = each(lambda x, c, w: x * jnp.exp(c - w), a, cum, lw)
    rt = each(lambda x, c: x * jnp.exp(c), r, cum)
    g_inv = each(lambda c: jnp.exp(-c), cum)
    bt = each(lambda x, g: stack(bf(x * g)), b, g_inv)
    kt = each(lambda x, g: stack(bf(x * g)), k, g_inv)
    lhs = each(lambda x, y: jnp.concatenate(
        [bf(jnp.where(lane_a, x, 0.0)), bf(jnp.where(lane_a, 0.0, x)),
         bf(jnp.where(lane_a, y, 0.0)), bf(jnp.where(lane_a, 0.0, y))], axis=0), at, rt)
    xb = each(_dot_nt, lhs, bt)
    xk = each(_dot_nt, lhs, kt)
    n = each(lambda x: jnp.where(strict_bd, x[:2 * L], 0.0), xb)
    m = each(lambda x: bf(jnp.where(strict_bd, x[:2 * L], 0.0)), xk)
    p = each(lambda x: bf(jnp.where(incl_bd, x[2 * L:], 0.0)), xb)
    q = each(lambda x: bf(jnp.where(incl_bd, x[2 * L:], 0.0)), xk)
    vb = each(bf, v)
    v_st = each(stack, vb)
    mv = each(_dot, m, v_st)
    w = each(lambda x: eye + x, n)
    pw = each(bf, n)
    for _ in range(int(math.log2(L)) - 1):
        pw = each(lambda x: bf(_dot(x, x)), pw)
        w = each(lambda x, y: x + _dot(bf(x), y), w, pw)
    au = each(lambda ww, x, y: _dot(bf(ww), jnp.concatenate([stack(bf(x)), bf(y)], axis=1)),
              w, at, mv)
    pau = each(lambda x, y: _dot(x, bf(y)), p, au)
    qv = each(_dot, q, v_st)
    rbar = each(lambda x, y: bf(x + unstack(left(y))), rt, pau)
    ybar = each(lambda x, y: unstack(right(x) + y), pau, qv)
    bh = each(lambda x, t: stack(bf(x * t)), b, tail)
    kh = each(lambda x, t: bf(x * t), k, tail)
    abar = each(lambda x: bf(jnp.where(bd, left(x), 0.0)), au)
    ubar = each(lambda x: bf(jnp.where(bd, right(x), 0.0)), au)
    tt = each(lambda x, y: bf(jnp.where(bd, _dot_tn(x, y), 0.0)), abar, bh)
    z = each(lambda u, vv, x, y: jnp.where(bd, _dot_tn(jnp.concatenate([u, vv], axis=0),
                                                       jnp.concatenate([x, y], axis=0)), 0.0),
             ubar, vb, bh, kh)
    sb = each(bf, st)
    y = each(lambda x, s, yb: _dot_nt(x, s) + yb, rbar, sb, ybar)
    st_new = each(lambda s, c, s16, t, zz: s * jnp.exp(c) + _dot(s16, t) + zz, st, cl, sb, tt, z)
    return y, st_new


def _scan_kernel(r_ref, lw_ref, k_ref, v_ref, a_ref, b_ref, y_ref, st_ref, *, ts, L, pairs, hd):
    @pl.when(pl.program_id(2) == 0)
    def _():
        st_ref[...] = jnp.zeros_like(st_ref)

    ri = lax.broadcasted_iota(jnp.int32, (L, L), 0)
    ci = lax.broadcasted_iota(jnp.int32, (L, L), 1)
    lane = lax.broadcasted_iota(jnp.int32, (1, V7X_LANES), 1)
    r2 = lax.broadcasted_iota(jnp.int32, (2 * L, 2 * L), 0)
    c2 = lax.broadcasted_iota(jnp.int32, (2 * L, 2 * L), 1)
    bd = (r2 < L) == (c2 < L)
    rl, cl2 = r2 & (L - 1), c2 & (L - 1)
    consts = (jnp.where(ri >= ci, 1.0, 0.0).astype(BF16), bd & (rl > cl2), bd & (rl >= cl2),
              jnp.where(r2 == c2, 1.0, 0.0), lane < hd, bd)

    def body(c, carry):
        rows = pl.ds(pl.multiple_of(c * L, L), L)
        cols = [slice(p * V7X_LANES, (p + 1) * V7X_LANES) for p in range(pairs)]
        load = lambda ref: [ref[rows, cs] for cs in cols]
        ys, sts = _scan_chunk(load(r_ref), load(lw_ref), load(k_ref), load(v_ref), load(a_ref),
                              load(b_ref), [st_ref[p] for p in range(pairs)], consts)
        for p in range(pairs):
            y_ref[rows, cols[p]] = ys[p]
            st_ref[p] = sts[p]
        return carry

    lax.fori_loop(0, ts // L, body, 0)


def _rwkv_scan(cfg, r, lw, k, v, a, b):
    bsz, s, d = r.shape
    hd = cfg.rwkv_head_dim
    assert 2 * hd == V7X_LANES
    ts, L = min(cfg.scan_rows, s), cfg.scan_chunk
    npairs = d // V7X_LANES
    pairs = min(cfg.scan_pairs, npairs)
    assert s % ts == 0 and ts % L == 0 and npairs % pairs == 0
    blk = pl.BlockSpec((None, ts, pairs * V7X_LANES), lambda bb, p, i: (bb, i, p))
    return pl.pallas_call(
        functools.partial(_scan_kernel, ts=ts, L=L, pairs=pairs, hd=hd),
        grid=(bsz, npairs // pairs, s // ts),
        in_specs=[blk] * 6,
        out_specs=blk,
        out_shape=jax.ShapeDtypeStruct((bsz, s, d), F32),
        scratch_shapes=[pltpu.VMEM((pairs, V7X_LANES, V7X_LANES), F32)],
        compiler_params=_cparams(cfg, "parallel", "parallel", "arbitrary"),
        name="rwkv7_scan",
    )(r, lw, k, v, a, b)


def _rwkv_post_kernel(y_ref, r_ref, k_ref, v_ref, g_ref, rk_ref, lg_ref, lb_ref, o_ref, *, hd):
    gmat = _group_ones(V7X_LANES, hd)
    y = y_ref[...]
    mu = _group_sum(y, gmat) * (1.0 / hd)
    d = y - mu
    var = _group_sum(d * d, gmat) * (1.0 / hd)
    yn = d * lax.rsqrt(var + RWKV_LNX_EPS) * lg_ref[...] + lb_ref[...]
    bonus = _group_sum(r_ref[...] * k_ref[...] * rk_ref[...], gmat) * v_ref[...]
    o_ref[...] = ((yn + bonus) * g_ref[...]).astype(o_ref.dtype)


def _rwkv_post(cfg, y, r, k, v, g, r_k, lnx_g, lnx_b):
    bsz, s, d = y.shape
    ts = min(cfg.row_tile, s)
    row = pl.BlockSpec((None, ts, d), lambda b, i: (b, i, 0))
    vec = pl.BlockSpec((1, d), lambda b, i: (0, 0))
    return pl.pallas_call(
        functools.partial(_rwkv_post_kernel, hd=cfg.rwkv_head_dim),
        grid=(bsz, s // ts),
        in_specs=[row] * 5 + [vec] * 3,
        out_specs=row,
        out_shape=jax.ShapeDtypeStruct((bsz, s, d), BF16),
        compiler_params=_cparams(cfg, "parallel", "parallel"),
        name="rwkv_groupnorm_gate",
    )(y, r, k, v, g, r_k.reshape(1, d), lnx_g.reshape(1, d), lnx_b.reshape(1, d))


def _even_mixer(cfg, h, w_in, w_out):
    bsz, s, d = h.shape
    proj = _matmul(cfg, [h.reshape(bsz * s, d)], w_in, 0, cfg.even_in, F32,
                   cfg.mm_tm, cfg.mm_tn).reshape(bsz, s, cfg.even_in)
    o_m = _moba(cfg, proj).reshape(bsz * s, cfg.dm)
    o_r = _retention(cfg, proj).reshape(bsz * s, cfg.dv)
    return _matmul(cfg, [o_m, o_r], w_out, 0, d, F32, cfg.mm_tm, cfg.mm_tn // 2).reshape(bsz, s, d)


def _odd_mixer(cfg, h, w_in, w_out, conv_w, conv_b, conv_ln_g, conv_ln_b, mu, w0, w_up, a0, a_up,
               g_up, k_k, k_a, r_k, lnx_g, lnx_b):
    bsz, s, d = h.shape
    h2 = h.reshape(bsz * s, d)
    w_nk = jnp.swapaxes(w_in, 1, 2)
    proj = _matmul(cfg, [h2], w_nk, 0, cfg.odd_main, F32,
                   cfg.mm_tm, cfg.mm_tn, w_is_nk=True).reshape(bsz, s, cfg.odd_main)
    lw_pad = V7X_LANES
    wl = w_nk[0, cfg.odd_main:]
    zr = lambda n: jnp.zeros((n, d), F32)
    o1, o2 = cfg.decay_lora, cfg.decay_lora + cfg.iclr_lora
    wl = jnp.concatenate([wl[:o1], zr(lw_pad - cfg.decay_lora), wl[o1:o2],
                          zr(lw_pad - cfg.iclr_lora), wl[o2:]], axis=0)[None]
    lo_w = wl.shape[1]
    lora = _matmul(cfg, [h2], wl, 0, lo_w, F32, cfg.mm_tm, lo_w, w_is_nk=True).reshape(bsz, s, lo_w)

    u = _ln_silu(cfg, _conv_glu(cfg, proj, conv_w, conv_b), conv_ln_g, conv_ln_b)
    r, lw, k, v, a, b, g = _rwkv_pre(cfg, proj, lora, mu, w0, w_up, a0, a_up, g_up, k_k, k_a)
    y = _rwkv_scan(cfg, r, lw, k, v, a, b)
    y = _rwkv_post(cfg, y, r, k, v, g, r_k.reshape(-1), lnx_g, lnx_b)
    return _matmul(cfg, [u.reshape(bsz * s, -1), y.reshape(bsz * s, -1)], w_out, 0, d, F32,
                   cfg.mm_tm, cfg.mm_tn).reshape(bsz, s, d)


def _forward(cfg, x, c, w_ada, b_ada, norm_g, w_ffn_in, w_ffn_out, even_w_in, even_w_out, odd_w_in,
             odd_w_out, conv_w, conv_b, conv_ln_g, conv_ln_b, rwkv_mu, rwkv_w0, rwkv_w_up, rwkv_a0,
             rwkv_a_up, rwkv_g_up, rwkv_k_k, rwkv_k_a, rwkv_r_k, rwkv_lnx_g, rwkv_lnx_b):
    bsz, s, d = x.shape
    depth = w_ada.shape[0]
    mods = _modulation(cfg, c, w_ada, b_ada)
    w_ffn_out_bf = w_ffn_out.astype(BF16)
    sh_m, sc_m = mods[0, :, 0], mods[0, :, 1]
    h = _norm_mod(cfg, x, norm_g[0, 0], sc_m, sh_m)
    for layer in range(depth):
        g_m, sh_f, sc_f, g_f = (mods[layer, :, i] for i in (2, 3, 4, 5))
        j = layer // 2
        if layer % 2 == 0:
            o = _even_mixer(cfg, h, even_w_in[j:j + 1], even_w_out[j:j + 1])
        else:
            o = _odd_mixer(cfg, h, odd_w_in[j:j + 1], odd_w_out[j:j + 1], conv_w[j], conv_b[j],
                           conv_ln_g[j], conv_ln_b[j], rwkv_mu[j], rwkv_w0[j], rwkv_w_up[j],
                           rwkv_a0[j], rwkv_a_up[j], rwkv_g_up[j], rwkv_k_k[j], rwkv_k_a[j],
                           rwkv_r_k[j], rwkv_lnx_g[j], rwkv_lnx_b[j])
        x, h = _resid(cfg, x, o, norm_g[layer, 1], g_m, (norm_g[layer, 2], sc_f, sh_f))
        act = _ffn_in(cfg, h.reshape(bsz * s, d), w_ffn_in, layer)
        f = _matmul(cfg, [act], w_ffn_out_bf, layer, d, F32, cfg.ffn_out_tm,
                    cfg.mm_tn).reshape(bsz, s, d)
        if layer + 1 < depth:
            nxt = (norm_g[layer + 1, 0], mods[layer + 1, :, 1], mods[layer + 1, :, 0])
            x, h = _resid(cfg, x, f, norm_g[layer, 3], g_f, nxt)
        else:
            x = _resid(cfg, x, f, norm_g[layer, 3], g_f)
    return x


def kernel(x, c, w_ada, b_ada, norm_g, w_ffn_in, w_ffn_out, even_w_in, even_w_out, odd_w_in, odd_w_out, conv_w, conv_b, conv_ln_g, conv_ln_b, rwkv_mu, rwkv_w0, rwkv_w_up, rwkv_a0, rwkv_a_up, rwkv_g_up, rwkv_k_k, rwkv_k_a, rwkv_r_k, rwkv_lnx_g, rwkv_lnx_b):
    return _forward(Config(), x, c, w_ada, b_ada, norm_g, w_ffn_in, w_ffn_out, even_w_in, even_w_out,
                    odd_w_in, odd_w_out, conv_w, conv_b, conv_ln_g, conv_ln_b, rwkv_mu, rwkv_w0,
                    rwkv_w_up, rwkv_a0, rwkv_a_up, rwkv_g_up, rwkv_k_k, rwkv_k_a, rwkv_r_k,
                    rwkv_lnx_g, rwkv_lnx_b)
```

```python
import dataclasses
import functools
import math

import jax
import jax.numpy as jnp
from jax import lax
from jax.experimental import pallas as pl
from jax.experimental.pallas import tpu as pltpu

F32 = jnp.float32
BF16 = jnp.bfloat16
HIGHEST = lax.Precision.HIGHEST

V7X_LANES = 128
V7X_SUBLANES = 8
MIB = 1024 * 1024
NORM_EPS = 1e-6
ROPE_THETA = 10000.0
CONV_LN_EPS = 1e-5
MOBA_ONES_ROWS = 16
RWKV_LNX_EPS = 64e-5


@dataclasses.dataclass(frozen=True)
class Config:
    d_model: int = 4096
    moba_heads: int = 16
    moba_head_dim: int = 128
    moba_block: int = 256
    moba_topk: int = 3
    ret_heads: int = 8
    ret_key_dim: int = 256
    ret_val_dim: int = 512
    ret_chunk: int = 128
    conv_ch: int = 2048
    conv_width: int = 31
    rwkv_dim: int = 2048
    rwkv_head_dim: int = 64
    decay_lora: int = 96
    iclr_lora: int = 96
    gate_lora: int = 256
    ffn_hidden: int = 11008
    row_tile: int = 256
    mm_tm: int = 2048
    ffn_out_tm: int = 1024
    mm_tn: int = 512
    ffn_tm: int = 2048
    ffn_tn: int = 256
    ret_rows: int = 512
    ret_heads_per_step: int = 2
    conv_rows: int = 256
    conv_cols: int = 256
    scan_rows: int = 512
    scan_chunk: int = 64
    scan_pairs: int = 8
    moba_group: int = 4
    moba_heads_per_step: int = 4
    vmem_mib: int = 56

    @property
    def dm(self):
        return self.moba_heads * self.moba_head_dim

    @property
    def dk(self):
        return self.ret_heads * self.ret_key_dim

    @property
    def dv(self):
        return self.ret_heads * self.ret_val_dim

    @property
    def even_in(self):
        return 3 * self.dm + 2 * self.dk + 2 * self.dv

    @property
    def lora_in(self):
        return self.decay_lora + self.iclr_lora + self.gate_lora

    @property
    def odd_main(self):
        return 2 * self.conv_ch + 3 * self.rwkv_dim


def _cparams(cfg, *sem):
    return pltpu.CompilerParams(dimension_semantics=sem, vmem_limit_bytes=cfg.vmem_mib * MIB)


def _silu(x):
    return x * jax.nn.sigmoid(x)


def _dot(a, b, **kw):
    return jnp.dot(a, b, preferred_element_type=F32, **kw)


def _dot_nt(a, b, **kw):
    return lax.dot_general(a, b, (((1,), (1,)), ((), ())), preferred_element_type=F32, **kw)


def _dot_tn(a, b, **kw):
    return lax.dot_general(a, b, (((0,), (0,)), ((), ())), preferred_element_type=F32, **kw)


def _ada_kernel(c_ref, w_ref, b_ref, o_ref):
    s = _silu(c_ref[...])
    hi = s.astype(BF16).astype(F32)
    parts = _dot(jnp.concatenate([hi, s - hi], axis=0).astype(BF16), w_ref[...].astype(BF16))
    o_ref[...] = parts[:V7X_SUBLANES] + parts[V7X_SUBLANES:] + b_ref[...]


def _modulation(cfg, c, w_ada, b_ada, tn=512):
    depth, d, n = w_ada.shape
    bsz = c.shape[0]
    cp = jnp.zeros((V7X_SUBLANES, d), F32).at[:bsz].set(c)
    out = pl.pallas_call(
        _ada_kernel,
        grid=(depth, n // tn),
        in_specs=[
            pl.BlockSpec((V7X_SUBLANES, d), lambda l, j: (0, 0)),
            pl.BlockSpec((None, d, tn), lambda l, j: (l, 0, j)),
            pl.BlockSpec((None, 1, tn), lambda l, j: (l, 0, j)),
        ],
        out_specs=pl.BlockSpec((None, V7X_SUBLANES, tn), lambda l, j: (l, 0, j)),
        out_shape=jax.ShapeDtypeStruct((depth, V7X_SUBLANES, n), F32),
        compiler_params=_cparams(cfg, "parallel", "parallel"),
        name="adaln_modulation",
    )(cp, w_ada, b_ada.reshape(depth, 1, n))
    return out[:, :bsz].reshape(depth, bsz, 6, 1, d)


def _rms(x, g):
    return x * lax.rsqrt(jnp.mean(x * x, axis=-1, keepdims=True) + NORM_EPS) * g


def _norm_mod_kernel(x_ref, g_ref, sc_ref, sh_ref, o_ref):
    y = _rms(x_ref[...], g_ref[...])
    o_ref[...] = (y * (1.0 + sc_ref[...]) + sh_ref[...]).astype(o_ref.dtype)


def _norm_mod(cfg, x, g, sc, sh):
    bsz, s, d = x.shape
    ts = min(cfg.row_tile, s)
    row = pl.BlockSpec((None, ts, d), lambda b, i: (b, i, 0))
    vec = pl.BlockSpec((1, d), lambda b, i: (0, 0))
    mod = pl.BlockSpec((None, 1, d), lambda b, i: (b, 0, 0))
    return pl.pallas_call(
        _norm_mod_kernel,
        grid=(bsz, s // ts),
        in_specs=[row, vec, mod, mod],
        out_specs=row,
        out_shape=jax.ShapeDtypeStruct((bsz, s, d), BF16),
        compiler_params=_cparams(cfg, "parallel", "parallel"),
        name="norm_modulate",
    )(x, g.reshape(1, d), sc, sh)


def _resid_kernel(x_ref, o_ref, ga_ref, gate_ref, *rest, with_h):
    xn = x_ref[...] + gate_ref[...] * _rms(o_ref[...], ga_ref[...])
    if with_h:
        gb_ref, sc_ref, sh_ref, xn_ref, h_ref = rest
        xn_ref[...] = xn
        h_ref[...] = (_rms(xn, gb_ref[...]) * (1.0 + sc_ref[...]) + sh_ref[...]).astype(h_ref.dtype)
    else:
        (xn_ref,) = rest
        xn_ref[...] = xn


def _resid(cfg, x, o, ga, gate, nxt=None):
    bsz, s, d = x.shape
    ts = min(cfg.row_tile, s)
    row = pl.BlockSpec((None, ts, d), lambda b, i: (b, i, 0))
    vec = pl.BlockSpec((1, d), lambda b, i: (0, 0))
    mod = pl.BlockSpec((None, 1, d), lambda b, i: (b, 0, 0))
    with_h = nxt is not None
    in_specs = [row, row, vec, mod]
    args = [x, o, ga.reshape(1, d), gate]
    out_specs = [row]
    out_shape = [jax.ShapeDtypeStruct((bsz, s, d), F32)]
    if with_h:
        gb, sc, sh = nxt
        in_specs += [vec, mod, mod]
        args += [gb.reshape(1, d), sc, sh]
        out_specs.append(row)
        out_shape.append(jax.ShapeDtypeStruct((bsz, s, d), BF16))
    outs = pl.pallas_call(
        functools.partial(_resid_kernel, with_h=with_h),
        grid=(bsz, s // ts),
        in_specs=in_specs,
        out_specs=out_specs,
        out_shape=out_shape,
        compiler_params=_cparams(cfg, "parallel", "parallel"),
        name="residual_norm",
    )(*args)
    return outs if with_h else outs[0]


def _mm_kernel(*refs, widths, w_is_nk):
    x_refs, (w_ref, o_ref) = refs[:len(widths)], refs[len(widths):]
    acc, off = None, 0
    for x_ref, wd in zip(x_refs, widths):
        if w_is_nk:
            part = _dot_nt(x_ref[...], w_ref[:, off:off + wd].astype(BF16))
        else:
            part = _dot(x_ref[...], w_ref[off:off + wd, :].astype(BF16))
        acc = part if acc is None else acc + part
        off += wd
    o_ref[...] = acc.astype(o_ref.dtype)


def _matmul(cfg, xs, w, layer, n, out_dtype, tm, tn, w_is_nk=False, single_buffer_x=False):
    m = xs[0].shape[0]
    widths = tuple(x.shape[1] for x in xs)
    kdim = sum(widths)
    tm, tn = min(tm, m), min(tn, n)
    assert m % tm == 0 and n % tn == 0 and w.shape[2 if w_is_nk else 1] == kdim
    wspec = (pl.BlockSpec((None, tn, kdim), lambda i, j: (layer, j, 0)) if w_is_nk
             else pl.BlockSpec((None, kdim, tn), lambda i, j: (layer, 0, j)))
    xmode = dict(pipeline_mode=pl.Buffered(1)) if single_buffer_x else {}
    return pl.pallas_call(
        functools.partial(_mm_kernel, widths=widths, w_is_nk=w_is_nk),
        grid=(m // tm, n // tn),
        in_specs=[pl.BlockSpec((tm, wd), lambda i, j: (i, 0), **xmode) for wd in widths] + [wspec],
        out_specs=pl.BlockSpec((tm, tn), lambda i, j: (i, j)),
        out_shape=jax.ShapeDtypeStruct((m, n), out_dtype),
        compiler_params=_cparams(cfg, "parallel", "parallel"),
        name="matmul",
    )(*xs, w)


def _ffn_in_kernel(x_ref, wg_ref, wu_ref, o_ref):
    x = x_ref[...]
    gate = _dot(x, wg_ref[...].astype(BF16))
    up = _dot(x, wu_ref[...].astype(BF16))
    o_ref[...] = (_silu(gate) * up).astype(o_ref.dtype)


def _ffn_in(cfg, x, w, layer):
    m, kdim = x.shape
    hid = w.shape[2] // 2
    tm, tn = min(cfg.ffn_tm, m), min(cfg.ffn_tn, hid)
    nt = hid // tn
    assert m % tm == 0 and hid % tn == 0
    return pl.pallas_call(
        _ffn_in_kernel,
        grid=(m // tm, nt),
        in_specs=[
            pl.BlockSpec((tm, kdim), lambda i, j: (i, 0)),
            pl.BlockSpec((None, kdim, tn), lambda i, j: (layer, 0, j)),
            pl.BlockSpec((None, kdim, tn), lambda i, j: (layer, 0, nt + j)),
        ],
        out_specs=pl.BlockSpec((tm, tn), lambda i, j: (i, j)),
        out_shape=jax.ShapeDtypeStruct((m, hid), BF16),
        compiler_params=_cparams(cfg, "parallel", "parallel"),
        name="ffn_in_swiglu",
    )(x, w, w)


def _moba_kernel(q_ref, k_ref, v_ref, cq_ref, sq_ref, ck_ref, sk_ref, o_ref,
                 kr_ref, vt_ref, km_ref, sel_ref, *, nb, blk, dh, topk, group, hp):
    qi = pl.program_id(2)
    half = dh // 2
    heads = range(hp)
    lanes = [slice(n * dh, (n + 1) * dh) for n in heads]
    each = lambda f, *ls: [f(*xs) for xs in zip(*ls)]

    @pl.when(qi == 0)
    def _():
        for j in range(nb):
            rows = slice(j * blk, (j + 1) * blk)
            for n in heads:
                kb = k_ref[rows, lanes[n]]
                kr = kb * ck_ref[rows, :] + pltpu.roll(kb, half, 1) * sk_ref[rows, :]
                km_ref[n, j:j + 1, :] = jnp.mean(kr, axis=0, keepdims=True)
                kr_ref[n, j] = kr.astype(BF16)
                vt_ref[n, j, 0:dh, :] = v_ref[rows, lanes[n]].T.astype(BF16)
                vt_ref[n, j, dh:dh + MOBA_ONES_ROWS, :] = jnp.ones((MOBA_ONES_ROWS, blk), BF16)

    cq, sq = cq_ref[...], sq_ref[...]
    qr = [(lambda q: q * cq + pltpu.roll(q, half, 1) * sq)(q_ref[:, lanes[n]]) for n in heads]

    gate = [_dot_nt(km_ref[n], qr[n], precision=HIGHEST) for n in heads]
    brow = lax.broadcasted_iota(jnp.int32, (nb, blk), 0)
    gm = each(lambda g: jnp.where(brow < qi, g, -jnp.inf), gate)
    for j in range(nb):
        below = jnp.where(brow < j, 1.0, 0.0)

        def keep_row(g):
            gj = g[j:j + 1, :]
            beats = jnp.where(g > gj, 1.0, 0.0) + jnp.where(g == gj, below, 0.0)
            keep = jnp.where(jnp.sum(beats, axis=0, keepdims=True) < topk, 1.0, 0.0)
            return jnp.broadcast_to(jnp.where(j < qi, keep, 0.0), (V7X_SUBLANES, blk))

        for n, row in enumerate(each(keep_row, gm)):
            sel_ref[n, j] = row

    qs = each(lambda x: (x * (dh ** -0.5 * math.log2(math.e))).T.astype(BF16), qr)
    kpos = lax.broadcasted_iota(jnp.int32, (blk, blk), 0)
    qpos = lax.broadcasted_iota(jnp.int32, (blk, blk), 1)
    colmax = lambda x: jnp.max(x, axis=0, keepdims=True)
    s = [jnp.where(kpos <= qpos, _dot(kr_ref[n, qi], qs[n]), -jnp.inf) for n in heads]
    m = each(colmax, s)
    p = each(lambda x, y: jnp.exp2(x - y), s, m)
    acc = [_dot(vt_ref[n, qi], p[n].astype(BF16)) for n in heads]

    def body(g, carry):
        m, acc = carry
        js = [g * group + u for u in range(group)]
        ss = [[jnp.where(sel_ref[n, j][0:1, :] > 0.0, _dot(kr_ref[n, j], qs[n]), -jnp.inf)
               for n in heads] for j in js]
        m_new = list(m)
        for su in ss:
            m_new = each(lambda x, y: jnp.maximum(x, colmax(y)), m_new, su)
        acc = each(lambda x, y, a: jnp.exp2(x - y) * a, m, m_new, acc)
        for j, su in zip(js, ss):
            p = each(lambda x, y: jnp.exp2(x - y), su, m_new)
            acc = [acc[n] + _dot(vt_ref[n, j], p[n].astype(BF16)) for n in heads]
        return tuple(m_new), tuple(acc)

    m, acc = lax.fori_loop(0, (qi + group - 1) // group, body, (tuple(m), tuple(acc)))
    for n in heads:
        o_ref[:, lanes[n]] = (acc[n][:dh] / acc[n][dh:dh + 1]).T.astype(o_ref.dtype)


def _rope_tables(seq, dim):
    inv = 1.0 / (ROPE_THETA ** (jnp.arange(0, dim, 2, dtype=F32) / dim))
    ang = jnp.arange(seq, dtype=F32)[:, None] * inv[None, :]
    return jnp.cos(ang), jnp.sin(ang)


def _moba(cfg, proj):
    bsz, s, _ = proj.shape
    h, dh, blk = cfg.moba_heads, cfg.moba_head_dim, cfg.moba_block
    assert dh == V7X_LANES and s % blk == 0
    nb = s // blk
    group = math.gcd(cfg.moba_group, nb)
    hp = math.gcd(cfg.moba_heads_per_step, h)
    hg = h // hp
    cos, sin = _rope_tables(s, dh)
    cosf = jnp.concatenate([cos, cos], axis=1)
    sinf = jnp.concatenate([-sin, sin], axis=1)
    qspec = pl.BlockSpec((None, blk, hp * dh), lambda b, hh, i: (b, i, hh))
    kspec = pl.BlockSpec((None, s, hp * dh), lambda b, hh, i: (b, 0, hg + hh))
    vspec = pl.BlockSpec((None, s, hp * dh), lambda b, hh, i: (b, 0, 2 * hg + hh))
    tq = pl.BlockSpec((blk, dh), lambda b, hh, i: (i, 0))
    tk = pl.BlockSpec((s, dh), lambda b, hh, i: (0, 0))
    return pl.pallas_call(
        functools.partial(_moba_kernel, nb=nb, blk=blk, dh=dh, topk=cfg.moba_topk, group=group,
                          hp=hp),
        grid=(bsz, hg, nb),
        in_specs=[qspec, kspec, vspec, tq, tq, tk, tk],
        out_specs=pl.BlockSpec((None, blk, hp * dh), lambda b, hh, i: (b, i, hh)),
        out_shape=jax.ShapeDtypeStruct((bsz, s, h * dh), BF16),
        scratch_shapes=[
            pltpu.VMEM((hp, nb, blk, dh), BF16),
            pltpu.VMEM((hp, nb, dh + MOBA_ONES_ROWS, blk), BF16),
            pltpu.VMEM((hp, nb, dh), F32),
            pltpu.VMEM((hp, nb, V7X_SUBLANES, blk), F32),
        ],
        compiler_params=_cparams(cfg, "parallel", "parallel", "arbitrary"),
        name="moba_attention",
    )(proj, proj, proj, cosf, sinf, cosf, sinf)


def _ret_kernel(q_ref, k_ref, v_ref, g_ref, cos_ref, sin_ref, dm_ref, qd_ref, kd_ref, cd_ref,
                o_ref, st_ref, *, c, nsub, dk, dv, hp):
    @pl.when(pl.program_id(2) == 0)
    def _():
        st_ref[...] = jnp.zeros_like(st_ref)

    half = dk // 2
    heads = range(hp)
    each = lambda f, *ls: [f(*xs) for xs in zip(*ls)]
    bf = lambda x: x.astype(BF16)
    dm, qd, kd, cd = ([ref[n] for n in heads] for ref in (dm_ref, qd_ref, kd_ref, cd_ref))
    for sidx in range(nsub):
        rows = slice(sidx * c, (sidx + 1) * c)
        cos = cos_ref[rows, :]
        sin = sin_ref[rows, :]

        def rope(x):
            x1, x2 = x[:, :half], x[:, half:]
            return jnp.concatenate([x1 * cos - x2 * sin, x2 * cos + x1 * sin], axis=1)

        q = [rope(q_ref[rows, n * dk:(n + 1) * dk]) for n in heads]
        k = [rope(k_ref[rows, n * dk:(n + 1) * dk]) * (dk ** -0.5) for n in heads]
        qb, kb = each(bf, q), each(bf, k)
        vb = [bf(v_ref[rows, n * dv:(n + 1) * dv]) for n in heads]
        st = [st_ref[n] for n in heads]
        inner = each(lambda x, y, d: bf(_dot_nt(x, y) * d), qb, kb, dm)
        cross = each(lambda x, s, d: _dot(x, bf(s)) * d, qb, st, qd)
        o = each(lambda i, v, x: _dot(i, v) + x, inner, vb, cross)
        kdb = each(lambda x, d: bf(x * d), k, kd)
        new_st = each(lambda s, d, x, v: s * d + _dot_tn(x, v), st, cd, kdb, vb)
        for n in heads:
            st_ref[n] = new_st[n]
        on = each(lambda x: x * lax.rsqrt(jnp.mean(x * x, axis=-1, keepdims=True) + NORM_EPS), o)
        for n in heads:
            cols = slice(n * dv, (n + 1) * dv)
            o_ref[rows, cols] = (on[n] * _silu(g_ref[rows, cols])).astype(o_ref.dtype)


def _retention(cfg, proj):
    bsz, s, _ = proj.shape
    h, dk, dv, c = cfg.ret_heads, cfg.ret_key_dim, cfg.ret_val_dim, cfg.ret_chunk
    ts = min(cfg.ret_rows, s)
    assert s % ts == 0 and ts % c == 0
    q0 = 3 * cfg.dm // dk
    k0 = (3 * cfg.dm + cfg.dk) // dk
    v0 = (3 * cfg.dm + 2 * cfg.dk) // dv
    g0 = (3 * cfg.dm + 2 * cfg.dk + cfg.dv) // dv
    assert (3 * cfg.dm) % dk == 0 and (3 * cfg.dm + 2 * cfg.dk) % dv == 0
    cos, sin = _rope_tables(s, dk)
    log_g = jnp.log1p(-jnp.exp2(-5.0 - jnp.arange(h, dtype=F32)))
    idx = jnp.arange(c, dtype=F32)
    diff = idx[:, None] - idx[None, :]
    dmask = jnp.where(diff >= 0, jnp.exp(jnp.maximum(diff, 0.0) * log_g[:, None, None]), 0.0)
    qdec = jnp.exp((idx + 1.0) * log_g[:, None])[..., None]
    kdec = jnp.exp((c - 1.0 - idx) * log_g[:, None])[..., None]
    cdec = jnp.broadcast_to(jnp.exp(c * log_g)[:, None, None], (h, 1, dv))
    hp = math.gcd(cfg.ret_heads_per_step, h)
    assert q0 % hp == 0 and k0 % hp == 0 and v0 % hp == 0 and g0 % hp == 0
    rowspec = lambda w, c0: pl.BlockSpec((None, ts, hp * w), lambda b, hh, i: (b, i, c0 // hp + hh))
    tab = pl.BlockSpec((ts, dk // 2), lambda b, hh, i: (i, 0))
    const = lambda r, w: pl.BlockSpec((hp, r, w), lambda b, hh, i: (hh, 0, 0))
    return pl.pallas_call(
        functools.partial(_ret_kernel, c=c, nsub=ts // c, dk=dk, dv=dv, hp=hp),
        grid=(bsz, h // hp, s // ts),
        in_specs=[
            rowspec(dk, q0), rowspec(dk, k0), rowspec(dv, v0), rowspec(dv, g0), tab, tab,
            const(c, c), const(c, 1), const(c, 1), const(1, dv),
        ],
        out_specs=pl.BlockSpec((None, ts, hp * dv), lambda b, hh, i: (b, i, hh)),
        out_shape=jax.ShapeDtypeStruct((bsz, s, h * dv), BF16),
        scratch_shapes=[pltpu.VMEM((hp, dk, dv), F32)],
        compiler_params=_cparams(cfg, "parallel", "parallel", "arbitrary"),
        name="retention",
    )(proj, proj, proj, proj, cos, sin, dmask, qdec, kdec, cdec)


CONV_HALO = 32


def _conv_kernel(a_ref, g_ref, w_ref, b_ref, o_ref, buf_ref, sh_ref, *, ts, kw, rc):
    @pl.when(pl.program_id(2) == 0)
    def _():
        buf_ref[0:CONV_HALO, :] = jnp.zeros((CONV_HALO, buf_ref.shape[1]), F32)

    buf_ref[CONV_HALO:CONV_HALO + ts, :] = a_ref[...] * jax.nn.sigmoid(g_ref[...])
    nshift = CONV_HALO + ts - V7X_SUBLANES
    for ph in range(1, V7X_SUBLANES):
        sh_ref[ph, 0:nshift, :] = buf_ref[ph:ph + nshift, :]
    off = CONV_HALO - (kw - 1)
    bias = b_ref[...]
    for r0 in range(0, ts, rc):
        acc = jnp.broadcast_to(bias, (rc, bias.shape[1]))
        for j in range(kw):
            ph = (off + j) % V7X_SUBLANES
            base = off + j - ph + r0
            rows = buf_ref[base:base + rc, :] if ph == 0 else sh_ref[ph, base:base + rc, :]
            acc = acc + w_ref[j:j + 1, :] * rows
        o_ref[r0:r0 + rc, :] = acc
    buf_ref[0:CONV_HALO, :] = buf_ref[ts:ts + CONV_HALO, :]


def _conv_glu(cfg, proj, conv_w, conv_b):
    bsz, s, _ = proj.shape
    ch, kw = cfg.conv_ch, cfg.conv_width
    ts, tc = min(cfg.conv_rows, s), min(cfg.conv_cols, ch)
    assert kw - 1 <= CONV_HALO <= ts and s % ts == 0 and ch % tc == 0
    nct = ch // tc
    wp = jnp.zeros((CONV_HALO, ch), F32).at[:kw].set(conv_w)
    return pl.pallas_call(
        functools.partial(_conv_kernel, ts=ts, kw=kw, rc=32),
        grid=(bsz, nct, s // ts),
        in_specs=[
            pl.BlockSpec((None, ts, tc), lambda b, c, i: (b, i, c)),
            pl.BlockSpec((None, ts, tc), lambda b, c, i: (b, i, nct + c)),
            pl.BlockSpec((CONV_HALO, tc), lambda b, c, i: (0, c)),
            pl.BlockSpec((1, tc), lambda b, c, i: (0, c)),
        ],
        out_specs=pl.BlockSpec((None, ts, tc), lambda b, c, i: (b, i, c)),
        out_shape=jax.ShapeDtypeStruct((bsz, s, ch), F32),
        scratch_shapes=[pltpu.VMEM((CONV_HALO + ts, tc), F32),
                        pltpu.VMEM((V7X_SUBLANES, CONV_HALO + ts, tc), F32)],
        compiler_params=_cparams(cfg, "parallel", "parallel", "arbitrary"),
        name="glu_causal_conv",
    )(proj, proj, wp, conv_b.reshape(1, ch))


def _ln_silu_kernel(x_ref, g_ref, b_ref, o_ref):
    x = x_ref[...]
    mu = jnp.mean(x, axis=-1, keepdims=True)
    d = x - mu
    var = jnp.mean(d * d, axis=-1, keepdims=True)
    y = d * lax.rsqrt(var + CONV_LN_EPS) * g_ref[...] + b_ref[...]
    o_ref[...] = _silu(y).astype(o_ref.dtype)


def _ln_silu(cfg, x, g, b):
    bsz, s, d = x.shape
    ts = min(cfg.row_tile, s)
    row = pl.BlockSpec((None, ts, d), lambda bb, i: (bb, i, 0))
    vec = pl.BlockSpec((1, d), lambda bb, i: (0, 0))
    return pl.pallas_call(
        _ln_silu_kernel,
        grid=(bsz, s // ts),
        in_specs=[row, vec, vec],
        out_specs=row,
        out_shape=jax.ShapeDtypeStruct((bsz, s, d), BF16),
        compiler_params=_cparams(cfg, "parallel", "parallel"),
        name="layernorm_swish",
    )(x, g.reshape(1, d), b.reshape(1, d))


def _group_ones(n, group):
    r = lax.broadcasted_iota(jnp.int32, (n, n), 0)
    c = lax.broadcasted_iota(jnp.int32, (n, n), 1)
    shift = int(math.log2(group))
    return jnp.where((r >> shift) == (c >> shift), 1.0, 0.0).astype(BF16)


def _split2(x):
    hi = x.astype(BF16)
    return hi, (x - hi.astype(F32)).astype(BF16)


def _dot_split(x, w):
    xh, xl = _split2(x)
    wh, wl = _split2(w)
    return _dot(xh, wh) + (_dot(xh, wl) + _dot(xl, wh))


def _group_sum(x, gmat):
    n = x.shape[1]
    hi, lo = _split2(x)
    parts = [_dot(hi[:, s0:s0 + V7X_LANES], gmat) + _dot(lo[:, s0:s0 + V7X_LANES], gmat)
             for s0 in range(0, n, V7X_LANES)]
    return parts[0] if len(parts) == 1 else jnp.concatenate(parts, axis=1)


def _rwkv_pre_kernel(r_ref, k_ref, v_ref, lo_ref, mur_ref, muk_ref, muv_ref, mul_ref,
                     w0_ref, wup_ref, a0_ref, aup_ref, gup_ref, kk_ref, ka_ref,
                     ro_ref, lw_ref, ko_ref, vo_ref, ao_ref, bo_ref, go_ref,
                     lr_ref, lk_ref, lv_ref, ll_ref, *, ts, hd, lw_pad):
    first = pl.program_id(1) == 0

    def shift(x_ref, last_ref, mu_ref):
        @pl.when(first)
        def _():
            last_ref[...] = jnp.zeros_like(last_ref)

        x = x_ref[...]
        row = lax.broadcasted_iota(jnp.int32, x.shape, 0)
        prev = jnp.where(row == 0, last_ref[V7X_SUBLANES - 1:V7X_SUBLANES, :], pltpu.roll(x, 1, 0))
        last_ref[...] = x[ts - V7X_SUBLANES:ts, :]
        return x + (prev - x) * mu_ref[...]

    r = shift(r_ref, lr_ref, mur_ref)
    k = shift(k_ref, lk_ref, muk_ref)
    v = shift(v_ref, lv_ref, muv_ref)
    lo = shift(lo_ref, ll_ref, mul_ref)
    xw, xa, xg = lo[:, :lw_pad], lo[:, lw_pad:2 * lw_pad], lo[:, 2 * lw_pad:]

    z = w0_ref[...] + _dot_split(jnp.tanh(xw), wup_ref[...])
    softplus = jnp.maximum(-z, 0.0) + jnp.log(1.0 + jnp.exp(-jnp.abs(z)))
    lw_ref[...] = -jnp.exp(-softplus - 0.5)
    a = jax.nn.sigmoid(a0_ref[...] + _dot_split(xa, aup_ref[...]))
    go_ref[...] = _dot(jax.nn.sigmoid(xg).astype(BF16), gup_ref[...].astype(BF16))

    kkr = k * kk_ref[...]
    ss = _group_sum(kkr * kkr, _group_ones(V7X_LANES, hd))
    kk = kkr / jnp.maximum(jnp.sqrt(ss), 1e-12)
    ro_ref[...] = r
    vo_ref[...] = v
    ko_ref[...] = k * (1.0 + (a - 1.0) * ka_ref[...])
    ao_ref[...] = -kk
    bo_ref[...] = kk * a


def _rwkv_pre(cfg, proj, lora, mu, w0, w_up, a0, a_up, g_up, k_k, k_a):
    bsz, s, _ = proj.shape
    d = cfg.rwkv_dim
    ts = min(cfg.row_tile // 2, s)
    lw_pad = V7X_LANES
    assert cfg.decay_lora <= lw_pad and cfg.iclr_lora <= lw_pad and (2 * cfg.conv_ch) % d == 0
    lo_w = lora.shape[2]
    c0 = 2 * cfg.conv_ch // d
    pad_rows = lambda w: jnp.zeros((lw_pad, d), F32).at[:w.shape[0]].set(w)
    pad_vec = lambda vv, n: jnp.zeros((1, n), F32).at[0, :vv.shape[0]].set(vv)
    mu_r, mu_k, mu_v = (mu[i * d:(i + 1) * d].reshape(1, d) for i in range(3))
    o = 3 * d
    mu_l = jnp.concatenate([
        pad_vec(mu[o:o + cfg.decay_lora], lw_pad),
        pad_vec(mu[o + cfg.decay_lora:o + cfg.decay_lora + cfg.iclr_lora], lw_pad),
        mu[o + cfg.decay_lora + cfg.iclr_lora:].reshape(1, -1)], axis=1)
    row = lambda cb: pl.BlockSpec((None, ts, d), lambda b, i: (b, i, cb))
    lrow = pl.BlockSpec((None, ts, lo_w), lambda b, i: (b, i, 0))
    vec = lambda n: pl.BlockSpec((1, n), lambda b, i: (0, 0))
    mat = lambda rws: pl.BlockSpec((rws, d), lambda b, i: (0, 0))
    orow = pl.BlockSpec((None, ts, d), lambda b, i: (b, i, 0))
    return pl.pallas_call(
        functools.partial(_rwkv_pre_kernel, ts=ts, hd=cfg.rwkv_head_dim, lw_pad=lw_pad),
        grid=(bsz, s // ts),
        in_specs=[row(c0), row(c0 + 1), row(c0 + 2), lrow, vec(d), vec(d), vec(d), vec(lo_w),
                  vec(d), mat(lw_pad), vec(d), mat(lw_pad), mat(cfg.gate_lora), vec(d), vec(d)],
        out_specs=[orow] * 7,
        out_shape=[jax.ShapeDtypeStruct((bsz, s, d), F32)] * 7,
        scratch_shapes=[pltpu.VMEM((V7X_SUBLANES, d), F32)] * 3 + [pltpu.VMEM((V7X_SUBLANES, lo_w), F32)],
        compiler_params=_cparams(cfg, "parallel", "arbitrary"),
        name="rwkv_token_shift_lora",
    )(proj, proj, proj, lora, mu_r, mu_k, mu_v, mu_l, w0.reshape(1, d), pad_rows(w_up),
      a0.reshape(1, d), pad_rows(a_up), g_up, k_k.reshape(1, d), k_a.reshape(1, d))


def _scan_chunk(r, lw, k, v, a, b, st, consts):
    tri, strict_bd, incl_bd, eye, lane_a, bd = consts
    L = r[0].shape[0]
    each = lambda f, *ls: [f(*xs) for xs in zip(*ls)]
    bf = lambda x: x.astype(BF16)
    stack = lambda x: jnp.concatenate([x, x], axis=0)
    unstack = lambda x: jnp.where(lane_a, x[:L], x[L:])
    left, right = (lambda x: x[:, :V7X_LANES]), (lambda x: x[:, V7X_LANES:])

    def split3(x):
        w1 = bf(x)
        e1 = x - w1.astype(F32)
        w2 = bf(e1)
        return jnp.concatenate([w1, w2, bf(e1 - w2.astype(F32))], axis=1)

    cs = each(lambda x: _dot(tri, split3(x)), lw)
    cum = each(lambda x: x[:, :V7X_LANES] + x[:, V7X_LANES:2 * V7X_LANES] + x[:, 2 * V7X_LANES:], cs)
    cl = each(lambda x: x[L - 1:L, :], cum)
    tail = each(lambda x, y: jnp.exp(x - y), cl, cum)
    at = each(lambda x, c, w: x * jnp.exp(c - w), a, cum, lw)
    rt = each(lambda x, c: x * jnp.exp(c), r, cum)
    g_inv = each(lambda c: jnp.exp(-c), cum)
    bt = each(lambda x, g: stack(bf(x * g)), b, g_inv)
    kt = each(lambda x, g: stack(bf(x * g)), k, g_inv)
    lhs = each(lambda x, y: jnp.concatenate(
        [bf(jnp.where(lane_a, x, 0.0)), bf(jnp.where(lane_a, 0.0, x)),
         bf(jnp.where(lane_a, y, 0.0)), bf(jnp.where(lane_a, 0.0, y))], axis=0), at, rt)
    xb = each(_dot_nt, lhs, bt)
    xk = each(_dot_nt, lhs, kt)
    n = each(lambda x: jnp.where(strict_bd, x[:2 * L], 0.0), xb)
    m = each(lambda x: bf(jnp.where(strict_bd, x[:2 * L], 0.0)), xk)
    p = each(lambda x: bf(jnp.where(incl_bd, x[2 * L:], 0.0)), xb)
    q = each(lambda x: bf(jnp.where(incl_bd, x[2 * L:], 0.0)), xk)
    vb = each(bf, v)
    v_st = each(stack, vb)
    mv = each(_dot, m, v_st)
    w = each(lambda x: eye + x, n)
    pw = each(bf, n)
    for _ in range(int(math.log2(L)) - 1):
        pw = each(lambda x: bf(_dot(x, x)), pw)
        w = each(lambda x, y: x + _dot(bf(x), y), w, pw)
    au = each(lambda ww, x, y: _dot(bf(ww), jnp.concatenate([stack(bf(x)), bf(y)], axis=1)),
              w, at, mv)
    pau = each(lambda x, y: _dot(x, bf(y)), p, au)
    qv = each(_dot, q, v_st)
    rbar = each(lambda x, y: bf(x + unstack(left(y))), rt, pau)
    ybar = each(lambda x, y: unstack(right(x) + y), pau, qv)
    bh = each(lambda x, t: stack(bf(x * t)), b, tail)
    kh = each(lambda x, t: bf(x * t), k, tail)
    abar = each(lambda x: bf(jnp.where(bd, left(x), 0.0)), au)
    ubar = each(lambda x: bf(jnp.where(bd, right(x), 0.0)), au)
    tt = each(lambda x, y: bf(jnp.where(bd, _dot_tn(x, y), 0.0)), abar, bh)
    z = each(lambda u, vv, x, y: jnp.where(bd, _dot_tn(jnp.concatenate([u, vv], axis=0),
                                                       jnp.concatenate([x, y], axis=0)), 0.0),
             ubar, vb, bh, kh)
    sb = each(bf, st)
    y = each(lambda x, s, yb: _dot_nt(x, s) + yb, rbar, sb, ybar)
    st_new = each(lambda s, c, s16, t, zz: s * jnp.exp(c) + _dot(s16, t) + zz, st, cl, sb, tt, z)
    return y, st_new


def _scan_kernel(r_ref, lw_ref, k_ref, v_ref, a_ref, b_ref, y_ref, st_ref, *, ts, L, pairs, hd):
    @pl.when(pl.program_id(2) == 0)
    def _():
        st_ref[...] = jnp.zeros_like(st_ref)

    ri = lax.broadcasted_iota(jnp.int32, (L, L), 0)
    ci = lax.broadcasted_iota(jnp.int32, (L, L), 1)
    lane = lax.broadcasted_iota(jnp.int32, (1, V7X_LANES), 1)
    r2 = lax.broadcasted_iota(jnp.int32, (2 * L, 2 * L), 0)
    c2 = lax.broadcasted_iota(jnp.int32, (2 * L, 2 * L), 1)
    bd = (r2 < L) == (c2 < L)
    rl, cl2 = r2 & (L - 1), c2 & (L - 1)
    consts = (jnp.where(ri >= ci, 1.0, 0.0).astype(BF16), bd & (rl > cl2), bd & (rl >= cl2),
              jnp.where(r2 == c2, 1.0, 0.0), lane < hd, bd)

    def body(c, carry):
        rows = pl.ds(pl.multiple_of(c * L, L), L)
        cols = [slice(p * V7X_LANES, (p + 1) * V7X_LANES) for p in range(pairs)]
        load = lambda ref: [ref[rows, cs] for cs in cols]
        ys, sts = _scan_chunk(load(r_ref), load(lw_ref), load(k_ref), load(v_ref), load(a_ref),
                              load(b_ref), [st_ref[p] for p in range(pairs)], consts)
        for p in range(pairs):
            y_ref[rows, cols[p]] = ys[p]
            st_ref[p] = sts[p]
        return carry

    lax.fori_loop(0, ts // L, body, 0)


def _rwkv_scan(cfg, r, lw, k, v, a, b):
    bsz, s, d = r.shape
    hd = cfg.rwkv_head_dim
    assert 2 * hd == V7X_LANES
    ts, L = min(cfg.scan_rows, s), cfg.scan_chunk
    npairs = d // V7X_LANES
    pairs = min(cfg.scan_pairs, npairs)
    assert s % ts == 0 and ts % L == 0 and npairs % pairs == 0
    blk = pl.BlockSpec((None, ts, pairs * V7X_LANES), lambda bb, p, i: (bb, i, p))
    return pl.pallas_call(
        functools.partial(_scan_kernel, ts=ts, L=L, pairs=pairs, hd=hd),
        grid=(bsz, npairs // pairs, s // ts),
        in_specs=[blk] * 6,
        out_specs=blk,
        out_shape=jax.ShapeDtypeStruct((bsz, s, d), F32),
        scratch_shapes=[pltpu.VMEM((pairs, V7X_LANES, V7X_LANES), F32)],
        compiler_params=_cparams(cfg, "parallel", "parallel", "arbitrary"),
        name="rwkv7_scan",
    )(r, lw, k, v, a, b)


def _rwkv_post_kernel(y_ref, r_ref, k_ref, v_ref, g_ref, rk_ref, lg_ref, lb_ref, o_ref, *, hd):
    gmat = _group_ones(V7X_LANES, hd)
    y = y_ref[...]
    mu = _group_sum(y, gmat) * (1.0 / hd)
    d = y - mu
    var = _group_sum(d * d, gmat) * (1.0 / hd)
    yn = d * lax.rsqrt(var + RWKV_LNX_EPS) * lg_ref[...] + lb_ref[...]
    bonus = _group_sum(r_ref[...] * k_ref[...] * rk_ref[...], gmat) * v_ref[...]
    o_ref[...] = ((yn + bonus) * g_ref[...]).astype(o_ref.dtype)


def _rwkv_post(cfg, y, r, k, v, g, r_k, lnx_g, lnx_b):
    bsz, s, d = y.shape
    ts = min(cfg.row_tile, s)
    row = pl.BlockSpec((None, ts, d), lambda b, i: (b, i, 0))
    vec = pl.BlockSpec((1, d), lambda b, i: (0, 0))
    return pl.pallas_call(
        functools.partial(_rwkv_post_kernel, hd=cfg.rwkv_head_dim),
        grid=(bsz, s // ts),
        in_specs=[row] * 5 + [vec] * 3,
        out_specs=row,
        out_shape=jax.ShapeDtypeStruct((bsz, s, d), BF16),
        compiler_params=_cparams(cfg, "parallel", "parallel"),
        name="rwkv_groupnorm_gate",
    )(y, r, k, v, g, r_k.reshape(1, d), lnx_g.reshape(1, d), lnx_b.reshape(1, d))


def _even_mixer(cfg, h, w_in, w_out):
    bsz, s, d = h.shape
    proj = _matmul(cfg, [h.reshape(bsz * s, d)], w_in, 0, cfg.even_in, F32,
                   cfg.mm_tm, cfg.mm_tn, single_buffer_x=True).reshape(bsz, s, cfg.even_in)
    o_m = _moba(cfg, proj).reshape(bsz * s, cfg.dm)
    o_r = _retention(cfg, proj).reshape(bsz * s, cfg.dv)
    return _matmul(cfg, [o_m, o_r], w_out, 0, d, F32, cfg.mm_tm // 2,
                   cfg.mm_tn // 2).reshape(bsz, s, d)


def _odd_mixer(cfg, h, w_in, w_out, conv_w, conv_b, conv_ln_g, conv_ln_b, mu, w0, w_up, a0, a_up,
               g_up, k_k, k_a, r_k, lnx_g, lnx_b):
    bsz, s, d = h.shape
    h2 = h.reshape(bsz * s, d)
    w_nk = jnp.swapaxes(w_in, 1, 2)
    proj = _matmul(cfg, [h2], w_nk, 0, cfg.odd_main, F32,
                   cfg.mm_tm, cfg.mm_tn, w_is_nk=True, single_buffer_x=True).reshape(bsz, s, cfg.odd_main)
    lw_pad = V7X_LANES
    wl = w_nk[0, cfg.odd_main:]
    zr = lambda n: jnp.zeros((n, d), F32)
    o1, o2 = cfg.decay_lora, cfg.decay_lora + cfg.iclr_lora
    wl = jnp.concatenate([wl[:o1], zr(lw_pad - cfg.decay_lora), wl[o1:o2],
                          zr(lw_pad - cfg.iclr_lora), wl[o2:]], axis=0)[None]
    lo_w = wl.shape[1]
    lora = _matmul(cfg, [h2], wl, 0, lo_w, F32, cfg.mm_tm, lo_w, w_is_nk=True,
                   single_buffer_x=True).reshape(bsz, s, lo_w)

    u = _ln_silu(cfg, _conv_glu(cfg, proj, conv_w, conv_b), conv_ln_g, conv_ln_b)
    r, lw, k, v, a, b, g = _rwkv_pre(cfg, proj, lora, mu, w0, w_up, a0, a_up, g_up, k_k, k_a)
    y = _rwkv_scan(cfg, r, lw, k, v, a, b)
    y = _rwkv_post(cfg, y, r, k, v, g, r_k.reshape(-1), lnx_g, lnx_b)
    return _matmul(cfg, [u.reshape(bsz * s, -1), y.reshape(bsz * s, -1)], w_out, 0, d, F32,
                   cfg.mm_tm, cfg.mm_tn, single_buffer_x=True).reshape(bsz, s, d)


def _forward(cfg, x, c, w_ada, b_ada, norm_g, w_ffn_in, w_ffn_out, even_w_in, even_w_out, odd_w_in,
             odd_w_out, conv_w, conv_b, conv_ln_g, conv_ln_b, rwkv_mu, rwkv_w0, rwkv_w_up, rwkv_a0,
             rwkv_a_up, rwkv_g_up, rwkv_k_k, rwkv_k_a, rwkv_r_k, rwkv_lnx_g, rwkv_lnx_b):
    bsz, s, d = x.shape
    depth = w_ada.shape[0]
    mods = _modulation(cfg, c, w_ada, b_ada)
    sh_m, sc_m = mods[0, :, 0], mods[0, :, 1]
    h = _norm_mod(cfg, x, norm_g[0, 0], sc_m, sh_m)
    for layer in range(depth):
        g_m, sh_f, sc_f, g_f = (mods[layer, :, i] for i in (2, 3, 4, 5))
        j = layer // 2
        if layer % 2 == 0:
            o = _even_mixer(cfg, h, even_w_in[j:j + 1], even_w_out[j:j + 1])
        else:
            o = _odd_mixer(cfg, h, odd_w_in[j:j + 1], odd_w_out[j:j + 1], conv_w[j], conv_b[j],
                           conv_ln_g[j], conv_ln_b[j], rwkv_mu[j], rwkv_w0[j], rwkv_w_up[j],
                           rwkv_a0[j], rwkv_a_up[j], rwkv_g_up[j], rwkv_k_k[j], rwkv_k_a[j],
                           rwkv_r_k[j], rwkv_lnx_g[j], rwkv_lnx_b[j])
        x, h = _resid(cfg, x, o, norm_g[layer, 1], g_m, (norm_g[layer, 2], sc_f, sh_f))
        act = _ffn_in(cfg, h.reshape(bsz * s, d), w_ffn_in, layer)
        f = _matmul(cfg, [act], w_ffn_out, layer, d, F32, cfg.ffn_out_tm, cfg.ffn_tn,
                    single_buffer_x=True).reshape(bsz, s, d)
        if layer + 1 < depth:
            nxt = (norm_g[layer + 1, 0], mods[layer + 1, :, 1], mods[layer + 1, :, 0])
            x, h = _resid(cfg, x, f, norm_g[layer, 3], g_f, nxt)
        else:
            x = _resid(cfg, x, f, norm_g[layer, 3], g_f)
    return x


def kernel(x, c, w_ada, b_ada, norm_g, w_ffn_in, w_ffn_out, even_w_in, even_w_out, odd_w_in, odd_w_out, conv_w, conv_b, conv_ln_g, conv_ln_b, rwkv_mu, rwkv_w0, rwkv_w_up, rwkv_a0, rwkv_a_up, rwkv_g_up, rwkv_k_k, rwkv_k_a, rwkv_r_k, rwkv_lnx_g, rwkv_lnx_b):
    return _forward(Config(), x, c, w_ada, b_ada, norm_g, w_ffn_in, w_ffn_out, even_w_in, even_w_out,
                    odd_w_in, odd_w_out, conv_w, conv_b, conv_ln_g, conv_ln_b, rwkv_mu, rwkv_w0,
                    rwkv_w_up, rwkv_a0, rwkv_a_up, rwkv_g_up, rwkv_k_k, rwkv_k_a, rwkv_r_k,
                    rwkv_lnx_g, rwkv_lnx_b)
```

```python
import dataclasses
import functools
import math

import jax
import jax.numpy as jnp
from jax import lax
from jax.experimental import pallas as pl
from jax.experimental.pallas import tpu as pltpu

F32 = jnp.float32
BF16 = jnp.bfloat16
HIGHEST = lax.Precision.HIGHEST

V7X_LANES = 128
V7X_SUBLANES = 8
MIB = 1024 * 1024
NORM_EPS = 1e-6
ROPE_THETA = 10000.0
CONV_LN_EPS = 1e-5
MOBA_ONES_ROWS = 16
RWKV_LNX_EPS = 64e-5


@dataclasses.dataclass(frozen=True)
class Config:
    d_model: int = 4096
    moba_heads: int = 16
    moba_head_dim: int = 128
    moba_block: int = 256
    moba_topk: int = 3
    ret_heads: int = 8
    ret_key_dim: int = 256
    ret_val_dim: int = 512
    ret_chunk: int = 128
    conv_ch: int = 2048
    conv_width: int = 31
    rwkv_dim: int = 2048
    rwkv_head_dim: int = 64
    decay_lora: int = 96
    iclr_lora: int = 96
    gate_lora: int = 256
    ffn_hidden: int = 11008
    row_tile: int = 256
    mm_tm: int = 2048
    ffn_out_tm: int = 1024
    mm_tn: int = 512
    ffn_tm: int = 2048
    ffn_tn: int = 256
    ret_rows: int = 512
    ret_heads_per_step: int = 2
    conv_rows: int = 256
    conv_cols: int = 256
    scan_rows: int = 512
    scan_chunk: int = 64
    scan_pairs: int = 8
    moba_group: int = 4
    moba_heads_per_step: int = 4
    vmem_mib: int = 56

    @property
    def dm(self):
        return self.moba_heads * self.moba_head_dim

    @property
    def dk(self):
        return self.ret_heads * self.ret_key_dim

    @property
    def dv(self):
        return self.ret_heads * self.ret_val_dim

    @property
    def even_in(self):
        return 3 * self.dm + 2 * self.dk + 2 * self.dv

    @property
    def lora_in(self):
        return self.decay_lora + self.iclr_lora + self.gate_lora

    @property
    def odd_main(self):
        return 2 * self.conv_ch + 3 * self.rwkv_dim


def _cparams(cfg, *sem):
    return pltpu.CompilerParams(dimension_semantics=sem, vmem_limit_bytes=cfg.vmem_mib * MIB)


def _silu(x):
    return x * jax.nn.sigmoid(x)


def _dot(a, b, **kw):
    return jnp.dot(a, b, preferred_element_type=F32, **kw)


def _dot_nt(a, b, **kw):
    return lax.dot_general(a, b, (((1,), (1,)), ((), ())), preferred_element_type=F32, **kw)


def _dot_tn(a, b, **kw):
    return lax.dot_general(a, b, (((0,), (0,)), ((), ())), preferred_element_type=F32, **kw)


def _ada_kernel(c_ref, w_ref, b_ref, o_ref):
    s = _silu(c_ref[...])
    hi = s.astype(BF16).astype(F32)
    parts = _dot(jnp.concatenate([hi, s - hi], axis=0).astype(BF16), w_ref[...].astype(BF16))
    o_ref[...] = parts[:V7X_SUBLANES] + parts[V7X_SUBLANES:] + b_ref[...]


def _modulation(cfg, c, w_ada, b_ada, tn=512):
    depth, d, n = w_ada.shape
    bsz = c.shape[0]
    cp = jnp.zeros((V7X_SUBLANES, d), F32).at[:bsz].set(c)
    out = pl.pallas_call(
        _ada_kernel,
        grid=(depth, n // tn),
        in_specs=[
            pl.BlockSpec((V7X_SUBLANES, d), lambda l, j: (0, 0)),
            pl.BlockSpec((None, d, tn), lambda l, j: (l, 0, j)),
            pl.BlockSpec((None, 1, tn), lambda l, j: (l, 0, j)),
        ],
        out_specs=pl.BlockSpec((None, V7X_SUBLANES, tn), lambda l, j: (l, 0, j)),
        out_shape=jax.ShapeDtypeStruct((depth, V7X_SUBLANES, n), F32),
        compiler_params=_cparams(cfg, "parallel", "parallel"),
        name="adaln_modulation",
    )(cp, w_ada, b_ada.reshape(depth, 1, n))
    return out[:, :bsz].reshape(depth, bsz, 6, 1, d)


def _rms(x, g):
    return x * lax.rsqrt(jnp.mean(x * x, axis=-1, keepdims=True) + NORM_EPS) * g


def _norm_mod_kernel(x_ref, g_ref, sc_ref, sh_ref, o_ref):
    y = _rms(x_ref[...], g_ref[...])
    o_ref[...] = (y * (1.0 + sc_ref[...]) + sh_ref[...]).astype(o_ref.dtype)


def _norm_mod(cfg, x, g, sc, sh):
    bsz, s, d = x.shape
    ts = min(cfg.row_tile, s)
    row = pl.BlockSpec((None, ts, d), lambda b, i: (b, i, 0))
    vec = pl.BlockSpec((1, d), lambda b, i: (0, 0))
    mod = pl.BlockSpec((None, 1, d), lambda b, i: (b, 0, 0))
    return pl.pallas_call(
        _norm_mod_kernel,
        grid=(bsz, s // ts),
        in_specs=[row, vec, mod, mod],
        out_specs=row,
        out_shape=jax.ShapeDtypeStruct((bsz, s, d), BF16),
        compiler_params=_cparams(cfg, "parallel", "parallel"),
        name="norm_modulate",
    )(x, g.reshape(1, d), sc, sh)


def _resid_kernel(x_ref, o_ref, ga_ref, gate_ref, *rest, with_h):
    xn = x_ref[...] + gate_ref[...] * _rms(o_ref[...].astype(F32), ga_ref[...])
    if with_h:
        gb_ref, sc_ref, sh_ref, xn_ref, h_ref = rest
        xn_ref[...] = xn
        h_ref[...] = (_rms(xn, gb_ref[...]) * (1.0 + sc_ref[...]) + sh_ref[...]).astype(h_ref.dtype)
    else:
        (xn_ref,) = rest
        xn_ref[...] = xn


def _resid(cfg, x, o, ga, gate, nxt=None):
    bsz, s, d = x.shape
    ts = min(cfg.row_tile, s)
    row = pl.BlockSpec((None, ts, d), lambda b, i: (b, i, 0))
    vec = pl.BlockSpec((1, d), lambda b, i: (0, 0))
    mod = pl.BlockSpec((None, 1, d), lambda b, i: (b, 0, 0))
    with_h = nxt is not None
    in_specs = [row, row, vec, mod]
    args = [x, o, ga.reshape(1, d), gate]
    out_specs = [row]
    out_shape = [jax.ShapeDtypeStruct((bsz, s, d), F32)]
    if with_h:
        gb, sc, sh = nxt
        in_specs += [vec, mod, mod]
        args += [gb.reshape(1, d), sc, sh]
        out_specs.append(row)
        out_shape.append(jax.ShapeDtypeStruct((bsz, s, d), BF16))
    outs = pl.pallas_call(
        functools.partial(_resid_kernel, with_h=with_h),
        grid=(bsz, s // ts),
        in_specs=in_specs,
        out_specs=out_specs,
        out_shape=out_shape,
        compiler_params=_cparams(cfg, "parallel", "parallel"),
        name="residual_norm",
    )(*args)
    return outs if with_h else outs[0]


def _mm_kernel(*refs, widths, w_is_nk):
    x_refs, (w_ref, o_ref) = refs[:len(widths)], refs[len(widths):]
    acc, off = None, 0
    for x_ref, wd in zip(x_refs, widths):
        if w_is_nk:
            part = _dot_nt(x_ref[...], w_ref[:, off:off + wd].astype(BF16))
        else:
            part = _dot(x_ref[...], w_ref[off:off + wd, :].astype(BF16))
        acc = part if acc is None else acc + part
        off += wd
    o_ref[...] = acc.astype(o_ref.dtype)


def _matmul(cfg, xs, w, layer, n, out_dtype, tm, tn, w_is_nk=False, single_buffer_x=False):
    m = xs[0].shape[0]
    widths = tuple(x.shape[1] for x in xs)
    kdim = sum(widths)
    tm, tn = min(tm, m), min(tn, n)
    assert m % tm == 0 and n % tn == 0 and w.shape[2 if w_is_nk else 1] == kdim
    wspec = (pl.BlockSpec((None, tn, kdim), lambda i, j: (layer, j, 0)) if w_is_nk
             else pl.BlockSpec((None, kdim, tn), lambda i, j: (layer, 0, j)))
    xmode = dict(pipeline_mode=pl.Buffered(1)) if single_buffer_x else {}
    return pl.pallas_call(
        functools.partial(_mm_kernel, widths=widths, w_is_nk=w_is_nk),
        grid=(m // tm, n // tn),
        in_specs=[pl.BlockSpec((tm, wd), lambda i, j: (i, 0), **xmode) for wd in widths] + [wspec],
        out_specs=pl.BlockSpec((tm, tn), lambda i, j: (i, j)),
        out_shape=jax.ShapeDtypeStruct((m, n), out_dtype),
        compiler_params=_cparams(cfg, "parallel", "parallel"),
        name="matmul",
    )(*xs, w)


def _ffn_in_kernel(x_ref, wg_ref, wu_ref, o_ref):
    x = x_ref[...]
    gate = _dot(x, wg_ref[...].astype(BF16))
    up = _dot(x, wu_ref[...].astype(BF16))
    o_ref[...] = (_silu(gate) * up).astype(o_ref.dtype)


def _ffn_in(cfg, x, w, layer):
    m, kdim = x.shape
    hid = w.shape[2] // 2
    tm, tn = min(cfg.ffn_tm, m), min(cfg.ffn_tn, hid)
    nt = hid // tn
    assert m % tm == 0 and hid % tn == 0
    return pl.pallas_call(
        _ffn_in_kernel,
        grid=(m // tm, nt),
        in_specs=[
            pl.BlockSpec((tm, kdim), lambda i, j: (i, 0)),
            pl.BlockSpec((None, kdim, tn), lambda i, j: (layer, 0, j)),
            pl.BlockSpec((None, kdim, tn), lambda i, j: (layer, 0, nt + j)),
        ],
        out_specs=pl.BlockSpec((tm, tn), lambda i, j: (i, j)),
        out_shape=jax.ShapeDtypeStruct((m, hid), BF16),
        compiler_params=_cparams(cfg, "parallel", "parallel"),
        name="ffn_in_swiglu",
    )(x, w, w)


def _moba_kernel(q_ref, k_ref, v_ref, cq_ref, sq_ref, ck_ref, sk_ref, o_ref,
                 kr_ref, vt_ref, km_ref, sel_ref, *, nb, blk, dh, topk, group, hp):
    qi = pl.program_id(2)
    half = dh // 2
    heads = range(hp)
    lanes = [slice(n * dh, (n + 1) * dh) for n in heads]
    each = lambda f, *ls: [f(*xs) for xs in zip(*ls)]

    @pl.when(qi == 0)
    def _():
        for j in range(nb):
            rows = slice(j * blk, (j + 1) * blk)
            for n in heads:
                kb = k_ref[rows, lanes[n]]
                kr = kb * ck_ref[rows, :] + pltpu.roll(kb, half, 1) * sk_ref[rows, :]
                km_ref[n, j:j + 1, :] = jnp.mean(kr, axis=0, keepdims=True)
                kr_ref[n, j] = kr.astype(BF16)
                vt_ref[n, j, 0:dh, :] = v_ref[rows, lanes[n]].T.astype(BF16)
                vt_ref[n, j, dh:dh + MOBA_ONES_ROWS, :] = jnp.ones((MOBA_ONES_ROWS, blk), BF16)

    cq, sq = cq_ref[...], sq_ref[...]
    qr = [(lambda q: q * cq + pltpu.roll(q, half, 1) * sq)(q_ref[:, lanes[n]]) for n in heads]

    gate = [_dot_nt(km_ref[n], qr[n], precision=HIGHEST) for n in heads]
    brow = lax.broadcasted_iota(jnp.int32, (nb, blk), 0)
    gm = each(lambda g: jnp.where(brow < qi, g, -jnp.inf), gate)
    for j in range(nb):
        below = jnp.where(brow < j, 1.0, 0.0)

        def keep_row(g):
            gj = g[j:j + 1, :]
            beats = jnp.where(g > gj, 1.0, 0.0) + jnp.where(g == gj, below, 0.0)
            keep = jnp.where(jnp.sum(beats, axis=0, keepdims=True) < topk, 1.0, 0.0)
            return jnp.broadcast_to(jnp.where(j < qi, keep, 0.0), (V7X_SUBLANES, blk))

        for n, row in enumerate(each(keep_row, gm)):
            sel_ref[n, j] = row

    qs = each(lambda x: (x * (dh ** -0.5 * math.log2(math.e))).T.astype(BF16), qr)
    kpos = lax.broadcasted_iota(jnp.int32, (blk, blk), 0)
    qpos = lax.broadcasted_iota(jnp.int32, (blk, blk), 1)
    colmax = lambda x: jnp.max(x, axis=0, keepdims=True)
    s = [jnp.where(kpos <= qpos, _dot(kr_ref[n, qi], qs[n]), -jnp.inf) for n in heads]
    m = each(colmax, s)
    p = each(lambda x, y: jnp.exp2(x - y), s, m)
    acc = [_dot(vt_ref[n, qi], p[n].astype(BF16)) for n in heads]

    def body(g, carry):
        m, acc = carry
        js = [g * group + u for u in range(group)]
        ss = [[jnp.where(sel_ref[n, j][0:1, :] > 0.0, _dot(kr_ref[n, j], qs[n]), -jnp.inf)
               for n in heads] for j in js]
        m_new = list(m)
        for su in ss:
            m_new = each(lambda x, y: jnp.maximum(x, colmax(y)), m_new, su)
        acc = each(lambda x, y, a: jnp.exp2(x - y) * a, m, m_new, acc)
        for j, su in zip(js, ss):
            p = each(lambda x, y: jnp.exp2(x - y), su, m_new)
            acc = [acc[n] + _dot(vt_ref[n, j], p[n].astype(BF16)) for n in heads]
        return tuple(m_new), tuple(acc)

    m, acc = lax.fori_loop(0, (qi + group - 1) // group, body, (tuple(m), tuple(acc)))
    for n in heads:
        o_ref[:, lanes[n]] = (acc[n][:dh] / acc[n][dh:dh + 1]).T.astype(o_ref.dtype)


def _rope_tables(seq, dim):
    inv = 1.0 / (ROPE_THETA ** (jnp.arange(0, dim, 2, dtype=F32) / dim))
    ang = jnp.arange(seq, dtype=F32)[:, None] * inv[None, :]
    return jnp.cos(ang), jnp.sin(ang)


def _moba(cfg, proj):
    bsz, s, _ = proj.shape
    h, dh, blk = cfg.moba_heads, cfg.moba_head_dim, cfg.moba_block
    assert dh == V7X_LANES and s % blk == 0
    nb = s // blk
    group = math.gcd(cfg.moba_group, nb)
    hp = math.gcd(cfg.moba_heads_per_step, h)
    hg = h // hp
    cos, sin = _rope_tables(s, dh)
    cosf = jnp.concatenate([cos, cos], axis=1)
    sinf = jnp.concatenate([-sin, sin], axis=1)
    qspec = pl.BlockSpec((None, blk, hp * dh), lambda b, hh, i: (b, i, hh))
    kspec = pl.BlockSpec((None, s, hp * dh), lambda b, hh, i: (b, 0, hg + hh))
    vspec = pl.BlockSpec((None, s, hp * dh), lambda b, hh, i: (b, 0, 2 * hg + hh))
    tq = pl.BlockSpec((blk, dh), lambda b, hh, i: (i, 0))
    tk = pl.BlockSpec((s, dh), lambda b, hh, i: (0, 0), pipeline_mode=pl.Buffered(1))
    return pl.pallas_call(
        functools.partial(_moba_kernel, nb=nb, blk=blk, dh=dh, topk=cfg.moba_topk, group=group,
                          hp=hp),
        grid=(bsz, hg, nb),
        in_specs=[qspec, kspec, vspec, tq, tq, tk, tk],
        out_specs=pl.BlockSpec((None, blk, hp * dh), lambda b, hh, i: (b, i, hh)),
        out_shape=jax.ShapeDtypeStruct((bsz, s, h * dh), BF16),
        scratch_shapes=[
            pltpu.VMEM((hp, nb, blk, dh), BF16),
            pltpu.VMEM((hp, nb, dh + MOBA_ONES_ROWS, blk), BF16),
            pltpu.VMEM((hp, nb, dh), F32),
            pltpu.VMEM((hp, nb, V7X_SUBLANES, blk), F32),
        ],
        compiler_params=_cparams(cfg, "parallel", "parallel", "arbitrary"),
        name="moba_attention",
    )(proj, proj, proj, cosf, sinf, cosf, sinf)


def _ret_kernel(q_ref, k_ref, v_ref, g_ref, cos_ref, sin_ref, dm_ref, qd_ref, kd_ref, cd_ref,
                o_ref, st_ref, *, c, nsub, dk, dv, hp):
    @pl.when(pl.program_id(2) == 0)
    def _():
        st_ref[...] = jnp.zeros_like(st_ref)

    half = dk // 2
    heads = range(hp)
    each = lambda f, *ls: [f(*xs) for xs in zip(*ls)]
    bf = lambda x: x.astype(BF16)
    dm, qd, kd, cd = ([ref[n] for n in heads] for ref in (dm_ref, qd_ref, kd_ref, cd_ref))
    for sidx in range(nsub):
        rows = slice(sidx * c, (sidx + 1) * c)
        cos = cos_ref[rows, :]
        sin = sin_ref[rows, :]

        def rope(x):
            x1, x2 = x[:, :half], x[:, half:]
            return jnp.concatenate([x1 * cos - x2 * sin, x2 * cos + x1 * sin], axis=1)

        q = [rope(q_ref[rows, n * dk:(n + 1) * dk]) for n in heads]
        k = [rope(k_ref[rows, n * dk:(n + 1) * dk]) * (dk ** -0.5) for n in heads]
        qb, kb = each(bf, q), each(bf, k)
        vb = [bf(v_ref[rows, n * dv:(n + 1) * dv]) for n in heads]
        st = [st_ref[n] for n in heads]
        inner = each(lambda x, y, d: bf(_dot_nt(x, y) * d), qb, kb, dm)
        cross = each(lambda x, s, d: _dot(x, bf(s)) * d, qb, st, qd)
        o = each(lambda i, v, x: _dot(i, v) + x, inner, vb, cross)
        kdb = each(lambda x, d: bf(x * d), k, kd)
        new_st = each(lambda s, d, x, v: s * d + _dot_tn(x, v), st, cd, kdb, vb)
        for n in heads:
            st_ref[n] = new_st[n]
        on = each(lambda x: x * lax.rsqrt(jnp.mean(x * x, axis=-1, keepdims=True) + NORM_EPS), o)
        for n in heads:
            cols = slice(n * dv, (n + 1) * dv)
            o_ref[rows, cols] = (on[n] * _silu(g_ref[rows, cols])).astype(o_ref.dtype)


def _retention(cfg, proj):
    bsz, s, _ = proj.shape
    h, dk, dv, c = cfg.ret_heads, cfg.ret_key_dim, cfg.ret_val_dim, cfg.ret_chunk
    ts = min(cfg.ret_rows, s)
    assert s % ts == 0 and ts % c == 0
    q0 = 3 * cfg.dm // dk
    k0 = (3 * cfg.dm + cfg.dk) // dk
    v0 = (3 * cfg.dm + 2 * cfg.dk) // dv
    g0 = (3 * cfg.dm + 2 * cfg.dk + cfg.dv) // dv
    assert (3 * cfg.dm) % dk == 0 and (3 * cfg.dm + 2 * cfg.dk) % dv == 0
    cos, sin = _rope_tables(s, dk)
    log_g = jnp.log1p(-jnp.exp2(-5.0 - jnp.arange(h, dtype=F32)))
    idx = jnp.arange(c, dtype=F32)
    diff = idx[:, None] - idx[None, :]
    dmask = jnp.where(diff >= 0, jnp.exp(jnp.maximum(diff, 0.0) * log_g[:, None, None]), 0.0)
    qdec = jnp.exp((idx + 1.0) * log_g[:, None])[..., None]
    kdec = jnp.exp((c - 1.0 - idx) * log_g[:, None])[..., None]
    cdec = jnp.broadcast_to(jnp.exp(c * log_g)[:, None, None], (h, 1, dv))
    hp = math.gcd(cfg.ret_heads_per_step, h)
    assert q0 % hp == 0 and k0 % hp == 0 and v0 % hp == 0 and g0 % hp == 0
    rowspec = lambda w, c0: pl.BlockSpec((None, ts, hp * w), lambda b, hh, i: (b, i, c0 // hp + hh))
    tab = pl.BlockSpec((ts, dk // 2), lambda b, hh, i: (i, 0))
    const = lambda r, w: pl.BlockSpec((hp, r, w), lambda b, hh, i: (hh, 0, 0))
    return pl.pallas_call(
        functools.partial(_ret_kernel, c=c, nsub=ts // c, dk=dk, dv=dv, hp=hp),
        grid=(bsz, h // hp, s // ts),
        in_specs=[
            rowspec(dk, q0), rowspec(dk, k0), rowspec(dv, v0), rowspec(dv, g0), tab, tab,
            const(c, c), const(c, 1), const(c, 1), const(1, dv),
        ],
        out_specs=pl.BlockSpec((None, ts, hp * dv), lambda b, hh, i: (b, i, hh)),
        out_shape=jax.ShapeDtypeStruct((bsz, s, h * dv), BF16),
        scratch_shapes=[pltpu.VMEM((hp, dk, dv), F32)],
        compiler_params=_cparams(cfg, "parallel", "parallel", "arbitrary"),
        name="retention",
    )(proj, proj, proj, proj, cos, sin, dmask, qdec, kdec, cdec)


CONV_HALO = 32


def _conv_kernel(a_ref, g_ref, w_ref, b_ref, o_ref, buf_ref, sh_ref, *, ts, kw, rc):
    @pl.when(pl.program_id(2) == 0)
    def _():
        buf_ref[0:CONV_HALO, :] = jnp.zeros((CONV_HALO, buf_ref.shape[1]), F32)

    buf_ref[CONV_HALO:CONV_HALO + ts, :] = a_ref[...] * jax.nn.sigmoid(g_ref[...])
    nshift = CONV_HALO + ts - V7X_SUBLANES
    for ph in range(1, V7X_SUBLANES):
        sh_ref[ph, 0:nshift, :] = buf_ref[ph:ph + nshift, :]
    off = CONV_HALO - (kw - 1)
    bias = b_ref[...]
    for r0 in range(0, ts, rc):
        acc = jnp.broadcast_to(bias, (rc, bias.shape[1]))
        for j in range(kw):
            ph = (off + j) % V7X_SUBLANES
            base = off + j - ph + r0
            rows = buf_ref[base:base + rc, :] if ph == 0 else sh_ref[ph, base:base + rc, :]
            acc = acc + w_ref[j:j + 1, :] * rows
        o_ref[r0:r0 + rc, :] = acc
    buf_ref[0:CONV_HALO, :] = buf_ref[ts:ts + CONV_HALO, :]


def _conv_glu(cfg, proj, conv_w, conv_b):
    bsz, s, _ = proj.shape
    ch, kw = cfg.conv_ch, cfg.conv_width
    ts, tc = min(cfg.conv_rows, s), min(cfg.conv_cols, ch)
    assert kw - 1 <= CONV_HALO <= ts and s % ts == 0 and ch % tc == 0
    nct = ch // tc
    wp = jnp.zeros((CONV_HALO, ch), F32).at[:kw].set(conv_w)
    return pl.pallas_call(
        functools.partial(_conv_kernel, ts=ts, kw=kw, rc=32),
        grid=(bsz, nct, s // ts),
        in_specs=[
            pl.BlockSpec((None, ts, tc), lambda b, c, i: (b, i, c)),
            pl.BlockSpec((None, ts, tc), lambda b, c, i: (b, i, nct + c)),
            pl.BlockSpec((CONV_HALO, tc), lambda b, c, i: (0, c)),
            pl.BlockSpec((1, tc), lambda b, c, i: (0, c)),
        ],
        out_specs=pl.BlockSpec((None, ts, tc), lambda b, c, i: (b, i, c)),
        out_shape=jax.ShapeDtypeStruct((bsz, s, ch), F32),
        scratch_shapes=[pltpu.VMEM((CONV_HALO + ts, tc), F32),
                        pltpu.VMEM((V7X_SUBLANES, CONV_HALO + ts, tc), F32)],
        compiler_params=_cparams(cfg, "parallel", "parallel", "arbitrary"),
        name="glu_causal_conv",
    )(proj, proj, wp, conv_b.reshape(1, ch))


def _ln_silu_kernel(x_ref, g_ref, b_ref, o_ref):
    x = x_ref[...]
    mu = jnp.mean(x, axis=-1, keepdims=True)
    d = x - mu
    var = jnp.mean(d * d, axis=-1, keepdims=True)
    y = d * lax.rsqrt(var + CONV_LN_EPS) * g_ref[...] + b_ref[...]
    o_ref[...] = _silu(y).astype(o_ref.dtype)


def _ln_silu(cfg, x, g, b):
    bsz, s, d = x.shape
    ts = min(cfg.row_tile, s)
    row = pl.BlockSpec((None, ts, d), lambda bb, i: (bb, i, 0))
    vec = pl.BlockSpec((1, d), lambda bb, i: (0, 0))
    return pl.pallas_call(
        _ln_silu_kernel,
        grid=(bsz, s // ts),
        in_specs=[row, vec, vec],
        out_specs=row,
        out_shape=jax.ShapeDtypeStruct((bsz, s, d), BF16),
        compiler_params=_cparams(cfg, "parallel", "parallel"),
        name="layernorm_swish",
    )(x, g.reshape(1, d), b.reshape(1, d))


def _group_ones(n, group):
    r = lax.broadcasted_iota(jnp.int32, (n, n), 0)
    c = lax.broadcasted_iota(jnp.int32, (n, n), 1)
    shift = int(math.log2(group))
    return jnp.where((r >> shift) == (c >> shift), 1.0, 0.0).astype(BF16)


def _split2(x):
    hi = x.astype(BF16)
    return hi, (x - hi.astype(F32)).astype(BF16)


def _dot_split(x, w):
    xh, xl = _split2(x)
    wh, wl = _split2(w)
    return _dot(xh, wh) + (_dot(xh, wl) + _dot(xl, wh))


def _group_sum(x, gmat):
    n = x.shape[1]
    hi, lo = _split2(x)
    parts = [_dot(hi[:, s0:s0 + V7X_LANES], gmat) + _dot(lo[:, s0:s0 + V7X_LANES], gmat)
             for s0 in range(0, n, V7X_LANES)]
    return parts[0] if len(parts) == 1 else jnp.concatenate(parts, axis=1)


def _rwkv_pre_kernel(r_ref, k_ref, v_ref, lo_ref, mur_ref, muk_ref, muv_ref, mul_ref,
                     w0_ref, wup_ref, a0_ref, aup_ref, gup_ref, kk_ref, ka_ref,
                     ro_ref, lw_ref, ko_ref, vo_ref, ao_ref, bo_ref, go_ref,
                     lr_ref, lk_ref, lv_ref, ll_ref, *, ts, hd, lw_pad):
    first = pl.program_id(1) == 0

    def shift(x_ref, last_ref, mu_ref):
        @pl.when(first)
        def _():
            last_ref[...] = jnp.zeros_like(last_ref)

        x = x_ref[...]
        row = lax.broadcasted_iota(jnp.int32, x.shape, 0)
        prev = jnp.where(row == 0, last_ref[V7X_SUBLANES - 1:V7X_SUBLANES, :], pltpu.roll(x, 1, 0))
        last_ref[...] = x[ts - V7X_SUBLANES:ts, :]
        return x + (prev - x) * mu_ref[...]

    r = shift(r_ref, lr_ref, mur_ref)
    k = shift(k_ref, lk_ref, muk_ref)
    v = shift(v_ref, lv_ref, muv_ref)
    lo = shift(lo_ref, ll_ref, mul_ref)
    xw, xa, xg = lo[:, :lw_pad], lo[:, lw_pad:2 * lw_pad], lo[:, 2 * lw_pad:]

    z = w0_ref[...] + _dot_split(jnp.tanh(xw), wup_ref[...])
    softplus = jnp.maximum(-z, 0.0) + jnp.log(1.0 + jnp.exp(-jnp.abs(z)))
    lw_ref[...] = -jnp.exp(-softplus - 0.5)
    a = jax.nn.sigmoid(a0_ref[...] + _dot_split(xa, aup_ref[...]))
    go_ref[...] = _dot(jax.nn.sigmoid(xg).astype(BF16), gup_ref[...].astype(BF16))

    kkr = k * kk_ref[...]
    ss = _group_sum(kkr * kkr, _group_ones(V7X_LANES, hd))
    kk = kkr * lax.rsqrt(jnp.maximum(ss, 1e-24))
    ro_ref[...] = r
    vo_ref[...] = v
    ko_ref[...] = k * (1.0 + (a - 1.0) * ka_ref[...])
    ao_ref[...] = -kk
    bo_ref[...] = kk * a


def _rwkv_pre(cfg, proj, lora, mu, w0, w_up, a0, a_up, g_up, k_k, k_a):
    bsz, s, _ = proj.shape
    d = cfg.rwkv_dim
    ts = min(cfg.row_tile // 2, s)
    lw_pad = V7X_LANES
    assert cfg.decay_lora <= lw_pad and cfg.iclr_lora <= lw_pad and (2 * cfg.conv_ch) % d == 0
    lo_w = lora.shape[2]
    c0 = 2 * cfg.conv_ch // d
    pad_rows = lambda w: jnp.zeros((lw_pad, d), F32).at[:w.shape[0]].set(w)
    pad_vec = lambda vv, n: jnp.zeros((1, n), F32).at[0, :vv.shape[0]].set(vv)
    mu_r, mu_k, mu_v = (mu[i * d:(i + 1) * d].reshape(1, d) for i in range(3))
    o = 3 * d
    mu_l = jnp.concatenate([
        pad_vec(mu[o:o + cfg.decay_lora], lw_pad),
        pad_vec(mu[o + cfg.decay_lora:o + cfg.decay_lora + cfg.iclr_lora], lw_pad),
        mu[o + cfg.decay_lora + cfg.iclr_lora:].reshape(1, -1)], axis=1)
    row = lambda cb: pl.BlockSpec((None, ts, d), lambda b, i: (b, i, cb))
    lrow = pl.BlockSpec((None, ts, lo_w), lambda b, i: (b, i, 0))
    vec = lambda n: pl.BlockSpec((1, n), lambda b, i: (0, 0))
    mat = lambda rws: pl.BlockSpec((rws, d), lambda b, i: (0, 0))
    orow = pl.BlockSpec((None, ts, d), lambda b, i: (b, i, 0))
    return pl.pallas_call(
        functools.partial(_rwkv_pre_kernel, ts=ts, hd=cfg.rwkv_head_dim, lw_pad=lw_pad),
        grid=(bsz, s // ts),
        in_specs=[row(c0), row(c0 + 1), row(c0 + 2), lrow, vec(d), vec(d), vec(d), vec(lo_w),
                  vec(d), mat(lw_pad), vec(d), mat(lw_pad), mat(cfg.gate_lora), vec(d), vec(d)],
        out_specs=[orow] * 7,
        out_shape=[jax.ShapeDtypeStruct((bsz, s, d), F32)] * 7,
        scratch_shapes=[pltpu.VMEM((V7X_SUBLANES, d), F32)] * 3 + [pltpu.VMEM((V7X_SUBLANES, lo_w), F32)],
        compiler_params=_cparams(cfg, "parallel", "arbitrary"),
        name="rwkv_token_shift_lora",
    )(proj, proj, proj, lora, mu_r, mu_k, mu_v, mu_l, w0.reshape(1, d), pad_rows(w_up),
      a0.reshape(1, d), pad_rows(a_up), g_up, k_k.reshape(1, d), k_a.reshape(1, d))


def _scan_chunk(r, lw, k, v, a, b, st, consts):
    tri, strict_bd, incl_bd, eye, lane_a, bd = consts
    L = r[0].shape[0]
    each = lambda f, *ls: [f(*xs) for xs in zip(*ls)]
    bf = lambda x: x.astype(BF16)
    stack = lambda x: jnp.concatenate([x, x], axis=0)
    unstack = lambda x: jnp.where(lane_a, x[:L], x[L:])
    left, right = (lambda x: x[:, :V7X_LANES]), (lambda x: x[:, V7X_LANES:])

    def split3(x):
        w1 = bf(x)
        e1 = x - w1.astype(F32)
        w2 = bf(e1)
        return jnp.concatenate([w1, w2, bf(e1 - w2.astype(F32))], axis=1)

    cs = each(lambda x: _dot(tri, split3(x)), lw)
    cum = each(lambda x: x[:, :V7X_LANES] + x[:, V7X_LANES:2 * V7X_LANES] + x[:, 2 * V7X_LANES:], cs)
    cl = each(lambda x: x[L - 1:L, :], cum)
    tail = each(lambda x, y: jnp.exp(x - y), cl, cum)
    at = each(lambda x, c, w: x * jnp.exp(c - w), a, cum, lw)
    rt = each(lambda x, c: x * jnp.exp(c), r, cum)
    g_inv = each(lambda c: jnp.exp(-c), cum)
    bt = each(lambda x, g: stack(bf(x * g)), b, g_inv)
    kt = each(lambda x, g: stack(bf(x * g)), k, g_inv)
    lhs = each(lambda x, y: jnp.concatenate(
        [bf(jnp.where(lane_a, x, 0.0)), bf(jnp.where(lane_a, 0.0, x)),
         bf(jnp.where(lane_a, y, 0.0)), bf(jnp.where(lane_a, 0.0, y))], axis=0), at, rt)
    xb = each(_dot_nt, lhs, bt)
    xk = each(_dot_nt, lhs, kt)
    n = each(lambda x: jnp.where(strict_bd, x[:2 * L], 0.0), xb)
    m = each(lambda x: bf(jnp.where(strict_bd, x[:2 * L], 0.0)), xk)
    p = each(lambda x: bf(jnp.where(incl_bd, x[2 * L:], 0.0)), xb)
    q = each(lambda x: bf(jnp.where(incl_bd, x[2 * L:], 0.0)), xk)
    vb = each(bf, v)
    v_st = each(stack, vb)
    mv = each(_dot, m, v_st)
    w = each(lambda x: eye + x, n)
    pw = each(bf, n)
    for _ in range(int(math.log2(L)) - 1):
        pw = each(lambda x: bf(_dot(x, x)), pw)
        w = each(lambda x, y: x + _dot(bf(x), y), w, pw)
    au = each(lambda ww, x, y: _dot(bf(ww), jnp.concatenate([stack(bf(x)), bf(y)], axis=1)),
              w, at, mv)
    pau = each(lambda x, y: _dot(x, bf(y)), p, au)
    qv = each(_dot, q, v_st)
    rbar = each(lambda x, y: bf(x + unstack(left(y))), rt, pau)
    ybar = each(lambda x, y: unstack(right(x) + y), pau, qv)
    bh = each(lambda x, t: stack(bf(x * t)), b, tail)
    kh = each(lambda x, t: bf(x * t), k, tail)
    abar = each(lambda x: bf(jnp.where(bd, left(x), 0.0)), au)
    ubar = each(lambda x: bf(jnp.where(bd, right(x), 0.0)), au)
    tt = each(lambda x, y: bf(jnp.where(bd, _dot_tn(x, y), 0.0)), abar, bh)
    z = each(lambda u, vv, x, y: jnp.where(bd, _dot_tn(jnp.concatenate([u, vv], axis=0),
                                                       jnp.concatenate([x, y], axis=0)), 0.0),
             ubar, vb, bh, kh)
    sb = each(bf, st)
    y = each(lambda x, s, yb: _dot_nt(x, s) + yb, rbar, sb, ybar)
    st_new = each(lambda s, c, s16, t, zz: s * jnp.exp(c) + _dot(s16, t) + zz, st, cl, sb, tt, z)
    return y, st_new


def _scan_kernel(r_ref, lw_ref, k_ref, v_ref, a_ref, b_ref, y_ref, st_ref, *, ts, L, pairs, hd):
    @pl.when(pl.program_id(2) == 0)
    def _():
        st_ref[...] = jnp.zeros_like(st_ref)

    ri = lax.broadcasted_iota(jnp.int32, (L, L), 0)
    ci = lax.broadcasted_iota(jnp.int32, (L, L), 1)
    lane = lax.broadcasted_iota(jnp.int32, (1, V7X_LANES), 1)
    r2 = lax.broadcasted_iota(jnp.int32, (2 * L, 2 * L), 0)
    c2 = lax.broadcasted_iota(jnp.int32, (2 * L, 2 * L), 1)
    bd = (r2 < L) == (c2 < L)
    rl, cl2 = r2 & (L - 1), c2 & (L - 1)
    consts = (jnp.where(ri >= ci, 1.0, 0.0).astype(BF16), bd & (rl > cl2), bd & (rl >= cl2),
              jnp.where(r2 == c2, 1.0, 0.0), lane < hd, bd)

    def body(c, carry):
        rows = pl.ds(pl.multiple_of(c * L, L), L)
        cols = [slice(p * V7X_LANES, (p + 1) * V7X_LANES) for p in range(pairs)]
        load = lambda ref: [ref[rows, cs] for cs in cols]
        ys, sts = _scan_chunk(load(r_ref), load(lw_ref), load(k_ref), load(v_ref), load(a_ref),
                              load(b_ref), [st_ref[p] for p in range(pairs)], consts)
        for p in range(pairs):
            y_ref[rows, cols[p]] = ys[p]
            st_ref[p] = sts[p]
        return carry

    lax.fori_loop(0, ts // L, body, 0)


def _rwkv_scan(cfg, r, lw, k, v, a, b):
    bsz, s, d = r.shape
    hd = cfg.rwkv_head_dim
    assert 2 * hd == V7X_LANES
    ts, L = min(cfg.scan_rows, s), cfg.scan_chunk
    npairs = d // V7X_LANES
    pairs = min(cfg.scan_pairs, npairs)
    assert s % ts == 0 and ts % L == 0 and npairs % pairs == 0
    blk = pl.BlockSpec((None, ts, pairs * V7X_LANES), lambda bb, p, i: (bb, i, p))
    return pl.pallas_call(
        functools.partial(_scan_kernel, ts=ts, L=L, pairs=pairs, hd=hd),
        grid=(bsz, npairs // pairs, s // ts),
        in_specs=[blk] * 6,
        out_specs=blk,
        out_shape=jax.ShapeDtypeStruct((bsz, s, d), F32),
        scratch_shapes=[pltpu.VMEM((pairs, V7X_LANES, V7X_LANES), F32)],
        compiler_params=_cparams(cfg, "parallel", "parallel", "arbitrary"),
        name="rwkv7_scan",
    )(r, lw, k, v, a, b)


def _rwkv_post_kernel(y_ref, r_ref, k_ref, v_ref, g_ref, rk_ref, lg_ref, lb_ref, o_ref, *, hd):
    gmat = _group_ones(V7X_LANES, hd)
    y = y_ref[...]
    mu = _group_sum(y, gmat) * (1.0 / hd)
    d = y - mu
    var = _group_sum(d * d, gmat) * (1.0 / hd)
    yn = d * lax.rsqrt(var + RWKV_LNX_EPS) * lg_ref[...] + lb_ref[...]
    bonus = _group_sum(r_ref[...] * k_ref[...] * rk_ref[...], gmat) * v_ref[...]
    o_ref[...] = ((yn + bonus) * g_ref[...]).astype(o_ref.dtype)


def _rwkv_post(cfg, y, r, k, v, g, r_k, lnx_g, lnx_b):
    bsz, s, d = y.shape
    ts = min(cfg.row_tile, s)
    row = pl.BlockSpec((None, ts, d), lambda b, i: (b, i, 0))
    vec = pl.BlockSpec((1, d), lambda b, i: (0, 0))
    return pl.pallas_call(
        functools.partial(_rwkv_post_kernel, hd=cfg.rwkv_head_dim),
        grid=(bsz, s // ts),
        in_specs=[row] * 5 + [vec] * 3,
        out_specs=row,
        out_shape=jax.ShapeDtypeStruct((bsz, s, d), BF16),
        compiler_params=_cparams(cfg, "parallel", "parallel"),
        name="rwkv_groupnorm_gate",
    )(y, r, k, v, g, r_k.reshape(1, d), lnx_g.reshape(1, d), lnx_b.reshape(1, d))


def _even_mixer(cfg, h, w_in, w_out):
    bsz, s, d = h.shape
    proj = _matmul(cfg, [h.reshape(bsz * s, d)], w_in, 0, cfg.even_in, F32,
                   cfg.mm_tm, cfg.mm_tn, single_buffer_x=True).reshape(bsz, s, cfg.even_in)
    o_m = _moba(cfg, proj).reshape(bsz * s, cfg.dm)
    o_r = _retention(cfg, proj).reshape(bsz * s, cfg.dv)
    return _matmul(cfg, [o_m, o_r], w_out, 0, d, BF16, cfg.mm_tm // 2,
                   cfg.mm_tn // 2).reshape(bsz, s, d)


def _odd_mixer(cfg, h, w_in, w_out, conv_w, conv_b, conv_ln_g, conv_ln_b, mu, w0, w_up, a0, a_up,
               g_up, k_k, k_a, r_k, lnx_g, lnx_b):
    bsz, s, d = h.shape
    h2 = h.reshape(bsz * s, d)
    w_nk = jnp.swapaxes(w_in, 1, 2)
    proj = _matmul(cfg, [h2], w_nk, 0, cfg.odd_main, F32,
                   cfg.mm_tm, cfg.mm_tn, w_is_nk=True, single_buffer_x=True).reshape(bsz, s, cfg.odd_main)
    lw_pad = V7X_LANES
    wl = w_nk[0, cfg.odd_main:]
    zr = lambda n: jnp.zeros((n, d), F32)
    o1, o2 = cfg.decay_lora, cfg.decay_lora + cfg.iclr_lora
    wl = jnp.concatenate([wl[:o1], zr(lw_pad - cfg.decay_lora), wl[o1:o2],
                          zr(lw_pad - cfg.iclr_lora), wl[o2:]], axis=0)[None]
    lo_w = wl.shape[1]
    lora = _matmul(cfg, [h2], wl, 0, lo_w, F32, cfg.mm_tm // 2, lo_w,
                   w_is_nk=True).reshape(bsz, s, lo_w)

    u = _ln_silu(cfg, _conv_glu(cfg, proj, conv_w, conv_b), conv_ln_g, conv_ln_b)
    r, lw, k, v, a, b, g = _rwkv_pre(cfg, proj, lora, mu, w0, w_up, a0, a_up, g_up, k_k, k_a)
    y = _rwkv_scan(cfg, r, lw, k, v, a, b)
    y = _rwkv_post(cfg, y, r, k, v, g, r_k.reshape(-1), lnx_g, lnx_b)
    return _matmul(cfg, [u.reshape(bsz * s, -1), y.reshape(bsz * s, -1)], w_out, 0, d, BF16,
                   cfg.mm_tm, cfg.mm_tn, single_buffer_x=True).reshape(bsz, s, d)


def _forward(cfg, x, c, w_ada, b_ada, norm_g, w_ffn_in, w_ffn_out, even_w_in, even_w_out, odd_w_in,
             odd_w_out, conv_w, conv_b, conv_ln_g, conv_ln_b, rwkv_mu, rwkv_w0, rwkv_w_up, rwkv_a0,
             rwkv_a_up, rwkv_g_up, rwkv_k_k, rwkv_k_a, rwkv_r_k, rwkv_lnx_g, rwkv_lnx_b):
    bsz, s, d = x.shape
    depth = w_ada.shape[0]
    mods = _modulation(cfg, c, w_ada, b_ada)
    sh_m, sc_m = mods[0, :, 0], mods[0, :, 1]
    h = _norm_mod(cfg, x, norm_g[0, 0], sc_m, sh_m)
    for layer in range(depth):
        g_m, sh_f, sc_f, g_f = (mods[layer, :, i] for i in (2, 3, 4, 5))
        j = layer // 2
        if layer % 2 == 0:
            o = _even_mixer(cfg, h, even_w_in[j:j + 1], even_w_out[j:j + 1])
        else:
            o = _odd_mixer(cfg, h, odd_w_in[j:j + 1], odd_w_out[j:j + 1], conv_w[j], conv_b[j],
                           conv_ln_g[j], conv_ln_b[j], rwkv_mu[j], rwkv_w0[j], rwkv_w_up[j],
                           rwkv_a0[j], rwkv_a_up[j], rwkv_g_up[j], rwkv_k_k[j], rwkv_k_a[j],
                           rwkv_r_k[j], rwkv_lnx_g[j], rwkv_lnx_b[j])
        x, h = _resid(cfg, x, o, norm_g[layer, 1], g_m, (norm_g[layer, 2], sc_f, sh_f))
        act = _ffn_in(cfg, h.reshape(bsz * s, d), w_ffn_in, layer)
        f = _matmul(cfg, [act], w_ffn_out, layer, d, BF16, cfg.ffn_out_tm, cfg.ffn_tn,
                    single_buffer_x=True).reshape(bsz, s, d)
        if layer + 1 < depth:
            nxt = (norm_g[layer + 1, 0], mods[layer + 1, :, 1], mods[layer + 1, :, 0])
            x, h = _resid(cfg, x, f, norm_g[layer, 3], g_f, nxt)
        else:
            x = _resid(cfg, x, f, norm_g[layer, 3], g_f)
    return x


def kernel(x, c, w_ada, b_ada, norm_g, w_ffn_in, w_ffn_out, even_w_in, even_w_out, odd_w_in, odd_w_out, conv_w, conv_b, conv_ln_g, conv_ln_b, rwkv_mu, rwkv_w0, rwkv_w_up, rwkv_a0, rwkv_a_up, rwkv_g_up, rwkv_k_k, rwkv_k_a, rwkv_r_k, rwkv_lnx_g, rwkv_lnx_b):
    return _forward(Config(), x, c, w_ada, b_ada, norm_g, w_ffn_in, w_ffn_out, even_w_in, even_w_out,
                    odd_w_in, odd_w_out, conv_w, conv_b, conv_ln_g, conv_ln_b, rwkv_mu, rwkv_w0,
                    rwkv_w_up, rwkv_a0, rwkv_a_up, rwkv_g_up, rwkv_k_k, rwkv_k_a, rwkv_r_k,
                    rwkv_lnx_g, rwkv_lnx_b)
```

```python
import dataclasses
import functools
import math

import jax
import jax.numpy as jnp
from jax import lax
from jax.experimental import pallas as pl
from jax.experimental.pallas import tpu as pltpu

F32 = jnp.float32
BF16 = jnp.bfloat16
HIGHEST = lax.Precision.HIGHEST

V7X_LANES = 128
V7X_SUBLANES = 8
MIB = 1024 * 1024
NORM_EPS = 1e-6
ROPE_THETA = 10000.0
CONV_LN_EPS = 1e-5
MOBA_ONES_ROWS = 16
RWKV_LNX_EPS = 64e-5


@dataclasses.dataclass(frozen=True)
class Config:
    d_model: int = 4096
    moba_heads: int = 16
    moba_head_dim: int = 128
    moba_block: int = 256
    moba_topk: int = 3
    ret_heads: int = 8
    ret_key_dim: int = 256
    ret_val_dim: int = 512
    ret_chunk: int = 128
    conv_ch: int = 2048
    conv_width: int = 31
    rwkv_dim: int = 2048
    rwkv_head_dim: int = 64
    decay_lora: int = 96
    iclr_lora: int = 96
    gate_lora: int = 256
    ffn_hidden: int = 11008
    row_tile: int = 256
    mm_tm: int = 2048
    ffn_out_tm: int = 1024
    mm_tn: int = 512
    ffn_tm: int = 2048
    ffn_tn: int = 256
    ret_rows: int = 512
    ret_heads_per_step: int = 2
    conv_rows: int = 256
    conv_cols: int = 256
    scan_rows: int = 256
    scan_chunk: int = 64
    scan_pairs: int = 16
    moba_group: int = 4
    moba_heads_per_step: int = 4
    vmem_mib: int = 56

    @property
    def dm(self):
        return self.moba_heads * self.moba_head_dim

    @property
    def dk(self):
        return self.ret_heads * self.ret_key_dim

    @property
    def dv(self):
        return self.ret_heads * self.ret_val_dim

    @property
    def even_in(self):
        return 3 * self.dm + 2 * self.dk + 2 * self.dv

    @property
    def lora_in(self):
        return self.decay_lora + self.iclr_lora + self.gate_lora

    @property
    def odd_main(self):
        return 2 * self.conv_ch + 3 * self.rwkv_dim


def _cparams(cfg, *sem):
    return pltpu.CompilerParams(dimension_semantics=sem, vmem_limit_bytes=cfg.vmem_mib * MIB)


def _silu(x):
    return x * jax.nn.sigmoid(x)


def _dot(a, b, **kw):
    return jnp.dot(a, b, preferred_element_type=F32, **kw)


def _dot_nt(a, b, **kw):
    return lax.dot_general(a, b, (((1,), (1,)), ((), ())), preferred_element_type=F32, **kw)


def _dot_tn(a, b, **kw):
    return lax.dot_general(a, b, (((0,), (0,)), ((), ())), preferred_element_type=F32, **kw)


def _ada_kernel(c_ref, w_ref, b_ref, o_ref):
    s = _silu(c_ref[...])
    hi = s.astype(BF16).astype(F32)
    parts = _dot(jnp.concatenate([hi, s - hi], axis=0).astype(BF16), w_ref[...].astype(BF16))
    o_ref[...] = parts[:V7X_SUBLANES] + parts[V7X_SUBLANES:] + b_ref[...]


def _modulation(cfg, c, w_ada, b_ada, tn=512):
    depth, d, n = w_ada.shape
    bsz = c.shape[0]
    cp = jnp.zeros((V7X_SUBLANES, d), F32).at[:bsz].set(c)
    out = pl.pallas_call(
        _ada_kernel,
        grid=(depth, n // tn),
        in_specs=[
            pl.BlockSpec((V7X_SUBLANES, d), lambda l, j: (0, 0)),
            pl.BlockSpec((None, d, tn), lambda l, j: (l, 0, j)),
            pl.BlockSpec((None, 1, tn), lambda l, j: (l, 0, j)),
        ],
        out_specs=pl.BlockSpec((None, V7X_SUBLANES, tn), lambda l, j: (l, 0, j)),
        out_shape=jax.ShapeDtypeStruct((depth, V7X_SUBLANES, n), F32),
        compiler_params=_cparams(cfg, "parallel", "parallel"),
        name="adaln_modulation",
    )(cp, w_ada, b_ada.reshape(depth, 1, n))
    return out[:, :bsz].reshape(depth, bsz, 6, 1, d)


def _rms(x, g):
    return x * lax.rsqrt(jnp.mean(x * x, axis=-1, keepdims=True) + NORM_EPS) * g


def _norm_mod_kernel(x_ref, g_ref, sc_ref, sh_ref, o_ref):
    y = _rms(x_ref[...], g_ref[...])
    o_ref[...] = (y * (1.0 + sc_ref[...]) + sh_ref[...]).astype(o_ref.dtype)


def _norm_mod(cfg, x, g, sc, sh):
    bsz, s, d = x.shape
    ts = min(cfg.row_tile, s)
    row = pl.BlockSpec((None, ts, d), lambda b, i: (b, i, 0))
    vec = pl.BlockSpec((1, d), lambda b, i: (0, 0))
    mod = pl.BlockSpec((None, 1, d), lambda b, i: (b, 0, 0))
    return pl.pallas_call(
        _norm_mod_kernel,
        grid=(bsz, s // ts),
        in_specs=[row, vec, mod, mod],
        out_specs=row,
        out_shape=jax.ShapeDtypeStruct((bsz, s, d), BF16),
        compiler_params=_cparams(cfg, "parallel", "parallel"),
        name="norm_modulate",
    )(x, g.reshape(1, d), sc, sh)


def _resid_kernel(x_ref, o_ref, ga_ref, gate_ref, *rest, with_h):
    xn = x_ref[...] + gate_ref[...] * _rms(o_ref[...].astype(F32), ga_ref[...])
    if with_h:
        gb_ref, sc_ref, sh_ref, xn_ref, h_ref = rest
        xn_ref[...] = xn
        h_ref[...] = (_rms(xn, gb_ref[...]) * (1.0 + sc_ref[...]) + sh_ref[...]).astype(h_ref.dtype)
    else:
        (xn_ref,) = rest
        xn_ref[...] = xn


def _resid(cfg, x, o, ga, gate, nxt=None):
    bsz, s, d = x.shape
    ts = min(cfg.row_tile, s)
    row = pl.BlockSpec((None, ts, d), lambda b, i: (b, i, 0))
    vec = pl.BlockSpec((1, d), lambda b, i: (0, 0))
    mod = pl.BlockSpec((None, 1, d), lambda b, i: (b, 0, 0))
    with_h = nxt is not None
    in_specs = [row, row, vec, mod]
    args = [x, o, ga.reshape(1, d), gate]
    out_specs = [row]
    out_shape = [jax.ShapeDtypeStruct((bsz, s, d), F32)]
    if with_h:
        gb, sc, sh = nxt
        in_specs += [vec, mod, mod]
        args += [gb.reshape(1, d), sc, sh]
        out_specs.append(row)
        out_shape.append(jax.ShapeDtypeStruct((bsz, s, d), BF16))
    outs = pl.pallas_call(
        functools.partial(_resid_kernel, with_h=with_h),
        grid=(bsz, s // ts),
        in_specs=in_specs,
        out_specs=out_specs,
        out_shape=out_shape,
        compiler_params=_cparams(cfg, "parallel", "parallel"),
        name="residual_norm",
    )(*args)
    return outs if with_h else outs[0]


def _mm_kernel(*refs, widths, w_is_nk):
    x_refs, (w_ref, o_ref) = refs[:len(widths)], refs[len(widths):]
    acc, off = None, 0
    for x_ref, wd in zip(x_refs, widths):
        if w_is_nk:
            part = _dot_nt(x_ref[...], w_ref[:, off:off + wd].astype(BF16))
        else:
            part = _dot(x_ref[...], w_ref[off:off + wd, :].astype(BF16))
        acc = part if acc is None else acc + part
        off += wd
    o_ref[...] = acc.astype(o_ref.dtype)


def _matmul(cfg, xs, w, layer, n, out_dtype, tm, tn, w_is_nk=False, single_buffer_x=False):
    m = xs[0].shape[0]
    widths = tuple(x.shape[1] for x in xs)
    kdim = sum(widths)
    tm, tn = min(tm, m), min(tn, n)
    assert m % tm == 0 and n % tn == 0 and w.shape[2 if w_is_nk else 1] == kdim
    wspec = (pl.BlockSpec((None, tn, kdim), lambda i, j: (layer, j, 0)) if w_is_nk
             else pl.BlockSpec((None, kdim, tn), lambda i, j: (layer, 0, j)))
    xmode = dict(pipeline_mode=pl.Buffered(1)) if single_buffer_x else {}
    return pl.pallas_call(
        functools.partial(_mm_kernel, widths=widths, w_is_nk=w_is_nk),
        grid=(m // tm, n // tn),
        in_specs=[pl.BlockSpec((tm, wd), lambda i, j: (i, 0), **xmode) for wd in widths] + [wspec],
        out_specs=pl.BlockSpec((tm, tn), lambda i, j: (i, j)),
        out_shape=jax.ShapeDtypeStruct((m, n), out_dtype),
        compiler_params=_cparams(cfg, "parallel", "parallel"),
        name="matmul",
    )(*xs, w)


def _ffn_in_kernel(x_ref, wg_ref, wu_ref, o_ref):
    x = x_ref[...]
    gate = _dot(x, wg_ref[...].astype(BF16))
    up = _dot(x, wu_ref[...].astype(BF16))
    o_ref[...] = (_silu(gate) * up).astype(o_ref.dtype)


def _ffn_in(cfg, x, w, layer):
    m, kdim = x.shape
    hid = w.shape[2] // 2
    tm, tn = min(cfg.ffn_tm, m), min(cfg.ffn_tn, hid)
    nt = hid // tn
    assert m % tm == 0 and hid % tn == 0
    return pl.pallas_call(
        _ffn_in_kernel,
        grid=(m // tm, nt),
        in_specs=[
            pl.BlockSpec((tm, kdim), lambda i, j: (i, 0)),
            pl.BlockSpec((None, kdim, tn), lambda i, j: (layer, 0, j)),
            pl.BlockSpec((None, kdim, tn), lambda i, j: (layer, 0, nt + j)),
        ],
        out_specs=pl.BlockSpec((tm, tn), lambda i, j: (i, j)),
        out_shape=jax.ShapeDtypeStruct((m, hid), BF16),
        compiler_params=_cparams(cfg, "parallel", "parallel"),
        name="ffn_in_swiglu",
    )(x, w, w)


def _moba_kernel(q_ref, k_ref, v_ref, cq_ref, sq_ref, ck_ref, sk_ref, o_ref,
                 kr_ref, vt_ref, km_ref, sel_ref, *, nb, blk, dh, topk, group, hp):
    qi = pl.program_id(2)
    half = dh // 2
    heads = range(hp)
    lanes = [slice(n * dh, (n + 1) * dh) for n in heads]
    each = lambda f, *ls: [f(*xs) for xs in zip(*ls)]

    @pl.when(qi == 0)
    def _():
        for j in range(nb):
            rows = slice(j * blk, (j + 1) * blk)
            for n in heads:
                kb = k_ref[rows, lanes[n]]
                kr = kb * ck_ref[rows, :] + pltpu.roll(kb, half, 1) * sk_ref[rows, :]
                km_ref[n, j:j + 1, :] = jnp.mean(kr, axis=0, keepdims=True)
                kr_ref[n, j] = kr.astype(BF16)
                vt_ref[n, j, 0:dh, :] = v_ref[rows, lanes[n]].T.astype(BF16)
                vt_ref[n, j, dh:dh + MOBA_ONES_ROWS, :] = jnp.ones((MOBA_ONES_ROWS, blk), BF16)

    cq, sq = cq_ref[...], sq_ref[...]
    qr = [(lambda q: q * cq + pltpu.roll(q, half, 1) * sq)(q_ref[:, lanes[n]]) for n in heads]

    gate = [_dot_nt(km_ref[n], qr[n], precision=HIGHEST) for n in heads]
    brow = lax.broadcasted_iota(jnp.int32, (nb, blk), 0)
    gm = each(lambda g: jnp.where(brow < qi, g, -jnp.inf), gate)
    for j in range(nb):
        below = jnp.where(brow < j, 1.0, 0.0)

        def keep_row(g):
            gj = g[j:j + 1, :]
            beats = jnp.where(g > gj, 1.0, 0.0) + jnp.where(g == gj, below, 0.0)
            keep = jnp.where(jnp.sum(beats, axis=0, keepdims=True) < topk, 1.0, 0.0)
            return jnp.broadcast_to(jnp.where(j < qi, keep, 0.0), (V7X_SUBLANES, blk))

        for n, row in enumerate(each(keep_row, gm)):
            sel_ref[n, j] = row

    qs = each(lambda x: (x * (dh ** -0.5 * math.log2(math.e))).T.astype(BF16), qr)
    kpos = lax.broadcasted_iota(jnp.int32, (blk, blk), 0)
    qpos = lax.broadcasted_iota(jnp.int32, (blk, blk), 1)
    colmax = lambda x: jnp.max(x, axis=0, keepdims=True)
    s = [jnp.where(kpos <= qpos, _dot(kr_ref[n, qi], qs[n]), -jnp.inf) for n in heads]
    m = each(colmax, s)
    p = each(lambda x, y: jnp.exp2(x - y), s, m)
    acc = [_dot(vt_ref[n, qi], p[n].astype(BF16)) for n in heads]

    def past_blocks(width, first):
        def body(g, carry):
            m, acc = carry
            js = [first + g * width + u for u in range(width)]
            ss = [[jnp.where(sel_ref[n, j][0:1, :] > 0.0, _dot(kr_ref[n, j], qs[n]), -jnp.inf)
                   for n in heads] for j in js]
            m_new = list(m)
            for su in ss:
                m_new = each(lambda x, y: jnp.maximum(x, colmax(y)), m_new, su)
            acc = each(lambda x, y, a: jnp.exp2(x - y) * a, m, m_new, acc)
            for j, su in zip(js, ss):
                p = each(lambda x, y: jnp.exp2(x - y), su, m_new)
                acc = [acc[n] + _dot(vt_ref[n, j], p[n].astype(BF16)) for n in heads]
            return tuple(m_new), tuple(acc)
        return body

    whole = qi // group
    carry = lax.fori_loop(0, whole, past_blocks(group, 0), (tuple(m), tuple(acc)))
    m, acc = lax.fori_loop(0, qi - whole * group, past_blocks(1, whole * group), carry)
    for n in heads:
        o_ref[:, lanes[n]] = (acc[n][:dh] / acc[n][dh:dh + 1]).T.astype(o_ref.dtype)


def _rope_tables(seq, dim):
    inv = 1.0 / (ROPE_THETA ** (jnp.arange(0, dim, 2, dtype=F32) / dim))
    ang = jnp.arange(seq, dtype=F32)[:, None] * inv[None, :]
    return jnp.cos(ang), jnp.sin(ang)


def _moba(cfg, proj):
    bsz, s, _ = proj.shape
    h, dh, blk = cfg.moba_heads, cfg.moba_head_dim, cfg.moba_block
    assert dh == V7X_LANES and s % blk == 0
    nb = s // blk
    group = math.gcd(cfg.moba_group, nb)
    hp = math.gcd(cfg.moba_heads_per_step, h)
    hg = h // hp
    cos, sin = _rope_tables(s, dh)
    cosf = jnp.concatenate([cos, cos], axis=1)
    sinf = jnp.concatenate([-sin, sin], axis=1)
    qspec = pl.BlockSpec((None, blk, hp * dh), lambda b, hh, i: (b, i, hh))
    kspec = pl.BlockSpec((None, s, hp * dh), lambda b, hh, i: (b, 0, hg + hh))
    vspec = pl.BlockSpec((None, s, hp * dh), lambda b, hh, i: (b, 0, 2 * hg + hh))
    tq = pl.BlockSpec((blk, dh), lambda b, hh, i: (i, 0))
    tk = pl.BlockSpec((s, dh), lambda b, hh, i: (0, 0), pipeline_mode=pl.Buffered(1))
    return pl.pallas_call(
        functools.partial(_moba_kernel, nb=nb, blk=blk, dh=dh, topk=cfg.moba_topk, group=group,
                          hp=hp),
        grid=(bsz, hg, nb),
        in_specs=[qspec, kspec, vspec, tq, tq, tk, tk],
        out_specs=pl.BlockSpec((None, blk, hp * dh), lambda b, hh, i: (b, i, hh)),
        out_shape=jax.ShapeDtypeStruct((bsz, s, h * dh), BF16),
        scratch_shapes=[
            pltpu.VMEM((hp, nb, blk, dh), BF16),
            pltpu.VMEM((hp, nb, dh + MOBA_ONES_ROWS, blk), BF16),
            pltpu.VMEM((hp, nb, dh), F32),
            pltpu.VMEM((hp, nb, V7X_SUBLANES, blk), F32),
        ],
        compiler_params=_cparams(cfg, "parallel", "parallel", "arbitrary"),
        name="moba_attention",
    )(proj, proj, proj, cosf, sinf, cosf, sinf)


def _ret_kernel(q_ref, k_ref, v_ref, g_ref, cos_ref, sin_ref, dm_ref, qd_ref, kd_ref, cd_ref,
                o_ref, st_ref, *, c, nsub, dk, dv, hp):
    @pl.when(pl.program_id(2) == 0)
    def _():
        st_ref[...] = jnp.zeros_like(st_ref)

    half = dk // 2
    heads = range(hp)
    each = lambda f, *ls: [f(*xs) for xs in zip(*ls)]
    bf = lambda x: x.astype(BF16)
    dm, qd, kd, cd = ([ref[n] for n in heads] for ref in (dm_ref, qd_ref, kd_ref, cd_ref))
    for sidx in range(nsub):
        rows = slice(sidx * c, (sidx + 1) * c)
        cos = cos_ref[rows, :]
        sin = sin_ref[rows, :]

        def rope(x):
            x1, x2 = x[:, :half], x[:, half:]
            return jnp.concatenate([x1 * cos - x2 * sin, x2 * cos + x1 * sin], axis=1)

        q = [rope(q_ref[rows, n * dk:(n + 1) * dk]) for n in heads]
        k = [rope(k_ref[rows, n * dk:(n + 1) * dk]) * (dk ** -0.5) for n in heads]
        qb, kb = each(bf, q), each(bf, k)
        vb = [bf(v_ref[rows, n * dv:(n + 1) * dv]) for n in heads]
        st = [st_ref[n] for n in heads]
        inner = each(lambda x, y, d: bf(_dot_nt(x, y) * d), qb, kb, dm)
        cross = each(lambda x, s, d: _dot(x, bf(s)) * d, qb, st, qd)
        o = each(lambda i, v, x: _dot(i, v) + x, inner, vb, cross)
        kdb = each(lambda x, d: bf(x * d), k, kd)
        new_st = each(lambda s, d, x, v: s * d + _dot_tn(x, v), st, cd, kdb, vb)
        for n in heads:
            st_ref[n] = new_st[n]
        on = each(lambda x: x * lax.rsqrt(jnp.mean(x * x, axis=-1, keepdims=True) + NORM_EPS), o)
        for n in heads:
            cols = slice(n * dv, (n + 1) * dv)
            o_ref[rows, cols] = (on[n] * _silu(g_ref[rows, cols])).astype(o_ref.dtype)


def _retention(cfg, proj):
    bsz, s, _ = proj.shape
    h, dk, dv, c = cfg.ret_heads, cfg.ret_key_dim, cfg.ret_val_dim, cfg.ret_chunk
    ts = min(cfg.ret_rows, s)
    assert s % ts == 0 and ts % c == 0
    q0 = 3 * cfg.dm // dk
    k0 = (3 * cfg.dm + cfg.dk) // dk
    v0 = (3 * cfg.dm + 2 * cfg.dk) // dv
    g0 = (3 * cfg.dm + 2 * cfg.dk + cfg.dv) // dv
    assert (3 * cfg.dm) % dk == 0 and (3 * cfg.dm + 2 * cfg.dk) % dv == 0
    cos, sin = _rope_tables(s, dk)
    log_g = jnp.log1p(-jnp.exp2(-5.0 - jnp.arange(h, dtype=F32)))
    idx = jnp.arange(c, dtype=F32)
    diff = idx[:, None] - idx[None, :]
    dmask = jnp.where(diff >= 0, jnp.exp(jnp.maximum(diff, 0.0) * log_g[:, None, None]), 0.0)
    qdec = jnp.exp((idx + 1.0) * log_g[:, None])[..., None]
    kdec = jnp.exp((c - 1.0 - idx) * log_g[:, None])[..., None]
    cdec = jnp.broadcast_to(jnp.exp(c * log_g)[:, None, None], (h, 1, dv))
    hp = math.gcd(cfg.ret_heads_per_step, h)
    assert q0 % hp == 0 and k0 % hp == 0 and v0 % hp == 0 and g0 % hp == 0
    rowspec = lambda w, c0: pl.BlockSpec((None, ts, hp * w), lambda b, hh, i: (b, i, c0 // hp + hh))
    tab = pl.BlockSpec((ts, dk // 2), lambda b, hh, i: (i, 0))
    const = lambda r, w: pl.BlockSpec((hp, r, w), lambda b, hh, i: (hh, 0, 0))
    return pl.pallas_call(
        functools.partial(_ret_kernel, c=c, nsub=ts // c, dk=dk, dv=dv, hp=hp),
        grid=(bsz, h // hp, s // ts),
        in_specs=[
            rowspec(dk, q0), rowspec(dk, k0), rowspec(dv, v0), rowspec(dv, g0), tab, tab,
            const(c, c), const(c, 1), const(c, 1), const(1, dv),
        ],
        out_specs=pl.BlockSpec((None, ts, hp * dv), lambda b, hh, i: (b, i, hh)),
        out_shape=jax.ShapeDtypeStruct((bsz, s, h * dv), BF16),
        scratch_shapes=[pltpu.VMEM((hp, dk, dv), F32)],
        compiler_params=_cparams(cfg, "parallel", "parallel", "arbitrary"),
        name="retention",
    )(proj, proj, proj, proj, cos, sin, dmask, qdec, kdec, cdec)


CONV_HALO = 32


def _conv_kernel(a_ref, g_ref, w_ref, b_ref, o_ref, buf_ref, sh_ref, *, ts, kw, rc):
    @pl.when(pl.program_id(2) == 0)
    def _():
        buf_ref[0:CONV_HALO, :] = jnp.zeros((CONV_HALO, buf_ref.shape[1]), F32)

    buf_ref[CONV_HALO:CONV_HALO + ts, :] = a_ref[...] * jax.nn.sigmoid(g_ref[...])
    nshift = CONV_HALO + ts - V7X_SUBLANES
    for ph in range(1, V7X_SUBLANES):
        sh_ref[ph, 0:nshift, :] = buf_ref[ph:ph + nshift, :]
    off = CONV_HALO - (kw - 1)
    bias = b_ref[...]
    for r0 in range(0, ts, rc):
        acc = jnp.broadcast_to(bias, (rc, bias.shape[1]))
        for j in range(kw):
            ph = (off + j) % V7X_SUBLANES
            base = off + j - ph + r0
            rows = buf_ref[base:base + rc, :] if ph == 0 else sh_ref[ph, base:base + rc, :]
            acc = acc + w_ref[j:j + 1, :] * rows
        o_ref[r0:r0 + rc, :] = acc
    buf_ref[0:CONV_HALO, :] = buf_ref[ts:ts + CONV_HALO, :]


def _conv_glu(cfg, proj, conv_w, conv_b):
    bsz, s, _ = proj.shape
    ch, kw = cfg.conv_ch, cfg.conv_width
    ts, tc = min(cfg.conv_rows, s), min(cfg.conv_cols, ch)
    assert kw - 1 <= CONV_HALO <= ts and s % ts == 0 and ch % tc == 0
    nct = ch // tc
    wp = jnp.zeros((CONV_HALO, ch), F32).at[:kw].set(conv_w)
    return pl.pallas_call(
        functools.partial(_conv_kernel, ts=ts, kw=kw, rc=32),
        grid=(bsz, nct, s // ts),
        in_specs=[
            pl.BlockSpec((None, ts, tc), lambda b, c, i: (b, i, c)),
            pl.BlockSpec((None, ts, tc), lambda b, c, i: (b, i, nct + c)),
            pl.BlockSpec((CONV_HALO, tc), lambda b, c, i: (0, c)),
            pl.BlockSpec((1, tc), lambda b, c, i: (0, c)),
        ],
        out_specs=pl.BlockSpec((None, ts, tc), lambda b, c, i: (b, i, c)),
        out_shape=jax.ShapeDtypeStruct((bsz, s, ch), F32),
        scratch_shapes=[pltpu.VMEM((CONV_HALO + ts, tc), F32),
                        pltpu.VMEM((V7X_SUBLANES, CONV_HALO + ts, tc), F32)],
        compiler_params=_cparams(cfg, "parallel", "parallel", "arbitrary"),
        name="glu_causal_conv",
    )(proj, proj, wp, conv_b.reshape(1, ch))


def _ln_silu_kernel(x_ref, g_ref, b_ref, o_ref):
    x = x_ref[...]
    mu = jnp.mean(x, axis=-1, keepdims=True)
    d = x - mu
    var = jnp.mean(d * d, axis=-1, keepdims=True)
    y = d * lax.rsqrt(var + CONV_LN_EPS) * g_ref[...] + b_ref[...]
    o_ref[...] = _silu(y).astype(o_ref.dtype)


def _ln_silu(cfg, x, g, b):
    bsz, s, d = x.shape
    ts = min(cfg.row_tile, s)
    row = pl.BlockSpec((None, ts, d), lambda bb, i: (bb, i, 0))
    vec = pl.BlockSpec((1, d), lambda bb, i: (0, 0))
    return pl.pallas_call(
        _ln_silu_kernel,
        grid=(bsz, s // ts),
        in_specs=[row, vec, vec],
        out_specs=row,
        out_shape=jax.ShapeDtypeStruct((bsz, s, d), BF16),
        compiler_params=_cparams(cfg, "parallel", "parallel"),
        name="layernorm_swish",
    )(x, g.reshape(1, d), b.reshape(1, d))


def _group_ones(n, group):
    r = lax.broadcasted_iota(jnp.int32, (n, n), 0)
    c = lax.broadcasted_iota(jnp.int32, (n, n), 1)
    shift = int(math.log2(group))
    return jnp.where((r >> shift) == (c >> shift), 1.0, 0.0).astype(BF16)


def _split2(x):
    hi = x.astype(BF16)
    return hi, (x - hi.astype(F32)).astype(BF16)


def _dot_split(x, w):
    xh, xl = _split2(x)
    wh, wl = _split2(w)
    return _dot(xh, wh) + (_dot(xh, wl) + _dot(xl, wh))


def _group_sum(x, gmat):
    n = x.shape[1]
    hi, lo = _split2(x)
    parts = [_dot(hi[:, s0:s0 + V7X_LANES], gmat) + _dot(lo[:, s0:s0 + V7X_LANES], gmat)
             for s0 in range(0, n, V7X_LANES)]
    return parts[0] if len(parts) == 1 else jnp.concatenate(parts, axis=1)


def _rwkv_pre_kernel(r_ref, k_ref, v_ref, lo_ref, mur_ref, muk_ref, muv_ref, mul_ref,
                     w0_ref, wup_ref, a0_ref, aup_ref, gup_ref, kk_ref, ka_ref,
                     ro_ref, lw_ref, ko_ref, vo_ref, ao_ref, bo_ref, go_ref,
                     lr_ref, lk_ref, lv_ref, ll_ref, *, ts, hd, lw_pad):
    first = pl.program_id(1) == 0

    def shift(x_ref, last_ref, mu_ref):
        @pl.when(first)
        def _():
            last_ref[...] = jnp.zeros_like(last_ref)

        x = x_ref[...]
        row = lax.broadcasted_iota(jnp.int32, x.shape, 0)
        prev = jnp.where(row == 0, last_ref[V7X_SUBLANES - 1:V7X_SUBLANES, :], pltpu.roll(x, 1, 0))
        last_ref[...] = x[ts - V7X_SUBLANES:ts, :]
        return x + (prev - x) * mu_ref[...]

    r = shift(r_ref, lr_ref, mur_ref)
    k = shift(k_ref, lk_ref, muk_ref)
    v = shift(v_ref, lv_ref, muv_ref)
    lo = shift(lo_ref, ll_ref, mul_ref)
    xw, xa, xg = lo[:, :lw_pad], lo[:, lw_pad:2 * lw_pad], lo[:, 2 * lw_pad:]

    z = w0_ref[...] + _dot_split(jnp.tanh(xw), wup_ref[...])
    softplus = jnp.maximum(-z, 0.0) + jnp.log(1.0 + jnp.exp(-jnp.abs(z)))
    lw_ref[...] = -jnp.exp(-softplus - 0.5)
    a = jax.nn.sigmoid(a0_ref[...] + _dot_split(xa, aup_ref[...]))
    go_ref[...] = _dot(jax.nn.sigmoid(xg).astype(BF16), gup_ref[...].astype(BF16))

    kkr = k * kk_ref[...]
    ss = _group_sum(kkr * kkr, _group_ones(V7X_LANES, hd))
    kk = kkr * lax.rsqrt(jnp.maximum(ss, 1e-24))
    ro_ref[...] = r
    vo_ref[...] = v
    ko_ref[...] = k * (1.0 + (a - 1.0) * ka_ref[...])
    ao_ref[...] = -kk
    bo_ref[...] = kk * a


def _rwkv_pre(cfg, proj, lora, mu, w0, w_up, a0, a_up, g_up, k_k, k_a):
    bsz, s, _ = proj.shape
    d = cfg.rwkv_dim
    ts = min(cfg.row_tile // 2, s)
    lw_pad = V7X_LANES
    assert cfg.decay_lora <= lw_pad and cfg.iclr_lora <= lw_pad and (2 * cfg.conv_ch) % d == 0
    lo_w = lora.shape[2]
    c0 = 2 * cfg.conv_ch // d
    pad_rows = lambda w: jnp.zeros((lw_pad, d), F32).at[:w.shape[0]].set(w)
    pad_vec = lambda vv, n: jnp.zeros((1, n), F32).at[0, :vv.shape[0]].set(vv)
    mu_r, mu_k, mu_v = (mu[i * d:(i + 1) * d].reshape(1, d) for i in range(3))
    o = 3 * d
    mu_l = jnp.concatenate([
        pad_vec(mu[o:o + cfg.decay_lora], lw_pad),
        pad_vec(mu[o + cfg.decay_lora:o + cfg.decay_lora + cfg.iclr_lora], lw_pad),
        mu[o + cfg.decay_lora + cfg.iclr_lora:].reshape(1, -1)], axis=1)
    row = lambda cb: pl.BlockSpec((None, ts, d), lambda b, i: (b, i, cb))
    lrow = pl.BlockSpec((None, ts, lo_w), lambda b, i: (b, i, 0))
    vec = lambda n: pl.BlockSpec((1, n), lambda b, i: (0, 0))
    mat = lambda rws: pl.BlockSpec((rws, d), lambda b, i: (0, 0))
    orow = pl.BlockSpec((None, ts, d), lambda b, i: (b, i, 0))
    return pl.pallas_call(
        functools.partial(_rwkv_pre_kernel, ts=ts, hd=cfg.rwkv_head_dim, lw_pad=lw_pad),
        grid=(bsz, s // ts),
        in_specs=[row(c0), row(c0 + 1), row(c0 + 2), lrow, vec(d), vec(d), vec(d), vec(lo_w),
                  vec(d), mat(lw_pad), vec(d), mat(lw_pad), mat(cfg.gate_lora), vec(d), vec(d)],
        out_specs=[orow] * 7,
        out_shape=[jax.ShapeDtypeStruct((bsz, s, d), F32)] * 7,
        scratch_shapes=[pltpu.VMEM((V7X_SUBLANES, d), F32)] * 3 + [pltpu.VMEM((V7X_SUBLANES, lo_w), F32)],
        compiler_params=_cparams(cfg, "parallel", "arbitrary"),
        name="rwkv_token_shift_lora",
    )(proj, proj, proj, lora, mu_r, mu_k, mu_v, mu_l, w0.reshape(1, d), pad_rows(w_up),
      a0.reshape(1, d), pad_rows(a_up), g_up, k_k.reshape(1, d), k_a.reshape(1, d))


def _scan_chunk(r, lw, k, v, a, b, st, consts):
    tri, strict_bd, incl_bd, eye, lane_a, bd = consts
    L = r[0].shape[0]
    each = lambda f, *ls: [f(*xs) for xs in zip(*ls)]
    bf = lambda x: x.astype(BF16)
    stack = lambda x: jnp.concatenate([x, x], axis=0)
    unstack = lambda x: jnp.where(lane_a, x[:L], x[L:])
    left, right = (lambda x: x[:, :V7X_LANES]), (lambda x: x[:, V7X_LANES:])

    cs = each(lambda x: _dot(tri, jnp.concatenate(_split2(x), axis=1)), lw)
    cum = each(lambda x: left(x) + right(x), cs)
    cl = each(lambda x: x[L - 1:L, :], cum)
    tail = each(lambda x, y: jnp.exp(x - y), cl, cum)
    at = each(lambda x, c, w: x * jnp.exp(c - w), a, cum, lw)
    rt = each(lambda x, c: x * jnp.exp(c), r, cum)
    g_inv = each(lambda c: jnp.exp(-c), cum)
    bk = each(lambda x, y, g: jnp.concatenate([stack(bf(x * g)), stack(bf(y * g))], axis=0), b, k, g_inv)
    heads2 = lambda x: jnp.concatenate([bf(jnp.where(lane_a, x, 0.0)), bf(jnp.where(lane_a, 0.0, x))],
                                       axis=0)
    xa = each(lambda x, y: _dot_nt(heads2(x), y), at, bk)
    xr = each(lambda x, y: _dot_nt(heads2(x), y), rt, bk)
    n = each(lambda x: jnp.where(strict_bd, left(x), 0.0), xa)
    m = each(lambda x: bf(jnp.where(strict_bd, right(x), 0.0)), xa)
    p = each(lambda x: bf(jnp.where(incl_bd, left(x), 0.0)), xr)
    q = each(lambda x: bf(jnp.where(incl_bd, right(x), 0.0)), xr)
    vb = each(bf, v)
    v_st = each(stack, vb)
    mv = each(_dot, m, v_st)
    w = each(lambda x: eye + x, n)
    pw = each(bf, n)
    for _ in range(int(math.log2(L)) - 1):
        pw = each(lambda x: bf(_dot(x, x)), pw)
        w = each(lambda x, y: x + _dot(bf(x), y), w, pw)
    au = each(lambda ww, x, y: _dot(bf(ww), jnp.concatenate([stack(bf(x)), bf(y)], axis=1)),
              w, at, mv)
    pau = each(lambda x, y: _dot(x, bf(y)), p, au)
    qv = each(_dot, q, v_st)
    rbar = each(lambda x, y: bf(x + unstack(left(y))), rt, pau)
    ybar = each(lambda x, y: unstack(right(x) + y), pau, qv)
    bh = each(lambda x, t: stack(bf(x * t)), b, tail)
    kh = each(lambda x, t: bf(x * t), k, tail)
    abar = each(lambda x: bf(jnp.where(bd, left(x), 0.0)), au)
    ubar = each(lambda x: bf(jnp.where(bd, right(x), 0.0)), au)
    tt = each(lambda x, y: bf(jnp.where(bd, _dot_tn(x, y), 0.0)), abar, bh)
    z = each(lambda u, vv, x, y: jnp.where(bd, _dot_tn(jnp.concatenate([u, vv], axis=0),
                                                       jnp.concatenate([x, y], axis=0)), 0.0),
             ubar, vb, bh, kh)
    sb = each(bf, st)
    y = each(lambda x, s, yb: _dot_nt(x, s) + yb, rbar, sb, ybar)
    st_new = each(lambda s, c, s16, t, zz: s * jnp.exp(c) + _dot(s16, t) + zz, st, cl, sb, tt, z)
    return y, st_new


def _scan_kernel(r_ref, lw_ref, k_ref, v_ref, a_ref, b_ref, y_ref, st_ref, *, ts, L, pairs, hd):
    @pl.when(pl.program_id(2) == 0)
    def _():
        st_ref[...] = jnp.zeros_like(st_ref)

    ri = lax.broadcasted_iota(jnp.int32, (L, L), 0)
    ci = lax.broadcasted_iota(jnp.int32, (L, L), 1)
    lane = lax.broadcasted_iota(jnp.int32, (1, V7X_LANES), 1)
    r2 = lax.broadcasted_iota(jnp.int32, (2 * L, 2 * L), 0)
    c2 = lax.broadcasted_iota(jnp.int32, (2 * L, 2 * L), 1)
    bd = (r2 < L) == (c2 < L)
    rl, cl2 = r2 & (L - 1), c2 & (L - 1)
    consts = (jnp.where(ri >= ci, 1.0, 0.0).astype(BF16), bd & (rl > cl2), bd & (rl >= cl2),
              jnp.where(r2 == c2, 1.0, 0.0), lane < hd, bd)

    def body(c, carry):
        rows = pl.ds(pl.multiple_of(c * L, L), L)
        cols = [slice(p * V7X_LANES, (p + 1) * V7X_LANES) for p in range(pairs)]
        load = lambda ref: [ref[rows, cs] for cs in cols]
        ys, sts = _scan_chunk(load(r_ref), load(lw_ref), load(k_ref), load(v_ref), load(a_ref),
                              load(b_ref), [st_ref[p] for p in range(pairs)], consts)
        for p in range(pairs):
            y_ref[rows, cols[p]] = ys[p]
            st_ref[p] = sts[p]
        return carry

    lax.fori_loop(0, ts // L, body, 0)


def _rwkv_scan(cfg, r, lw, k, v, a, b):
    bsz, s, d = r.shape
    hd = cfg.rwkv_head_dim
    assert 2 * hd == V7X_LANES
    ts, L = min(cfg.scan_rows, s), cfg.scan_chunk
    npairs = d // V7X_LANES
    pairs = min(cfg.scan_pairs, npairs)
    assert s % ts == 0 and ts % L == 0 and npairs % pairs == 0
    blk = pl.BlockSpec((None, ts, pairs * V7X_LANES), lambda bb, p, i: (bb, i, p))
    return pl.pallas_call(
        functools.partial(_scan_kernel, ts=ts, L=L, pairs=pairs, hd=hd),
        grid=(bsz, npairs // pairs, s // ts),
        in_specs=[blk] * 6,
        out_specs=blk,
        out_shape=jax.ShapeDtypeStruct((bsz, s, d), F32),
        scratch_shapes=[pltpu.VMEM((pairs, V7X_LANES, V7X_LANES), F32)],
        compiler_params=_cparams(cfg, "parallel", "parallel", "arbitrary"),
        name="rwkv7_scan",
    )(r, lw, k, v, a, b)


def _rwkv_post_kernel(y_ref, r_ref, k_ref, v_ref, g_ref, rk_ref, lg_ref, lb_ref, o_ref, *, hd):
    gmat = _group_ones(V7X_LANES, hd)
    y = y_ref[...]
    mu = _group_sum(y, gmat) * (1.0 / hd)
    d = y - mu
    var = _group_sum(d * d, gmat) * (1.0 / hd)
    yn = d * lax.rsqrt(var + RWKV_LNX_EPS) * lg_ref[...] + lb_ref[...]
    bonus = _group_sum(r_ref[...] * k_ref[...] * rk_ref[...], gmat) * v_ref[...]
    o_ref[...] = ((yn + bonus) * g_ref[...]).astype(o_ref.dtype)


def _rwkv_post(cfg, y, r, k, v, g, r_k, lnx_g, lnx_b):
    bsz, s, d = y.shape
    ts = min(cfg.row_tile, s)
    row = pl.BlockSpec((None, ts, d), lambda b, i: (b, i, 0))
    vec = pl.BlockSpec((1, d), lambda b, i: (0, 0))
    return pl.pallas_call(
        functools.partial(_rwkv_post_kernel, hd=cfg.rwkv_head_dim),
        grid=(bsz, s // ts),
        in_specs=[row] * 5 + [vec] * 3,
        out_specs=row,
        out_shape=jax.ShapeDtypeStruct((bsz, s, d), BF16),
        compiler_params=_cparams(cfg, "parallel", "parallel"),
        name="rwkv_groupnorm_gate",
    )(y, r, k, v, g, r_k.reshape(1, d), lnx_g.reshape(1, d), lnx_b.reshape(1, d))


def _even_mixer(cfg, h, w_in, w_out):
    bsz, s, d = h.shape
    proj = _matmul(cfg, [h.reshape(bsz * s, d)], w_in, 0, cfg.even_in, F32,
                   cfg.mm_tm, cfg.mm_tn, single_buffer_x=True).reshape(bsz, s, cfg.even_in)
    o_m = _moba(cfg, proj).reshape(bsz * s, cfg.dm)
    o_r = _retention(cfg, proj).reshape(bsz * s, cfg.dv)
    return _matmul(cfg, [o_m, o_r], w_out, 0, d, BF16, cfg.mm_tm // 2,
                   cfg.mm_tn // 2).reshape(bsz, s, d)


def _odd_mixer(cfg, h, w_in, w_out, conv_w, conv_b, conv_ln_g, conv_ln_b, mu, w0, w_up, a0, a_up,
               g_up, k_k, k_a, r_k, lnx_g, lnx_b):
    bsz, s, d = h.shape
    h2 = h.reshape(bsz * s, d)
    w_nk = jnp.swapaxes(w_in, 1, 2)
    proj = _matmul(cfg, [h2], w_nk, 0, cfg.odd_main, F32,
                   cfg.mm_tm, cfg.mm_tn, w_is_nk=True, single_buffer_x=True).reshape(bsz, s, cfg.odd_main)
    lw_pad = V7X_LANES
    wl = w_nk[0, cfg.odd_main:]
    zr = lambda n: jnp.zeros((n, d), F32)
    o1, o2 = cfg.decay_lora, cfg.decay_lora + cfg.iclr_lora
    wl = jnp.concatenate([wl[:o1], zr(lw_pad - cfg.decay_lora), wl[o1:o2],
                          zr(lw_pad - cfg.iclr_lora), wl[o2:]], axis=0)[None]
    lo_w = wl.shape[1]
    lora = _matmul(cfg, [h2], wl, 0, lo_w, F32, cfg.mm_tm // 2, lo_w,
                   w_is_nk=True).reshape(bsz, s, lo_w)

    u = _ln_silu(cfg, _conv_glu(cfg, proj, conv_w, conv_b), conv_ln_g, conv_ln_b)
    r, lw, k, v, a, b, g = _rwkv_pre(cfg, proj, lora, mu, w0, w_up, a0, a_up, g_up, k_k, k_a)
    y = _rwkv_scan(cfg, r, lw, k, v, a, b)
    y = _rwkv_post(cfg, y, r, k, v, g, r_k.reshape(-1), lnx_g, lnx_b)
    return _matmul(cfg, [u.reshape(bsz * s, -1), y.reshape(bsz * s, -1)], w_out, 0, d, BF16,
                   cfg.mm_tm, cfg.mm_tn, single_buffer_x=True).reshape(bsz, s, d)


def _forward(cfg, x, c, w_ada, b_ada, norm_g, w_ffn_in, w_ffn_out, even_w_in, even_w_out, odd_w_in,
             odd_w_out, conv_w, conv_b, conv_ln_g, conv_ln_b, rwkv_mu, rwkv_w0, rwkv_w_up, rwkv_a0,
             rwkv_a_up, rwkv_g_up, rwkv_k_k, rwkv_k_a, rwkv_r_k, rwkv_lnx_g, rwkv_lnx_b):
    bsz, s, d = x.shape
    depth = w_ada.shape[0]
    mods = _modulation(cfg, c, w_ada, b_ada)
    sh_m, sc_m = mods[0, :, 0], mods[0, :, 1]
    h = _norm_mod(cfg, x, norm_g[0, 0], sc_m, sh_m)
    for layer in range(depth):
        g_m, sh_f, sc_f, g_f = (mods[layer, :, i] for i in (2, 3, 4, 5))
        j = layer // 2
        if layer % 2 == 0:
            o = _even_mixer(cfg, h, even_w_in[j:j + 1], even_w_out[j:j + 1])
        else:
            o = _odd_mixer(cfg, h, odd_w_in[j:j + 1], odd_w_out[j:j + 1], conv_w[j], conv_b[j],
                           conv_ln_g[j], conv_ln_b[j], rwkv_mu[j], rwkv_w0[j], rwkv_w_up[j],
                           rwkv_a0[j], rwkv_a_up[j], rwkv_g_up[j], rwkv_k_k[j], rwkv_k_a[j],
                           rwkv_r_k[j], rwkv_lnx_g[j], rwkv_lnx_b[j])
        x, h = _resid(cfg, x, o, norm_g[layer, 1], g_m, (norm_g[layer, 2], sc_f, sh_f))
        act = _ffn_in(cfg, h.reshape(bsz * s, d), w_ffn_in, layer)
        f = _matmul(cfg, [act], w_ffn_out, layer, d, BF16, cfg.ffn_out_tm, cfg.ffn_tn,
                    single_buffer_x=True).reshape(bsz, s, d)
        if layer + 1 < depth:
            nxt = (norm_g[layer + 1, 0], mods[layer + 1, :, 1], mods[layer + 1, :, 0])
            x, h = _resid(cfg, x, f, norm_g[layer, 3], g_f, nxt)
        else:
            x = _resid(cfg, x, f, norm_g[layer, 3], g_f)
    return x


def kernel(x, c, w_ada, b_ada, norm_g, w_ffn_in, w_ffn_out, even_w_in, even_w_out, odd_w_in, odd_w_out, conv_w, conv_b, conv_ln_g, conv_ln_b, rwkv_mu, rwkv_w0, rwkv_w_up, rwkv_a0, rwkv_a_up, rwkv_g_up, rwkv_k_k, rwkv_k_a, rwkv_r_k, rwkv_lnx_g, rwkv_lnx_b):
    return _forward(Config(), x, c, w_ada, b_ada, norm_g, w_ffn_in, w_ffn_out, even_w_in, even_w_out,
                    odd_w_in, odd_w_out, conv_w, conv_b, conv_ln_g, conv_ln_b, rwkv_mu, rwkv_w0,
                    rwkv_w_up, rwkv_a0, rwkv_a_up, rwkv_g_up, rwkv_k_k, rwkv_k_a, rwkv_r_k,
                    rwkv_lnx_g, rwkv_lnx_b)
```

```python
import dataclasses
import functools
import math

import jax
import jax.numpy as jnp
from jax import lax
from jax.experimental import pallas as pl
from jax.experimental.pallas import tpu as pltpu

F32 = jnp.float32
BF16 = jnp.bfloat16
HIGHEST = lax.Precision.HIGHEST

V7X_LANES = 128
V7X_SUBLANES = 8
V7X_VMEM_MIB = 64
MIB = 1024 * 1024
NORM_EPS = 1e-6
ROPE_THETA = 10000.0
CONV_LN_EPS = 1e-5
MOBA_ONES_ROWS = 16
RWKV_LNX_EPS = 64e-5


@dataclasses.dataclass(frozen=True)
class Config:
    d_model: int = 4096
    moba_heads: int = 16
    moba_head_dim: int = 128
    moba_block: int = 256
    moba_topk: int = 3
    ret_heads: int = 8
    ret_key_dim: int = 256
    ret_val_dim: int = 512
    ret_chunk: int = 128
    conv_ch: int = 2048
    conv_width: int = 31
    rwkv_dim: int = 2048
    rwkv_head_dim: int = 64
    decay_lora: int = 96
    iclr_lora: int = 96
    gate_lora: int = 256
    ffn_hidden: int = 11008
    row_tile: int = 256
    mm_tm: int = 2048
    ffn_out_tm: int = 1024
    mm_tn: int = 512
    ffn_tm: int = 2048
    ffn_tn: int = 256
    ret_rows: int = 1024
    ret_heads_per_step: int = 2
    conv_rows: int = 256
    conv_cols: int = 256
    scan_rows: int = 256
    scan_chunk: int = 64
    scan_pairs: int = 16
    moba_group: int = 4
    moba_heads_per_step: int = 4
    vmem_mib: int = V7X_VMEM_MIB - 8

    @property
    def dm(self):
        return self.moba_heads * self.moba_head_dim

    @property
    def dk(self):
        return self.ret_heads * self.ret_key_dim

    @property
    def dv(self):
        return self.ret_heads * self.ret_val_dim

    @property
    def even_in(self):
        return 3 * self.dm + 2 * self.dk + 2 * self.dv

    @property
    def odd_main(self):
        return 2 * self.conv_ch + 3 * self.rwkv_dim


def _cparams(cfg, *sem):
    return pltpu.CompilerParams(dimension_semantics=sem, vmem_limit_bytes=cfg.vmem_mib * MIB)


def _silu(x):
    return x * jax.nn.sigmoid(x)


def _dot(a, b, **kw):
    return jnp.dot(a, b, preferred_element_type=F32, **kw)


def _dot_nt(a, b, **kw):
    return lax.dot_general(a, b, (((1,), (1,)), ((), ())), preferred_element_type=F32, **kw)


def _dot_tn(a, b, **kw):
    return lax.dot_general(a, b, (((0,), (0,)), ((), ())), preferred_element_type=F32, **kw)


def _ada_kernel(c_ref, w_ref, b_ref, o_ref):
    s = _silu(c_ref[...])
    hi = s.astype(BF16).astype(F32)
    parts = _dot(jnp.concatenate([hi, s - hi], axis=0).astype(BF16), w_ref[...].astype(BF16))
    o_ref[...] = parts[:V7X_SUBLANES] + parts[V7X_SUBLANES:] + b_ref[...]


def _modulation(cfg, c, w_ada, b_ada, tn=512):
    depth, d, n = w_ada.shape
    bsz = c.shape[0]
    cp = jnp.zeros((V7X_SUBLANES, d), F32).at[:bsz].set(c)
    out = pl.pallas_call(
        _ada_kernel,
        grid=(depth, n // tn),
        in_specs=[
            pl.BlockSpec((V7X_SUBLANES, d), lambda l, j: (0, 0)),
            pl.BlockSpec((None, d, tn), lambda l, j: (l, 0, j)),
            pl.BlockSpec((None, 1, tn), lambda l, j: (l, 0, j)),
        ],
        out_specs=pl.BlockSpec((None, V7X_SUBLANES, tn), lambda l, j: (l, 0, j)),
        out_shape=jax.ShapeDtypeStruct((depth, V7X_SUBLANES, n), F32),
        compiler_params=_cparams(cfg, "parallel", "parallel"),
        name="adaln_modulation",
    )(cp, w_ada, b_ada.reshape(depth, 1, n))
    return out[:, :bsz].reshape(depth, bsz, 6, 1, d)


def _rms(x, g):
    return x * lax.rsqrt(jnp.mean(x * x, axis=-1, keepdims=True) + NORM_EPS) * g


def _norm_mod_kernel(x_ref, g_ref, sc_ref, sh_ref, o_ref):
    y = _rms(x_ref[...], g_ref[...])
    o_ref[...] = (y * (1.0 + sc_ref[...]) + sh_ref[...]).astype(o_ref.dtype)


def _norm_mod(cfg, x, g, sc, sh):
    bsz, s, d = x.shape
    ts = min(cfg.row_tile, s)
    row = pl.BlockSpec((None, ts, d), lambda b, i: (b, i, 0))
    vec = pl.BlockSpec((1, d), lambda b, i: (0, 0))
    mod = pl.BlockSpec((None, 1, d), lambda b, i: (b, 0, 0))
    return pl.pallas_call(
        _norm_mod_kernel,
        grid=(bsz, s // ts),
        in_specs=[row, vec, mod, mod],
        out_specs=row,
        out_shape=jax.ShapeDtypeStruct((bsz, s, d), BF16),
        compiler_params=_cparams(cfg, "parallel", "parallel"),
        name="norm_modulate",
    )(x, g.reshape(1, d), sc, sh)


def _resid_kernel(x_ref, o_ref, ga_ref, gate_ref, *rest, with_h):
    xn = x_ref[...] + gate_ref[...] * _rms(o_ref[...].astype(F32), ga_ref[...])
    if with_h:
        gb_ref, sc_ref, sh_ref, xn_ref, h_ref = rest
        xn_ref[...] = xn
        h_ref[...] = (_rms(xn, gb_ref[...]) * (1.0 + sc_ref[...]) + sh_ref[...]).astype(h_ref.dtype)
    else:
        (xn_ref,) = rest
        xn_ref[...] = xn


def _resid(cfg, x, o, ga, gate, nxt=None):
    bsz, s, d = x.shape
    ts = min(cfg.row_tile, s)
    row = pl.BlockSpec((None, ts, d), lambda b, i: (b, i, 0))
    vec = pl.BlockSpec((1, d), lambda b, i: (0, 0))
    mod = pl.BlockSpec((None, 1, d), lambda b, i: (b, 0, 0))
    with_h = nxt is not None
    in_specs = [row, row, vec, mod]
    args = [x, o, ga.reshape(1, d), gate]
    out_specs = [row]
    out_shape = [jax.ShapeDtypeStruct((bsz, s, d), F32)]
    if with_h:
        gb, sc, sh = nxt
        in_specs += [vec, mod, mod]
        args += [gb.reshape(1, d), sc, sh]
        out_specs.append(row)
        out_shape.append(jax.ShapeDtypeStruct((bsz, s, d), BF16))
    outs = pl.pallas_call(
        functools.partial(_resid_kernel, with_h=with_h),
        grid=(bsz, s // ts),
        in_specs=in_specs,
        out_specs=out_specs,
        out_shape=out_shape,
        compiler_params=_cparams(cfg, "parallel", "parallel"),
        name="residual_norm",
    )(*args)
    return outs if with_h else outs[0]


def _mm_kernel(*refs, widths, w_is_nk):
    x_refs, (w_ref, o_ref) = refs[:len(widths)], refs[len(widths):]
    acc, off = None, 0
    for x_ref, wd in zip(x_refs, widths):
        if w_is_nk:
            part = _dot_nt(x_ref[...], w_ref[:, off:off + wd].astype(BF16))
        else:
            part = _dot(x_ref[...], w_ref[off:off + wd, :].astype(BF16))
        acc = part if acc is None else acc + part
        off += wd
    o_ref[...] = acc.astype(o_ref.dtype)


def _matmul(cfg, xs, w, layer, n, out_dtype, tm, tn, w_is_nk=False, single_buffer_x=False):
    m = xs[0].shape[0]
    widths = tuple(x.shape[1] for x in xs)
    kdim = sum(widths)
    tm, tn = min(tm, m), min(tn, n)
    assert m % tm == 0 and n % tn == 0 and w.shape[2 if w_is_nk else 1] == kdim
    wspec = (pl.BlockSpec((None, tn, kdim), lambda i, j: (layer, j, 0)) if w_is_nk
             else pl.BlockSpec((None, kdim, tn), lambda i, j: (layer, 0, j)))
    xmode = dict(pipeline_mode=pl.Buffered(1)) if single_buffer_x else {}
    return pl.pallas_call(
        functools.partial(_mm_kernel, widths=widths, w_is_nk=w_is_nk),
        grid=(m // tm, n // tn),
        in_specs=[pl.BlockSpec((tm, wd), lambda i, j: (i, 0), **xmode) for wd in widths] + [wspec],
        out_specs=pl.BlockSpec((tm, tn), lambda i, j: (i, j)),
        out_shape=jax.ShapeDtypeStruct((m, n), out_dtype),
        compiler_params=_cparams(cfg, "parallel", "parallel"),
        name="matmul",
    )(*xs, w)


def _ffn_in_kernel(x_ref, wg_ref, wu_ref, o_ref):
    x = x_ref[...]
    gate = _dot(x, wg_ref[...].astype(BF16))
    up = _dot(x, wu_ref[...].astype(BF16))
    o_ref[...] = (_silu(gate) * up).astype(o_ref.dtype)


def _ffn_in(cfg, x, w, layer):
    m, kdim = x.shape
    hid = w.shape[2] // 2
    tm, tn = min(cfg.ffn_tm, m), min(cfg.ffn_tn, hid)
    nt = hid // tn
    assert m % tm == 0 and hid % tn == 0
    return pl.pallas_call(
        _ffn_in_kernel,
        grid=(m // tm, nt),
        in_specs=[
            pl.BlockSpec((tm, kdim), lambda i, j: (i, 0)),
            pl.BlockSpec((None, kdim, tn), lambda i, j: (layer, 0, j)),
            pl.BlockSpec((None, kdim, tn), lambda i, j: (layer, 0, nt + j)),
        ],
        out_specs=pl.BlockSpec((tm, tn), lambda i, j: (i, j)),
        out_shape=jax.ShapeDtypeStruct((m, hid), BF16),
        compiler_params=_cparams(cfg, "parallel", "parallel"),
        name="ffn_in_swiglu",
    )(x, w, w)


def _moba_kernel(q_ref, k_ref, v_ref, cq_ref, sq_ref, ck_ref, sk_ref, o_ref,
                 kr_ref, vt_ref, km_ref, sel_ref, *, nb, blk, dh, topk, group, hp):
    qi = pl.program_id(2)
    half = dh // 2
    heads = range(hp)
    lanes = [slice(n * dh, (n + 1) * dh) for n in heads]
    each = lambda f, *ls: [f(*xs) for xs in zip(*ls)]
    colmax = lambda x: jnp.max(x, axis=0, keepdims=True)

    @pl.when(qi == 0)
    def _():
        for j in range(nb):
            rows = slice(j * blk, (j + 1) * blk)
            for n in heads:
                kb = k_ref[rows, lanes[n]]
                kr = kb * ck_ref[rows, :] + pltpu.roll(kb, half, 1) * sk_ref[rows, :]
                km_ref[n, j:j + 1, :] = jnp.mean(kr, axis=0, keepdims=True)
                kr_ref[n, j] = kr.astype(BF16)
                vt_ref[n, j, 0:dh, :] = v_ref[rows, lanes[n]].T.astype(BF16)
                vt_ref[n, j, dh:dh + MOBA_ONES_ROWS, :] = jnp.ones((MOBA_ONES_ROWS, blk), BF16)

    cq, sq = cq_ref[...], sq_ref[...]
    qr = [(lambda q: q * cq + pltpu.roll(q, half, 1) * sq)(q_ref[:, lanes[n]]) for n in heads]

    gate = [_dot_nt(km_ref[n], qr[n], precision=HIGHEST) for n in heads]
    brow = lax.broadcasted_iota(jnp.int32, (nb, blk), 0)
    browf = brow.astype(F32)
    gm = each(lambda g: jnp.where(brow < qi, g, -jnp.inf), gate)
    keep = [jnp.zeros((nb, blk), F32) for _ in heads]
    for _ in range(topk):
        mx = each(colmax, gm)
        first = each(lambda g, t: jnp.min(jnp.where(g == t, browf, float(nb)), axis=0, keepdims=True),
                     gm, mx)
        pick = each(lambda t, f: jnp.where(jnp.abs(t) < jnp.inf, f, -1.0), mx, first)
        keep = each(lambda kp, pk: jnp.where(browf == pk, 1.0, kp), keep, pick)
        gm = each(lambda g, pk: jnp.where(browf == pk, -jnp.inf, g), gm, pick)
    for j in range(nb):
        for n in heads:
            sel_ref[n, j] = jnp.broadcast_to(keep[n][j:j + 1, :], (V7X_SUBLANES, blk))

    qs = each(lambda x: (x * (dh ** -0.5 * math.log2(math.e))).T.astype(BF16), qr)
    kpos = lax.broadcasted_iota(jnp.int32, (blk, blk), 0)
    qpos = lax.broadcasted_iota(jnp.int32, (blk, blk), 1)
    s = [jnp.where(kpos <= qpos, _dot(kr_ref[n, qi], qs[n]), -jnp.inf) for n in heads]
    m = each(colmax, s)
    p = each(lambda x, y: jnp.exp2(x - y), s, m)
    acc = [_dot(vt_ref[n, qi], p[n].astype(BF16)) for n in heads]

    def past_blocks(width, first):
        def body(g, carry):
            m, acc = carry
            js = [first + g * width + u for u in range(width)]
            ss = [[jnp.where(sel_ref[n, j][0:1, :] > 0.0, _dot(kr_ref[n, j], qs[n]), -jnp.inf)
                   for n in heads] for j in js]
            m_new = list(m)
            for su in ss:
                m_new = each(lambda x, y: jnp.maximum(x, colmax(y)), m_new, su)
            acc = each(lambda x, y, a: jnp.exp2(x - y) * a, m, m_new, acc)
            for j, su in zip(js, ss):
                p = each(lambda x, y: jnp.exp2(x - y), su, m_new)
                acc = [acc[n] + _dot(vt_ref[n, j], p[n].astype(BF16)) for n in heads]
            return tuple(m_new), tuple(acc)
        return body

    whole = qi // group
    carry = lax.fori_loop(0, whole, past_blocks(group, 0), (tuple(m), tuple(acc)))
    m, acc = lax.fori_loop(0, qi - whole * group, past_blocks(1, whole * group), carry)
    for n in heads:
        o_ref[:, lanes[n]] = (acc[n][:dh] / acc[n][dh:dh + 1]).T.astype(o_ref.dtype)


def _rope_tables(seq, dim):
    inv = 1.0 / (ROPE_THETA ** (jnp.arange(0, dim, 2, dtype=F32) / dim))
    ang = jnp.arange(seq, dtype=F32)[:, None] * inv[None, :]
    return jnp.cos(ang), jnp.sin(ang)


def _moba(cfg, proj):
    bsz, s, _ = proj.shape
    h, dh, blk = cfg.moba_heads, cfg.moba_head_dim, cfg.moba_block
    assert dh == V7X_LANES and s % blk == 0
    nb = s // blk
    group = math.gcd(cfg.moba_group, nb)
    hp = math.gcd(cfg.moba_heads_per_step, h)
    hg = h // hp
    cos, sin = _rope_tables(s, dh)
    cosf = jnp.concatenate([cos, cos], axis=1)
    sinf = jnp.concatenate([-sin, sin], axis=1)
    qspec = pl.BlockSpec((None, blk, hp * dh), lambda b, hh, i: (b, i, hh))
    kspec = pl.BlockSpec((None, s, hp * dh), lambda b, hh, i: (b, 0, hg + hh))
    vspec = pl.BlockSpec((None, s, hp * dh), lambda b, hh, i: (b, 0, 2 * hg + hh))
    tq = pl.BlockSpec((blk, dh), lambda b, hh, i: (i, 0))
    tk = pl.BlockSpec((s, dh), lambda b, hh, i: (0, 0), pipeline_mode=pl.Buffered(1))
    return pl.pallas_call(
        functools.partial(_moba_kernel, nb=nb, blk=blk, dh=dh, topk=cfg.moba_topk, group=group,
                          hp=hp),
        grid=(bsz, hg, nb),
        in_specs=[qspec, kspec, vspec, tq, tq, tk, tk],
        out_specs=pl.BlockSpec((None, blk, hp * dh), lambda b, hh, i: (b, i, hh)),
        out_shape=jax.ShapeDtypeStruct((bsz, s, h * dh), BF16),
        scratch_shapes=[
            pltpu.VMEM((hp, nb, blk, dh), BF16),
            pltpu.VMEM((hp, nb, dh + MOBA_ONES_ROWS, blk), BF16),
            pltpu.VMEM((hp, nb, dh), F32),
            pltpu.VMEM((hp, nb, V7X_SUBLANES, blk), F32),
        ],
        compiler_params=_cparams(cfg, "parallel", "parallel", "arbitrary"),
        name="moba_attention",
    )(proj, proj, proj, cosf, sinf, cosf, sinf)


def _ret_kernel(q_ref, k_ref, v_ref, g_ref, cos_ref, sin_ref, dm_ref, qd_ref, kd_ref, cd_ref,
                o_ref, st_ref, *, c, nsub, dk, dv, hp):
    @pl.when(pl.program_id(2) == 0)
    def _():
        st_ref[...] = jnp.zeros_like(st_ref)

    half = dk // 2
    heads = range(hp)
    each = lambda f, *ls: [f(*xs) for xs in zip(*ls)]
    bf = lambda x: x.astype(BF16)
    dm, qd, kd, cd = ([ref[n] for n in heads] for ref in (dm_ref, qd_ref, kd_ref, cd_ref))
    for sidx in range(nsub):
        rows = slice(sidx * c, (sidx + 1) * c)
        cos = cos_ref[rows, :]
        sin = sin_ref[rows, :]

        def rope(x):
            x1, x2 = x[:, :half], x[:, half:]
            return jnp.concatenate([x1 * cos - x2 * sin, x2 * cos + x1 * sin], axis=1)

        q = [rope(q_ref[rows, n * dk:(n + 1) * dk]) for n in heads]
        k = [rope(k_ref[rows, n * dk:(n + 1) * dk]) * (dk ** -0.5) for n in heads]
        qb, kb = each(bf, q), each(bf, k)
        vb = [bf(v_ref[rows, n * dv:(n + 1) * dv]) for n in heads]
        st = [st_ref[n] for n in heads]
        inner = each(lambda x, y, d: bf(_dot_nt(x, y) * d), qb, kb, dm)
        cross = each(lambda x, s, d: _dot(x, bf(s)) * d, qb, st, qd)
        o = each(lambda i, v, x: _dot(i, v) + x, inner, vb, cross)
        kdb = each(lambda x, d: bf(x * d), k, kd)
        new_st = each(lambda s, d, x, v: s * d + _dot_tn(x, v), st, cd, kdb, vb)
        for n in heads:
            st_ref[n] = new_st[n]
        on = each(lambda x: x * lax.rsqrt(jnp.mean(x * x, axis=-1, keepdims=True) + NORM_EPS), o)
        for n in heads:
            cols = slice(n * dv, (n + 1) * dv)
            o_ref[rows, cols] = (on[n] * _silu(g_ref[rows, cols])).astype(o_ref.dtype)


def _retention(cfg, proj):
    bsz, s, _ = proj.shape
    h, dk, dv, c = cfg.ret_heads, cfg.ret_key_dim, cfg.ret_val_dim, cfg.ret_chunk
    ts = min(cfg.ret_rows, s)
    assert s % ts == 0 and ts % c == 0
    q0 = 3 * cfg.dm // dk
    k0 = (3 * cfg.dm + cfg.dk) // dk
    v0 = (3 * cfg.dm + 2 * cfg.dk) // dv
    g0 = (3 * cfg.dm + 2 * cfg.dk + cfg.dv) // dv
    assert (3 * cfg.dm) % dk == 0 and (3 * cfg.dm + 2 * cfg.dk) % dv == 0
    cos, sin = _rope_tables(s, dk)
    log_g = jnp.log1p(-jnp.exp2(-5.0 - jnp.arange(h, dtype=F32)))
    idx = jnp.arange(c, dtype=F32)
    diff = idx[:, None] - idx[None, :]
    dmask = jnp.where(diff >= 0, jnp.exp(jnp.maximum(diff, 0.0) * log_g[:, None, None]), 0.0)
    qdec = jnp.exp((idx + 1.0) * log_g[:, None])[..., None]
    kdec = jnp.exp((c - 1.0 - idx) * log_g[:, None])[..., None]
    cdec = jnp.broadcast_to(jnp.exp(c * log_g)[:, None, None], (h, 1, dv))
    hp = math.gcd(cfg.ret_heads_per_step, h)
    assert q0 % hp == 0 and k0 % hp == 0 and v0 % hp == 0 and g0 % hp == 0
    rowspec = lambda w, c0: pl.BlockSpec((None, ts, hp * w), lambda b, hh, i: (b, i, c0 // hp + hh))
    tab = pl.BlockSpec((ts, dk // 2), lambda b, hh, i: (i, 0))
    const = lambda r, w: pl.BlockSpec((hp, r, w), lambda b, hh, i: (hh, 0, 0))
    return pl.pallas_call(
        functools.partial(_ret_kernel, c=c, nsub=ts // c, dk=dk, dv=dv, hp=hp),
        grid=(bsz, h // hp, s // ts),
        in_specs=[
            rowspec(dk, q0), rowspec(dk, k0), rowspec(dv, v0), rowspec(dv, g0), tab, tab,
            const(c, c), const(c, 1), const(c, 1), const(1, dv),
        ],
        out_specs=pl.BlockSpec((None, ts, hp * dv), lambda b, hh, i: (b, i, hh)),
        out_shape=jax.ShapeDtypeStruct((bsz, s, h * dv), BF16),
        scratch_shapes=[pltpu.VMEM((hp, dk, dv), F32)],
        compiler_params=_cparams(cfg, "parallel", "parallel", "arbitrary"),
        name="retention",
    )(proj, proj, proj, proj, cos, sin, dmask, qdec, kdec, cdec)


CONV_HALO = 32


def _conv_kernel(a_ref, g_ref, w_ref, b_ref, o_ref, buf_ref, sh_ref, *, ts, kw, rc):
    @pl.when(pl.program_id(2) == 0)
    def _():
        buf_ref[0:CONV_HALO, :] = jnp.zeros((CONV_HALO, buf_ref.shape[1]), F32)

    buf_ref[CONV_HALO:CONV_HALO + ts, :] = a_ref[...] * jax.nn.sigmoid(g_ref[...])
    nshift = CONV_HALO + ts - V7X_SUBLANES
    for ph in range(1, V7X_SUBLANES):
        sh_ref[ph, 0:nshift, :] = buf_ref[ph:ph + nshift, :]
    off = CONV_HALO - (kw - 1)
    bias = b_ref[...]
    for r0 in range(0, ts, rc):
        acc = jnp.broadcast_to(bias, (rc, bias.shape[1]))
        for j in range(kw):
            ph = (off + j) % V7X_SUBLANES
            base = off + j - ph + r0
            rows = buf_ref[base:base + rc, :] if ph == 0 else sh_ref[ph, base:base + rc, :]
            acc = acc + w_ref[j:j + 1, :] * rows
        o_ref[r0:r0 + rc, :] = acc
    buf_ref[0:CONV_HALO, :] = buf_ref[ts:ts + CONV_HALO, :]


def _conv_glu(cfg, proj, conv_w, conv_b):
    bsz, s, _ = proj.shape
    ch, kw = cfg.conv_ch, cfg.conv_width
    ts, tc = min(cfg.conv_rows, s), min(cfg.conv_cols, ch)
    assert kw - 1 <= CONV_HALO <= ts and s % ts == 0 and ch % tc == 0
    nct = ch // tc
    wp = jnp.zeros((CONV_HALO, ch), F32).at[:kw].set(conv_w)
    return pl.pallas_call(
        functools.partial(_conv_kernel, ts=ts, kw=kw, rc=32),
        grid=(bsz, nct, s // ts),
        in_specs=[
            pl.BlockSpec((None, ts, tc), lambda b, c, i: (b, i, c)),
            pl.BlockSpec((None, ts, tc), lambda b, c, i: (b, i, nct + c)),
            pl.BlockSpec((CONV_HALO, tc), lambda b, c, i: (0, c)),
            pl.BlockSpec((1, tc), lambda b, c, i: (0, c)),
        ],
        out_specs=pl.BlockSpec((None, ts, tc), lambda b, c, i: (b, i, c)),
        out_shape=jax.ShapeDtypeStruct((bsz, s, ch), F32),
        scratch_shapes=[pltpu.VMEM((CONV_HALO + ts, tc), F32),
                        pltpu.VMEM((V7X_SUBLANES, CONV_HALO + ts, tc), F32)],
        compiler_params=_cparams(cfg, "parallel", "parallel", "arbitrary"),
        name="glu_causal_conv",
    )(proj, proj, wp, conv_b.reshape(1, ch))


def _ln_silu_kernel(x_ref, g_ref, b_ref, o_ref):
    x = x_ref[...]
    mu = jnp.mean(x, axis=-1, keepdims=True)
    d = x - mu
    var = jnp.mean(d * d, axis=-1, keepdims=True)
    y = d * lax.rsqrt(var + CONV_LN_EPS) * g_ref[...] + b_ref[...]
    o_ref[...] = _silu(y).astype(o_ref.dtype)


def _ln_silu(cfg, x, g, b):
    bsz, s, d = x.shape
    ts = min(cfg.row_tile, s)
    row = pl.BlockSpec((None, ts, d), lambda bb, i: (bb, i, 0))
    vec = pl.BlockSpec((1, d), lambda bb, i: (0, 0))
    return pl.pallas_call(
        _ln_silu_kernel,
        grid=(bsz, s // ts),
        in_specs=[row, vec, vec],
        out_specs=row,
        out_shape=jax.ShapeDtypeStruct((bsz, s, d), BF16),
        compiler_params=_cparams(cfg, "parallel", "parallel"),
        name="layernorm_swish",
    )(x, g.reshape(1, d), b.reshape(1, d))


def _group_ones(n, group):
    r = lax.broadcasted_iota(jnp.int32, (n, n), 0)
    c = lax.broadcasted_iota(jnp.int32, (n, n), 1)
    shift = int(math.log2(group))
    return jnp.where((r >> shift) == (c >> shift), 1.0, 0.0).astype(BF16)


def _split2(x):
    hi = x.astype(BF16)
    return hi, (x - hi.astype(F32)).astype(BF16)


def _dot_split(x, w):
    xh, xl = _split2(x)
    wh, wl = _split2(w)
    return _dot(xh, wh) + (_dot(xh, wl) + _dot(xl, wh))


def _group_sum(x, gmat):
    n = x.shape[1]
    hi, lo = _split2(x)
    parts = [_dot(hi[:, s0:s0 + V7X_LANES], gmat) + _dot(lo[:, s0:s0 + V7X_LANES], gmat)
             for s0 in range(0, n, V7X_LANES)]
    return parts[0] if len(parts) == 1 else jnp.concatenate(parts, axis=1)


def _rwkv_pre_kernel(r_ref, k_ref, v_ref, lo_ref, mur_ref, muk_ref, muv_ref, mul_ref,
                     w0_ref, wup_ref, a0_ref, aup_ref, gup_ref, kk_ref, ka_ref,
                     ro_ref, lw_ref, ko_ref, vo_ref, ao_ref, bo_ref, go_ref,
                     lr_ref, lk_ref, lv_ref, ll_ref, *, ts, hd, lw_pad):
    first = pl.program_id(1) == 0

    def shift(x_ref, last_ref, mu_ref):
        @pl.when(first)
        def _():
            last_ref[...] = jnp.zeros_like(last_ref)

        x = x_ref[...]
        row = lax.broadcasted_iota(jnp.int32, x.shape, 0)
        prev = jnp.where(row == 0, last_ref[V7X_SUBLANES - 1:V7X_SUBLANES, :], pltpu.roll(x, 1, 0))
        last_ref[...] = x[ts - V7X_SUBLANES:ts, :]
        return x + (prev - x) * mu_ref[...]

    r = shift(r_ref, lr_ref, mur_ref)
    k = shift(k_ref, lk_ref, muk_ref)
    v = shift(v_ref, lv_ref, muv_ref)
    lo = shift(lo_ref, ll_ref, mul_ref)
    xw, xa, xg = lo[:, :lw_pad], lo[:, lw_pad:2 * lw_pad], lo[:, 2 * lw_pad:]

    z = w0_ref[...] + _dot_split(jnp.tanh(xw), wup_ref[...])
    softplus = jnp.maximum(-z, 0.0) + jnp.log(1.0 + jnp.exp(-jnp.abs(z)))
    lw_ref[...] = -jnp.exp(-softplus - 0.5)
    a = jax.nn.sigmoid(a0_ref[...] + _dot_split(xa, aup_ref[...]))
    go_ref[...] = _dot(jax.nn.sigmoid(xg).astype(BF16), gup_ref[...].astype(BF16))

    kkr = k * kk_ref[...]
    ss = _group_sum(kkr * kkr, _group_ones(V7X_LANES, hd))
    kk = kkr * lax.rsqrt(jnp.maximum(ss, 1e-24))
    ro_ref[...] = r
    vo_ref[...] = v
    ko_ref[...] = k * (1.0 + (a - 1.0) * ka_ref[...])
    ao_ref[...] = -kk
    bo_ref[...] = kk * a


def _rwkv_pre(cfg, proj, lora, mu, w0, w_up, a0, a_up, g_up, k_k, k_a):
    bsz, s, _ = proj.shape
    d = cfg.rwkv_dim
    ts = min(cfg.row_tile // 2, s)
    lw_pad = V7X_LANES
    assert cfg.decay_lora <= lw_pad and cfg.iclr_lora <= lw_pad and (2 * cfg.conv_ch) % d == 0
    lo_w = lora.shape[2]
    c0 = 2 * cfg.conv_ch // d
    pad_rows = lambda w: jnp.zeros((lw_pad, d), F32).at[:w.shape[0]].set(w)
    pad_vec = lambda vv, n: jnp.zeros((1, n), F32).at[0, :vv.shape[0]].set(vv)
    mu_r, mu_k, mu_v = (mu[i * d:(i + 1) * d].reshape(1, d) for i in range(3))
    o = 3 * d
    mu_l = jnp.concatenate([
        pad_vec(mu[o:o + cfg.decay_lora], lw_pad),
        pad_vec(mu[o + cfg.decay_lora:o + cfg.decay_lora + cfg.iclr_lora], lw_pad),
        mu[o + cfg.decay_lora + cfg.iclr_lora:].reshape(1, -1)], axis=1)
    row = lambda cb: pl.BlockSpec((None, ts, d), lambda b, i: (b, i, cb))
    lrow = pl.BlockSpec((None, ts, lo_w), lambda b, i: (b, i, 0))
    vec = lambda n: pl.BlockSpec((1, n), lambda b, i: (0, 0))
    mat = lambda rws: pl.BlockSpec((rws, d), lambda b, i: (0, 0))
    orow = pl.BlockSpec((None, ts, d), lambda b, i: (b, i, 0))
    return pl.pallas_call(
        functools.partial(_rwkv_pre_kernel, ts=ts, hd=cfg.rwkv_head_dim, lw_pad=lw_pad),
        grid=(bsz, s // ts),
        in_specs=[row(c0), row(c0 + 1), row(c0 + 2), lrow, vec(d), vec(d), vec(d), vec(lo_w),
                  vec(d), mat(lw_pad), vec(d), mat(lw_pad), mat(cfg.gate_lora), vec(d), vec(d)],
        out_specs=[orow] * 7,
        out_shape=[jax.ShapeDtypeStruct((bsz, s, d), F32)] * 7,
        scratch_shapes=[pltpu.VMEM((V7X_SUBLANES, d), F32)] * 3 + [pltpu.VMEM((V7X_SUBLANES, lo_w), F32)],
        compiler_params=_cparams(cfg, "parallel", "arbitrary"),
        name="rwkv_token_shift_lora",
    )(proj, proj, proj, lora, mu_r, mu_k, mu_v, mu_l, w0.reshape(1, d), pad_rows(w_up),
      a0.reshape(1, d), pad_rows(a_up), g_up, k_k.reshape(1, d), k_a.reshape(1, d))


def _scan_chunk(r, lw, k, v, a, b, st, consts):
    tri, strict_bd, incl_bd, eye, lane_a, bd = consts
    L = r[0].shape[0]
    each = lambda f, *ls: [f(*xs) for xs in zip(*ls)]
    bf = lambda x: x.astype(BF16)
    stack = lambda x: jnp.concatenate([x, x], axis=0)
    unstack = lambda x: jnp.where(lane_a, x[:L], x[L:])
    left, right = (lambda x: x[:, :V7X_LANES]), (lambda x: x[:, V7X_LANES:])

    cs = each(lambda x: _dot(tri, jnp.concatenate(_split2(x), axis=1)), lw)
    cum = each(lambda x: left(x) + right(x), cs)
    cl = each(lambda x: x[L - 1:L, :], cum)
    tail = each(lambda x, y: jnp.exp(x - y), cl, cum)
    at = each(lambda x, c, w: x * jnp.exp(c - w), a, cum, lw)
    rt = each(lambda x, c: x * jnp.exp(c), r, cum)
    g_inv = each(lambda c: jnp.exp(-c), cum)
    bk = each(lambda x, y, g: jnp.concatenate([stack(bf(x * g)), stack(bf(y * g))], axis=0), b, k, g_inv)
    heads2 = lambda x: jnp.concatenate([bf(jnp.where(lane_a, x, 0.0)), bf(jnp.where(lane_a, 0.0, x))],
                                       axis=0)
    xa = each(lambda x, y: _dot_nt(heads2(x), y), at, bk)
    xr = each(lambda x, y: _dot_nt(heads2(x), y), rt, bk)
    n = each(lambda x: jnp.where(strict_bd, left(x), 0.0), xa)
    m = each(lambda x: bf(jnp.where(strict_bd, right(x), 0.0)), xa)
    p = each(lambda x: bf(jnp.where(incl_bd, left(x), 0.0)), xr)
    q = each(lambda x: bf(jnp.where(incl_bd, right(x), 0.0)), xr)
    vb = each(bf, v)
    v_st = each(stack, vb)
    mv = each(_dot, m, v_st)
    w = each(lambda x: eye + x, n)
    pw = each(bf, n)
    for _ in range(int(math.log2(L)) - 1):
        pw = each(lambda x: bf(_dot(x, x)), pw)
        w = each(lambda x, y: x + _dot(bf(x), y), w, pw)
    au = each(lambda ww, x, y: _dot(bf(ww), jnp.concatenate([stack(bf(x)), bf(y)], axis=1)),
              w, at, mv)
    pau = each(lambda x, y: _dot(x, bf(y)), p, au)
    qv = each(_dot, q, v_st)
    rbar = each(lambda x, y: bf(x + unstack(left(y))), rt, pau)
    ybar = each(lambda x, y: unstack(right(x) + y), pau, qv)
    bh = each(lambda x, t: stack(bf(x * t)), b, tail)
    kh = each(lambda x, t: bf(x * t), k, tail)
    abar = each(lambda x: bf(jnp.where(bd, left(x), 0.0)), au)
    ubar = each(lambda x: bf(jnp.where(bd, right(x), 0.0)), au)
    tt = each(lambda x, y: bf(jnp.where(bd, _dot_tn(x, y), 0.0)), abar, bh)
    z = each(lambda u, vv, x, y: jnp.where(bd, _dot_tn(jnp.concatenate([u, vv], axis=0),
                                                       jnp.concatenate([x, y], axis=0)), 0.0),
             ubar, vb, bh, kh)
    sb = each(bf, st)
    y = each(lambda x, s, yb: _dot_nt(x, s) + yb, rbar, sb, ybar)
    st_new = each(lambda s, c, s16, t, zz: s * jnp.exp(c) + _dot(s16, t) + zz, st, cl, sb, tt, z)
    return y, st_new


def _scan_kernel(r_ref, lw_ref, k_ref, v_ref, a_ref, b_ref, y_ref, st_ref, *, ts, L, pairs, hd):
    @pl.when(pl.program_id(2) == 0)
    def _():
        st_ref[...] = jnp.zeros_like(st_ref)

    ri = lax.broadcasted_iota(jnp.int32, (L, L), 0)
    ci = lax.broadcasted_iota(jnp.int32, (L, L), 1)
    lane = lax.broadcasted_iota(jnp.int32, (1, V7X_LANES), 1)
    r2 = lax.broadcasted_iota(jnp.int32, (2 * L, 2 * L), 0)
    c2 = lax.broadcasted_iota(jnp.int32, (2 * L, 2 * L), 1)
    bd = (r2 < L) == (c2 < L)
    rl, cl2 = r2 & (L - 1), c2 & (L - 1)
    consts = (jnp.where(ri >= ci, 1.0, 0.0).astype(BF16), bd & (rl > cl2), bd & (rl >= cl2),
              jnp.where(r2 == c2, 1.0, 0.0), lane < hd, bd)

    def body(c, carry):
        rows = pl.ds(pl.multiple_of(c * L, L), L)
        cols = [slice(p * V7X_LANES, (p + 1) * V7X_LANES) for p in range(pairs)]
        load = lambda ref: [ref[rows, cs] for cs in cols]
        ys, sts = _scan_chunk(load(r_ref), load(lw_ref), load(k_ref), load(v_ref), load(a_ref),
                              load(b_ref), [st_ref[p] for p in range(pairs)], consts)
        for p in range(pairs):
            y_ref[rows, cols[p]] = ys[p]
            st_ref[p] = sts[p]
        return carry

    lax.fori_loop(0, ts // L, body, 0)


def _rwkv_scan(cfg, r, lw, k, v, a, b):
    bsz, s, d = r.shape
    hd = cfg.rwkv_head_dim
    assert 2 * hd == V7X_LANES
    ts, L = min(cfg.scan_rows, s), cfg.scan_chunk
    npairs = d // V7X_LANES
    pairs = min(cfg.scan_pairs, npairs)
    assert s % ts == 0 and ts % L == 0 and npairs % pairs == 0
    blk = pl.BlockSpec((None, ts, pairs * V7X_LANES), lambda bb, p, i: (bb, i, p))
    return pl.pallas_call(
        functools.partial(_scan_kernel, ts=ts, L=L, pairs=pairs, hd=hd),
        grid=(bsz, npairs // pairs, s // ts),
        in_specs=[blk] * 6,
        out_specs=blk,
        out_shape=jax.ShapeDtypeStruct((bsz, s, d), F32),
        scratch_shapes=[pltpu.VMEM((pairs, V7X_LANES, V7X_LANES), F32)],
        compiler_params=_cparams(cfg, "parallel", "parallel", "arbitrary"),
        name="rwkv7_scan",
    )(r, lw, k, v, a, b)


def _rwkv_post_kernel(y_ref, r_ref, k_ref, v_ref, g_ref, rk_ref, lg_ref, lb_ref, o_ref, *, hd):
    gmat = _group_ones(V7X_LANES, hd)
    y = y_ref[...]
    mu = _group_sum(y, gmat) * (1.0 / hd)
    d = y - mu
    var = _group_sum(d * d, gmat) * (1.0 / hd)
    yn = d * lax.rsqrt(var + RWKV_LNX_EPS) * lg_ref[...] + lb_ref[...]
    bonus = _group_sum(r_ref[...] * k_ref[...] * rk_ref[...], gmat) * v_ref[...]
    o_ref[...] = ((yn + bonus) * g_ref[...]).astype(o_ref.dtype)


def _rwkv_post(cfg, y, r, k, v, g, r_k, lnx_g, lnx_b):
    bsz, s, d = y.shape
    ts = min(cfg.row_tile, s)
    row = pl.BlockSpec((None, ts, d), lambda b, i: (b, i, 0))
    vec = pl.BlockSpec((1, d), lambda b, i: (0, 0))
    return pl.pallas_call(
        functools.partial(_rwkv_post_kernel, hd=cfg.rwkv_head_dim),
        grid=(bsz, s // ts),
        in_specs=[row] * 5 + [vec] * 3,
        out_specs=row,
        out_shape=jax.ShapeDtypeStruct((bsz, s, d), BF16),
        compiler_params=_cparams(cfg, "parallel", "parallel"),
        name="rwkv_groupnorm_gate",
    )(y, r, k, v, g, r_k.reshape(1, d), lnx_g.reshape(1, d), lnx_b.reshape(1, d))


def _even_mixer(cfg, h, w_in, w_out):
    bsz, s, d = h.shape
    proj = _matmul(cfg, [h.reshape(bsz * s, d)], w_in, 0, cfg.even_in, F32,
                   cfg.mm_tm, cfg.mm_tn, single_buffer_x=True).reshape(bsz, s, cfg.even_in)
    o_m = _moba(cfg, proj).reshape(bsz * s, cfg.dm)
    o_r = _retention(cfg, proj).reshape(bsz * s, cfg.dv)
    return _matmul(cfg, [o_m, o_r], w_out, 0, d, BF16, cfg.mm_tm // 2,
                   cfg.mm_tn // 2).reshape(bsz, s, d)


def _odd_mixer(cfg, h, w_in, w_out, conv_w, conv_b, conv_ln_g, conv_ln_b, mu, w0, w_up, a0, a_up,
               g_up, k_k, k_a, r_k, lnx_g, lnx_b):
    bsz, s, d = h.shape
    h2 = h.reshape(bsz * s, d)
    w_nk = jnp.swapaxes(w_in, 1, 2)
    proj = _matmul(cfg, [h2], w_nk, 0, cfg.odd_main, F32,
                   cfg.mm_tm, cfg.mm_tn, w_is_nk=True, single_buffer_x=True).reshape(bsz, s, cfg.odd_main)
    lw_pad = V7X_LANES
    wl = w_nk[0, cfg.odd_main:]
    zr = lambda n: jnp.zeros((n, d), F32)
    o1, o2 = cfg.decay_lora, cfg.decay_lora + cfg.iclr_lora
    wl = jnp.concatenate([wl[:o1], zr(lw_pad - cfg.decay_lora), wl[o1:o2],
                          zr(lw_pad - cfg.iclr_lora), wl[o2:]], axis=0)[None]
    lo_w = wl.shape[1]
    lora = _matmul(cfg, [h2], wl, 0, lo_w, F32, cfg.mm_tm // 2, lo_w,
                   w_is_nk=True).reshape(bsz, s, lo_w)

    u = _ln_silu(cfg, _conv_glu(cfg, proj, conv_w, conv_b), conv_ln_g, conv_ln_b)
    r, lw, k, v, a, b, g = _rwkv_pre(cfg, proj, lora, mu, w0, w_up, a0, a_up, g_up, k_k, k_a)
    y = _rwkv_scan(cfg, r, lw, k, v, a, b)
    y = _rwkv_post(cfg, y, r, k, v, g, r_k.reshape(-1), lnx_g, lnx_b)
    return _matmul(cfg, [u.reshape(bsz * s, -1), y.reshape(bsz * s, -1)], w_out, 0, d, BF16,
                   cfg.mm_tm, cfg.mm_tn, single_buffer_x=True).reshape(bsz, s, d)


def _forward(cfg, x, c, w_ada, b_ada, norm_g, w_ffn_in, w_ffn_out, even_w_in, even_w_out, odd_w_in,
             odd_w_out, conv_w, conv_b, conv_ln_g, conv_ln_b, rwkv_mu, rwkv_w0, rwkv_w_up, rwkv_a0,
             rwkv_a_up, rwkv_g_up, rwkv_k_k, rwkv_k_a, rwkv_r_k, rwkv_lnx_g, rwkv_lnx_b):
    bsz, s, d = x.shape
    depth = w_ada.shape[0]
    mods = _modulation(cfg, c, w_ada, b_ada)
    sh_m, sc_m = mods[0, :, 0], mods[0, :, 1]
    h = _norm_mod(cfg, x, norm_g[0, 0], sc_m, sh_m)
    for layer in range(depth):
        g_m, sh_f, sc_f, g_f = (mods[layer, :, i] for i in (2, 3, 4, 5))
        j = layer // 2
        if layer % 2 == 0:
            o = _even_mixer(cfg, h, even_w_in[j:j + 1], even_w_out[j:j + 1])
        else:
            o = _odd_mixer(cfg, h, odd_w_in[j:j + 1], odd_w_out[j:j + 1], conv_w[j], conv_b[j],
                           conv_ln_g[j], conv_ln_b[j], rwkv_mu[j], rwkv_w0[j], rwkv_w_up[j],
                           rwkv_a0[j], rwkv_a_up[j], rwkv_g_up[j], rwkv_k_k[j], rwkv_k_a[j],
                           rwkv_r_k[j], rwkv_lnx_g[j], rwkv_lnx_b[j])
        x, h = _resid(cfg, x, o, norm_g[layer, 1], g_m, (norm_g[layer, 2], sc_f, sh_f))
        act = _ffn_in(cfg, h.reshape(bsz * s, d), w_ffn_in, layer)
        f = _matmul(cfg, [act], w_ffn_out, layer, d, BF16, cfg.ffn_out_tm, cfg.ffn_tn,
                    single_buffer_x=True).reshape(bsz, s, d)
        if layer + 1 < depth:
            nxt = (norm_g[layer + 1, 0], mods[layer + 1, :, 1], mods[layer + 1, :, 0])
            x, h = _resid(cfg, x, f, norm_g[layer, 3], g_f, nxt)
        else:
            x = _resid(cfg, x, f, norm_g[layer, 3], g_f)
    return x


def kernel(x, c, w_ada, b_ada, norm_g, w_ffn_in, w_ffn_out, even_w_in, even_w_out, odd_w_in, odd_w_out, conv_w, conv_b, conv_ln_g, conv_ln_b, rwkv_mu, rwkv_w0, rwkv_w_up, rwkv_a0, rwkv_a_up, rwkv_g_up, rwkv_k_k, rwkv_k_a, rwkv_r_k, rwkv_lnx_g, rwkv_lnx_b):
    return _forward(Config(), x, c, w_ada, b_ada, norm_g, w_ffn_in, w_ffn_out, even_w_in, even_w_out,
                    odd_w_in, odd_w_out, conv_w, conv_b, conv_ln_g, conv_ln_b, rwkv_mu, rwkv_w0,
                    rwkv_w_up, rwkv_a0, rwkv_a_up, rwkv_g_up, rwkv_k_k, rwkv_k_a, rwkv_r_k,
                    rwkv_lnx_g, rwkv_lnx_b)
```

```python
import dataclasses
import functools
import math

import jax
import jax.numpy as jnp
from jax import lax
from jax.experimental import pallas as pl
from jax.experimental.pallas import tpu as pltpu

F32 = jnp.float32
BF16 = jnp.bfloat16
HIGHEST = lax.Precision.HIGHEST

V7X_LANES = 128
V7X_SUBLANES = 8
V7X_VMEM_MIB = 64
MIB = 1024 * 1024
NORM_EPS = 1e-6
ROPE_THETA = 10000.0
CONV_LN_EPS = 1e-5
MOBA_ONES_ROWS = 16
RWKV_LNX_EPS = 64e-5


@dataclasses.dataclass(frozen=True)
class Config:
    d_model: int = 4096
    moba_heads: int = 16
    moba_head_dim: int = 128
    moba_block: int = 256
    moba_topk: int = 3
    ret_heads: int = 8
    ret_key_dim: int = 256
    ret_val_dim: int = 512
    ret_chunk: int = 128
    conv_ch: int = 2048
    conv_width: int = 31
    rwkv_dim: int = 2048
    rwkv_head_dim: int = 64
    decay_lora: int = 96
    iclr_lora: int = 96
    gate_lora: int = 256
    ffn_hidden: int = 11008
    row_tile: int = 256
    pre_rows: int = 128
    mm_tm: int = 2048
    ffn_out_tm: int = 1024
    mm_tn: int = 512
    ffn_tm: int = 2048
    ffn_tn: int = 256
    ret_rows: int = 1024
    ret_heads_per_step: int = 2
    conv_rows: int = 1024
    conv_cols: int = 256
    scan_rows: int = 256
    scan_chunk: int = 64
    scan_pairs: int = 16
    moba_group: int = 4
    moba_heads_per_step: int = 4
    vmem_mib: int = V7X_VMEM_MIB - 8

    @property
    def dm(self):
        return self.moba_heads * self.moba_head_dim

    @property
    def dk(self):
        return self.ret_heads * self.ret_key_dim

    @property
    def dv(self):
        return self.ret_heads * self.ret_val_dim

    @property
    def even_in(self):
        return 3 * self.dm + 2 * self.dk + 2 * self.dv

    @property
    def odd_main(self):
        return 2 * self.conv_ch + 3 * self.rwkv_dim


def _cparams(cfg, *sem):
    return pltpu.CompilerParams(dimension_semantics=sem, vmem_limit_bytes=cfg.vmem_mib * MIB)


def _silu(x):
    return x * jax.nn.sigmoid(x)


def _dot(a, b, **kw):
    return jnp.dot(a, b, preferred_element_type=F32, **kw)


def _dot_nt(a, b, **kw):
    return lax.dot_general(a, b, (((1,), (1,)), ((), ())), preferred_element_type=F32, **kw)


def _dot_tn(a, b, **kw):
    return lax.dot_general(a, b, (((0,), (0,)), ((), ())), preferred_element_type=F32, **kw)


def _ada_kernel(c_ref, w_ref, b_ref, o_ref):
    s = _silu(c_ref[...])
    hi = s.astype(BF16).astype(F32)
    parts = _dot(jnp.concatenate([hi, s - hi], axis=0).astype(BF16), w_ref[...].astype(BF16))
    o_ref[...] = parts[:V7X_SUBLANES] + parts[V7X_SUBLANES:] + b_ref[...]


def _modulation(cfg, c, w_ada, b_ada, tn=512):
    depth, d, n = w_ada.shape
    bsz = c.shape[0]
    cp = jnp.zeros((V7X_SUBLANES, d), F32).at[:bsz].set(c)
    out = pl.pallas_call(
        _ada_kernel,
        grid=(depth, n // tn),
        in_specs=[
            pl.BlockSpec((V7X_SUBLANES, d), lambda l, j: (0, 0)),
            pl.BlockSpec((None, d, tn), lambda l, j: (l, 0, j)),
            pl.BlockSpec((None, 1, tn), lambda l, j: (l, 0, j)),
        ],
        out_specs=pl.BlockSpec((None, V7X_SUBLANES, tn), lambda l, j: (l, 0, j)),
        out_shape=jax.ShapeDtypeStruct((depth, V7X_SUBLANES, n), F32),
        compiler_params=_cparams(cfg, "parallel", "parallel"),
        name="adaln_modulation",
    )(cp, w_ada, b_ada.reshape(depth, 1, n))
    return out[:, :bsz].reshape(depth, bsz, 6, 1, d)


def _rms(x, g):
    return x * lax.rsqrt(jnp.mean(x * x, axis=-1, keepdims=True) + NORM_EPS) * g


def _norm_mod_kernel(x_ref, g_ref, sc_ref, sh_ref, o_ref):
    y = _rms(x_ref[...], g_ref[...])
    o_ref[...] = (y * (1.0 + sc_ref[...]) + sh_ref[...]).astype(o_ref.dtype)


def _norm_mod(cfg, x, g, sc, sh):
    bsz, s, d = x.shape
    ts = min(cfg.row_tile, s)
    row = pl.BlockSpec((None, ts, d), lambda b, i: (b, i, 0))
    vec = pl.BlockSpec((1, d), lambda b, i: (0, 0))
    mod = pl.BlockSpec((None, 1, d), lambda b, i: (b, 0, 0))
    return pl.pallas_call(
        _norm_mod_kernel,
        grid=(bsz, s // ts),
        in_specs=[row, vec, mod, mod],
        out_specs=row,
        out_shape=jax.ShapeDtypeStruct((bsz, s, d), BF16),
        compiler_params=_cparams(cfg, "parallel", "parallel"),
        name="norm_modulate",
    )(x, g.reshape(1, d), sc, sh)


def _resid_kernel(x_ref, o_ref, ga_ref, gate_ref, *rest, with_h):
    xn = x_ref[...] + gate_ref[...] * _rms(o_ref[...].astype(F32), ga_ref[...])
    if with_h:
        gb_ref, sc_ref, sh_ref, xn_ref, h_ref = rest
        xn_ref[...] = xn
        h_ref[...] = (_rms(xn, gb_ref[...]) * (1.0 + sc_ref[...]) + sh_ref[...]).astype(h_ref.dtype)
    else:
        (xn_ref,) = rest
        xn_ref[...] = xn


def _resid(cfg, x, o, ga, gate, nxt=None):
    bsz, s, d = x.shape
    ts = min(cfg.row_tile, s)
    row = pl.BlockSpec((None, ts, d), lambda b, i: (b, i, 0))
    vec = pl.BlockSpec((1, d), lambda b, i: (0, 0))
    mod = pl.BlockSpec((None, 1, d), lambda b, i: (b, 0, 0))
    with_h = nxt is not None
    in_specs = [row, row, vec, mod]
    args = [x, o, ga.reshape(1, d), gate]
    out_specs = [row]
    out_shape = [jax.ShapeDtypeStruct((bsz, s, d), F32)]
    if with_h:
        gb, sc, sh = nxt
        in_specs += [vec, mod, mod]
        args += [gb.reshape(1, d), sc, sh]
        out_specs.append(row)
        out_shape.append(jax.ShapeDtypeStruct((bsz, s, d), BF16))
    outs = pl.pallas_call(
        functools.partial(_resid_kernel, with_h=with_h),
        grid=(bsz, s // ts),
        in_specs=in_specs,
        out_specs=out_specs,
        out_shape=out_shape,
        compiler_params=_cparams(cfg, "parallel", "parallel"),
        name="residual_norm",
    )(*args)
    return outs if with_h else outs[0]


def _mm_kernel(*refs, widths, w_is_nk):
    x_refs, (w_ref, o_ref) = refs[:len(widths)], refs[len(widths):]
    acc, off = None, 0
    for x_ref, wd in zip(x_refs, widths):
        if w_is_nk:
            part = _dot_nt(x_ref[...], w_ref[:, off:off + wd].astype(BF16))
        else:
            part = _dot(x_ref[...], w_ref[off:off + wd, :].astype(BF16))
        acc = part if acc is None else acc + part
        off += wd
    o_ref[...] = acc.astype(o_ref.dtype)


def _matmul(cfg, xs, w, layer, n, out_dtype, tm, tn, w_is_nk=False, single_buffer_x=False):
    m = xs[0].shape[0]
    widths = tuple(x.shape[1] for x in xs)
    kdim = sum(widths)
    tm, tn = min(tm, m), min(tn, n)
    assert m % tm == 0 and n % tn == 0 and w.shape[2 if w_is_nk else 1] == kdim
    wspec = (pl.BlockSpec((None, tn, kdim), lambda i, j: (layer, j, 0)) if w_is_nk
             else pl.BlockSpec((None, kdim, tn), lambda i, j: (layer, 0, j)))
    xmode = dict(pipeline_mode=pl.Buffered(1)) if single_buffer_x else {}
    return pl.pallas_call(
        functools.partial(_mm_kernel, widths=widths, w_is_nk=w_is_nk),
        grid=(m // tm, n // tn),
        in_specs=[pl.BlockSpec((tm, wd), lambda i, j: (i, 0), **xmode) for wd in widths] + [wspec],
        out_specs=pl.BlockSpec((tm, tn), lambda i, j: (i, j)),
        out_shape=jax.ShapeDtypeStruct((m, n), out_dtype),
        compiler_params=_cparams(cfg, "parallel", "parallel"),
        name="matmul",
    )(*xs, w)


def _ffn_in_kernel(x_ref, wg_ref, wu_ref, o_ref):
    x = x_ref[...]
    gate = _dot(x, wg_ref[...].astype(BF16))
    up = _dot(x, wu_ref[...].astype(BF16))
    o_ref[...] = (_silu(gate) * up).astype(o_ref.dtype)


def _ffn_in(cfg, x, w, layer):
    m, kdim = x.shape
    hid = w.shape[2] // 2
    tm, tn = min(cfg.ffn_tm, m), min(cfg.ffn_tn, hid)
    nt = hid // tn
    assert m % tm == 0 and hid % tn == 0
    return pl.pallas_call(
        _ffn_in_kernel,
        grid=(m // tm, nt),
        in_specs=[
            pl.BlockSpec((tm, kdim), lambda i, j: (i, 0)),
            pl.BlockSpec((None, kdim, tn), lambda i, j: (layer, 0, j)),
            pl.BlockSpec((None, kdim, tn), lambda i, j: (layer, 0, nt + j)),
        ],
        out_specs=pl.BlockSpec((tm, tn), lambda i, j: (i, j)),
        out_shape=jax.ShapeDtypeStruct((m, hid), BF16),
        compiler_params=_cparams(cfg, "parallel", "parallel"),
        name="ffn_in_swiglu",
    )(x, w, w)


def _moba_kernel(q_ref, k_ref, v_ref, cq_ref, sq_ref, ck_ref, sk_ref, o_ref,
                 kr_ref, vt_ref, km_ref, sel_ref, *, nb, blk, dh, topk, group, hp):
    qi = pl.program_id(2)
    half = dh // 2
    heads = range(hp)
    lanes = [slice(n * dh, (n + 1) * dh) for n in heads]
    each = lambda f, *ls: [f(*xs) for xs in zip(*ls)]
    colmax = lambda x: jnp.max(x, axis=0, keepdims=True)

    @pl.when(qi == 0)
    def _():
        for j in range(nb):
            rows = slice(j * blk, (j + 1) * blk)
            for n in heads:
                kb = k_ref[rows, lanes[n]]
                kr = kb * ck_ref[rows, :] + pltpu.roll(kb, half, 1) * sk_ref[rows, :]
                km_ref[n, j:j + 1, :] = jnp.mean(kr, axis=0, keepdims=True)
                kr_ref[n, j] = kr.astype(BF16)
                vt_ref[n, j, 0:dh, :] = v_ref[rows, lanes[n]].T.astype(BF16)
                vt_ref[n, j, dh:dh + MOBA_ONES_ROWS, :] = jnp.ones((MOBA_ONES_ROWS, blk), BF16)

    cq, sq = cq_ref[...], sq_ref[...]
    qr = [(lambda q: q * cq + pltpu.roll(q, half, 1) * sq)(q_ref[:, lanes[n]]) for n in heads]

    gate = [_dot_nt(km_ref[n], qr[n], precision=HIGHEST) for n in heads]
    brow = lax.broadcasted_iota(jnp.int32, (nb, blk), 0)
    browf = brow.astype(F32)
    gm = each(lambda g: jnp.where(brow < qi, g, -jnp.inf), gate)
    keep = [jnp.zeros((nb, blk), F32) for _ in heads]
    for _ in range(topk):
        mx = each(colmax, gm)
        first = each(lambda g, t: jnp.min(jnp.where(g == t, browf, float(nb)), axis=0, keepdims=True),
                     gm, mx)
        pick = each(lambda t, f: jnp.where(jnp.abs(t) < jnp.inf, f, -1.0), mx, first)
        keep = each(lambda kp, pk: jnp.where(browf == pk, 1.0, kp), keep, pick)
        gm = each(lambda g, pk: jnp.where(browf == pk, -jnp.inf, g), gm, pick)
    for j in range(nb):
        for n in heads:
            sel_ref[n, j] = jnp.broadcast_to(keep[n][j:j + 1, :], (V7X_SUBLANES, blk))

    qs = each(lambda x: (x * (dh ** -0.5 * math.log2(math.e))).T.astype(BF16), qr)
    kpos = lax.broadcasted_iota(jnp.int32, (blk, blk), 0)
    qpos = lax.broadcasted_iota(jnp.int32, (blk, blk), 1)
    s = [jnp.where(kpos <= qpos, _dot(kr_ref[n, qi], qs[n]), -jnp.inf) for n in heads]
    m = each(colmax, s)
    p = each(lambda x, y: jnp.exp2(x - y), s, m)
    acc = [_dot(vt_ref[n, qi], p[n].astype(BF16)) for n in heads]

    def past_blocks(width, first):
        def body(g, carry):
            m, acc = carry
            js = [first + g * width + u for u in range(width)]
            ss = [[jnp.where(sel_ref[n, j][0:1, :] > 0.0, _dot(kr_ref[n, j], qs[n]), -jnp.inf)
                   for n in heads] for j in js]
            m_new = list(m)
            for su in ss:
                m_new = each(lambda x, y: jnp.maximum(x, colmax(y)), m_new, su)
            acc = each(lambda x, y, a: jnp.exp2(x - y) * a, m, m_new, acc)
            for j, su in zip(js, ss):
                p = each(lambda x, y: jnp.exp2(x - y), su, m_new)
                acc = [acc[n] + _dot(vt_ref[n, j], p[n].astype(BF16)) for n in heads]
            return tuple(m_new), tuple(acc)
        return body

    whole = qi // group
    carry = lax.fori_loop(0, whole, past_blocks(group, 0), (tuple(m), tuple(acc)))
    m, acc = lax.fori_loop(0, qi - whole * group, past_blocks(1, whole * group), carry)
    for n in heads:
        o_ref[:, lanes[n]] = (acc[n][:dh] / acc[n][dh:dh + 1]).T.astype(o_ref.dtype)


def _rope_tables(seq, dim):
    inv = 1.0 / (ROPE_THETA ** (jnp.arange(0, dim, 2, dtype=F32) / dim))
    ang = jnp.arange(seq, dtype=F32)[:, None] * inv[None, :]
    return jnp.cos(ang), jnp.sin(ang)


def _moba(cfg, proj):
    bsz, s, _ = proj.shape
    h, dh, blk = cfg.moba_heads, cfg.moba_head_dim, cfg.moba_block
    assert dh == V7X_LANES and s % blk == 0
    nb = s // blk
    group = math.gcd(cfg.moba_group, nb)
    hp = math.gcd(cfg.moba_heads_per_step, h)
    hg = h // hp
    cos, sin = _rope_tables(s, dh)
    cosf = jnp.concatenate([cos, cos], axis=1)
    sinf = jnp.concatenate([-sin, sin], axis=1)
    qspec = pl.BlockSpec((None, blk, hp * dh), lambda b, hh, i: (b, i, hh))
    kspec = pl.BlockSpec((None, s, hp * dh), lambda b, hh, i: (b, 0, hg + hh))
    vspec = pl.BlockSpec((None, s, hp * dh), lambda b, hh, i: (b, 0, 2 * hg + hh))
    tq = pl.BlockSpec((blk, dh), lambda b, hh, i: (i, 0))
    tk = pl.BlockSpec((s, dh), lambda b, hh, i: (0, 0), pipeline_mode=pl.Buffered(1))
    return pl.pallas_call(
        functools.partial(_moba_kernel, nb=nb, blk=blk, dh=dh, topk=cfg.moba_topk, group=group,
                          hp=hp),
        grid=(bsz, hg, nb),
        in_specs=[qspec, kspec, vspec, tq, tq, tk, tk],
        out_specs=pl.BlockSpec((None, blk, hp * dh), lambda b, hh, i: (b, i, hh)),
        out_shape=jax.ShapeDtypeStruct((bsz, s, h * dh), BF16),
        scratch_shapes=[
            pltpu.VMEM((hp, nb, blk, dh), BF16),
            pltpu.VMEM((hp, nb, dh + MOBA_ONES_ROWS, blk), BF16),
            pltpu.VMEM((hp, nb, dh), F32),
            pltpu.VMEM((hp, nb, V7X_SUBLANES, blk), F32),
        ],
        compiler_params=_cparams(cfg, "parallel", "parallel", "arbitrary"),
        name="moba_attention",
    )(proj, proj, proj, cosf, sinf, cosf, sinf)


def _ret_kernel(q_ref, k_ref, v_ref, g_ref, cos_ref, sin_ref, dm_ref, qd_ref, kd_ref, cd_ref,
                o_ref, st_ref, *, c, nsub, dk, dv, hp):
    @pl.when(pl.program_id(2) == 0)
    def _():
        st_ref[...] = jnp.zeros_like(st_ref)

    half = dk // 2
    heads = range(hp)
    each = lambda f, *ls: [f(*xs) for xs in zip(*ls)]
    bf = lambda x: x.astype(BF16)
    dm, qd, kd, cd = ([ref[n] for n in heads] for ref in (dm_ref, qd_ref, kd_ref, cd_ref))
    for sidx in range(nsub):
        rows = slice(sidx * c, (sidx + 1) * c)
        cos = cos_ref[rows, :]
        sin = sin_ref[rows, :]

        def rope(x):
            x1, x2 = x[:, :half], x[:, half:]
            return jnp.concatenate([x1 * cos - x2 * sin, x2 * cos + x1 * sin], axis=1)

        q = [rope(q_ref[rows, n * dk:(n + 1) * dk]) for n in heads]
        k = [rope(k_ref[rows, n * dk:(n + 1) * dk]) * (dk ** -0.5) for n in heads]
        qb, kb = each(bf, q), each(bf, k)
        vb = [bf(v_ref[rows, n * dv:(n + 1) * dv]) for n in heads]
        st = [st_ref[n] for n in heads]
        inner = each(lambda x, y, d: bf(_dot_nt(x, y) * d), qb, kb, dm)
        cross = each(lambda x, s, d: _dot(x, bf(s)) * d, qb, st, qd)
        o = each(lambda i, v, x: _dot(i, v) + x, inner, vb, cross)
        kdb = each(lambda x, d: bf(x * d), k, kd)
        new_st = each(lambda s, d, x, v: s * d + _dot_tn(x, v), st, cd, kdb, vb)
        for n in heads:
            st_ref[n] = new_st[n]
        on = each(lambda x: x * lax.rsqrt(jnp.mean(x * x, axis=-1, keepdims=True) + NORM_EPS), o)
        for n in heads:
            cols = slice(n * dv, (n + 1) * dv)
            o_ref[rows, cols] = (on[n] * _silu(g_ref[rows, cols])).astype(o_ref.dtype)


def _retention(cfg, proj):
    bsz, s, _ = proj.shape
    h, dk, dv, c = cfg.ret_heads, cfg.ret_key_dim, cfg.ret_val_dim, cfg.ret_chunk
    ts = min(cfg.ret_rows, s)
    assert s % ts == 0 and ts % c == 0
    q0 = 3 * cfg.dm // dk
    k0 = (3 * cfg.dm + cfg.dk) // dk
    v0 = (3 * cfg.dm + 2 * cfg.dk) // dv
    g0 = (3 * cfg.dm + 2 * cfg.dk + cfg.dv) // dv
    assert (3 * cfg.dm) % dk == 0 and (3 * cfg.dm + 2 * cfg.dk) % dv == 0
    cos, sin = _rope_tables(s, dk)
    log_g = jnp.log1p(-jnp.exp2(-5.0 - jnp.arange(h, dtype=F32)))
    idx = jnp.arange(c, dtype=F32)
    diff = idx[:, None] - idx[None, :]
    dmask = jnp.where(diff >= 0, jnp.exp(jnp.maximum(diff, 0.0) * log_g[:, None, None]), 0.0)
    qdec = jnp.exp((idx + 1.0) * log_g[:, None])[..., None]
    kdec = jnp.exp((c - 1.0 - idx) * log_g[:, None])[..., None]
    cdec = jnp.broadcast_to(jnp.exp(c * log_g)[:, None, None], (h, 1, dv))
    hp = math.gcd(cfg.ret_heads_per_step, h)
    assert q0 % hp == 0 and k0 % hp == 0 and v0 % hp == 0 and g0 % hp == 0
    rowspec = lambda w, c0: pl.BlockSpec((None, ts, hp * w), lambda b, hh, i: (b, i, c0 // hp + hh))
    tab = pl.BlockSpec((ts, dk // 2), lambda b, hh, i: (i, 0))
    const = lambda r, w: pl.BlockSpec((hp, r, w), lambda b, hh, i: (hh, 0, 0))
    return pl.pallas_call(
        functools.partial(_ret_kernel, c=c, nsub=ts // c, dk=dk, dv=dv, hp=hp),
        grid=(bsz, h // hp, s // ts),
        in_specs=[
            rowspec(dk, q0), rowspec(dk, k0), rowspec(dv, v0), rowspec(dv, g0), tab, tab,
            const(c, c), const(c, 1), const(c, 1), const(1, dv),
        ],
        out_specs=pl.BlockSpec((None, ts, hp * dv), lambda b, hh, i: (b, i, hh)),
        out_shape=jax.ShapeDtypeStruct((bsz, s, h * dv), BF16),
        scratch_shapes=[pltpu.VMEM((hp, dk, dv), F32)],
        compiler_params=_cparams(cfg, "parallel", "parallel", "arbitrary"),
        name="retention",
    )(proj, proj, proj, proj, cos, sin, dmask, qdec, kdec, cdec)


CONV_HALO = 32


def _conv_kernel(a_ref, g_ref, w_ref, b_ref, o_ref, buf_ref, sh_ref, *, ts, kw, rc):
    @pl.when(pl.program_id(2) == 0)
    def _():
        buf_ref[0:CONV_HALO, :] = jnp.zeros((CONV_HALO, buf_ref.shape[1]), F32)

    buf_ref[CONV_HALO:CONV_HALO + ts, :] = a_ref[...] * jax.nn.sigmoid(g_ref[...])
    nshift = CONV_HALO + ts - V7X_SUBLANES
    for ph in range(1, V7X_SUBLANES):
        sh_ref[ph, 0:nshift, :] = buf_ref[ph:ph + nshift, :]
    off = CONV_HALO - (kw - 1)
    bias = b_ref[...]
    for r0 in range(0, ts, rc):
        acc = jnp.broadcast_to(bias, (rc, bias.shape[1]))
        for j in range(kw):
            ph = (off + j) % V7X_SUBLANES
            base = off + j - ph + r0
            rows = buf_ref[base:base + rc, :] if ph == 0 else sh_ref[ph, base:base + rc, :]
            acc = acc + w_ref[j:j + 1, :] * rows
        o_ref[r0:r0 + rc, :] = acc
    buf_ref[0:CONV_HALO, :] = buf_ref[ts:ts + CONV_HALO, :]


def _conv_glu(cfg, proj, conv_w, conv_b):
    bsz, s, _ = proj.shape
    ch, kw = cfg.conv_ch, cfg.conv_width
    ts, tc = min(cfg.conv_rows, s), min(cfg.conv_cols, ch)
    assert kw - 1 <= CONV_HALO <= ts and s % ts == 0 and ch % tc == 0
    nct = ch // tc
    wp = jnp.zeros((CONV_HALO, ch), F32).at[:kw].set(conv_w)
    return pl.pallas_call(
        functools.partial(_conv_kernel, ts=ts, kw=kw, rc=64),
        grid=(bsz, nct, s // ts),
        in_specs=[
            pl.BlockSpec((None, ts, tc), lambda b, c, i: (b, i, c)),
            pl.BlockSpec((None, ts, tc), lambda b, c, i: (b, i, nct + c)),
            pl.BlockSpec((CONV_HALO, tc), lambda b, c, i: (0, c)),
            pl.BlockSpec((1, tc), lambda b, c, i: (0, c)),
        ],
        out_specs=pl.BlockSpec((None, ts, tc), lambda b, c, i: (b, i, c)),
        out_shape=jax.ShapeDtypeStruct((bsz, s, ch), F32),
        scratch_shapes=[pltpu.VMEM((CONV_HALO + ts, tc), F32),
                        pltpu.VMEM((V7X_SUBLANES, CONV_HALO + ts, tc), F32)],
        compiler_params=_cparams(cfg, "parallel", "parallel", "arbitrary"),
        name="glu_causal_conv",
    )(proj, proj, wp, conv_b.reshape(1, ch))


def _ln_silu_kernel(x_ref, g_ref, b_ref, o_ref):
    x = x_ref[...]
    mu = jnp.mean(x, axis=-1, keepdims=True)
    d = x - mu
    var = jnp.mean(d * d, axis=-1, keepdims=True)
    y = d * lax.rsqrt(var + CONV_LN_EPS) * g_ref[...] + b_ref[...]
    o_ref[...] = _silu(y).astype(o_ref.dtype)


def _ln_silu(cfg, x, g, b):
    bsz, s, d = x.shape
    ts = min(cfg.row_tile, s)
    row = pl.BlockSpec((None, ts, d), lambda bb, i: (bb, i, 0))
    vec = pl.BlockSpec((1, d), lambda bb, i: (0, 0))
    return pl.pallas_call(
        _ln_silu_kernel,
        grid=(bsz, s // ts),
        in_specs=[row, vec, vec],
        out_specs=row,
        out_shape=jax.ShapeDtypeStruct((bsz, s, d), BF16),
        compiler_params=_cparams(cfg, "parallel", "parallel"),
        name="layernorm_swish",
    )(x, g.reshape(1, d), b.reshape(1, d))


def _group_ones(n, group):
    r = lax.broadcasted_iota(jnp.int32, (n, n), 0)
    c = lax.broadcasted_iota(jnp.int32, (n, n), 1)
    shift = int(math.log2(group))
    return jnp.where((r >> shift) == (c >> shift), 1.0, 0.0).astype(BF16)


def _split2(x):
    hi = x.astype(BF16)
    return hi, (x - hi.astype(F32)).astype(BF16)


def _dot_split(x, w):
    xh, xl = _split2(x)
    wh, wl = _split2(w)
    return _dot(xh, wh) + (_dot(xh, wl) + _dot(xl, wh))


def _group_sum(x, gmat):
    n = x.shape[1]
    hi, lo = _split2(x)
    parts = [_dot(hi[:, s0:s0 + V7X_LANES], gmat) + _dot(lo[:, s0:s0 + V7X_LANES], gmat)
             for s0 in range(0, n, V7X_LANES)]
    return parts[0] if len(parts) == 1 else jnp.concatenate(parts, axis=1)


def _rwkv_pre_kernel(r_ref, k_ref, v_ref, lo_ref, mur_ref, muk_ref, muv_ref, mul_ref,
                     w0_ref, wup_ref, a0_ref, aup_ref, gup_ref, kk_ref, ka_ref,
                     ro_ref, lw_ref, ko_ref, vo_ref, ao_ref, bo_ref, go_ref,
                     lr_ref, lk_ref, lv_ref, ll_ref, *, ts, hd, lw_pad):
    first = pl.program_id(1) == 0

    def shift(x_ref, last_ref, mu_ref):
        @pl.when(first)
        def _():
            last_ref[...] = jnp.zeros_like(last_ref)

        x = x_ref[...]
        row = lax.broadcasted_iota(jnp.int32, x.shape, 0)
        prev = jnp.where(row == 0, last_ref[V7X_SUBLANES - 1:V7X_SUBLANES, :], pltpu.roll(x, 1, 0))
        last_ref[...] = x[ts - V7X_SUBLANES:ts, :]
        return x + (prev - x) * mu_ref[...]

    r = shift(r_ref, lr_ref, mur_ref)
    k = shift(k_ref, lk_ref, muk_ref)
    v = shift(v_ref, lv_ref, muv_ref)
    lo = shift(lo_ref, ll_ref, mul_ref)
    xw, xa, xg = lo[:, :lw_pad], lo[:, lw_pad:2 * lw_pad], lo[:, 2 * lw_pad:]

    z = w0_ref[...] + _dot_split(jnp.tanh(xw), wup_ref[...])
    softplus = jnp.maximum(-z, 0.0) + jnp.log(1.0 + jnp.exp(-jnp.abs(z)))
    lw_ref[...] = -jnp.exp(-softplus - 0.5)
    a = jax.nn.sigmoid(a0_ref[...] + _dot_split(xa, aup_ref[...]))
    go_ref[...] = _dot(jax.nn.sigmoid(xg).astype(BF16), gup_ref[...].astype(BF16))

    kkr = k * kk_ref[...]
    ss = _group_sum(kkr * kkr, _group_ones(V7X_LANES, hd))
    kk = kkr * lax.rsqrt(jnp.maximum(ss, 1e-24))
    ro_ref[...] = r
    vo_ref[...] = v
    ko_ref[...] = k * (1.0 + (a - 1.0) * ka_ref[...])
    ao_ref[...] = -kk
    bo_ref[...] = kk * a


def _rwkv_pre(cfg, proj, lora, mu, w0, w_up, a0, a_up, g_up, k_k, k_a):
    bsz, s, _ = proj.shape
    d = cfg.rwkv_dim
    ts = min(cfg.pre_rows, s)
    lw_pad = V7X_LANES
    assert cfg.decay_lora <= lw_pad and cfg.iclr_lora <= lw_pad and (2 * cfg.conv_ch) % d == 0
    lo_w = lora.shape[2]
    c0 = 2 * cfg.conv_ch // d
    pad_rows = lambda w: jnp.zeros((lw_pad, d), F32).at[:w.shape[0]].set(w)
    pad_vec = lambda vv, n: jnp.zeros((1, n), F32).at[0, :vv.shape[0]].set(vv)
    mu_r, mu_k, mu_v = (mu[i * d:(i + 1) * d].reshape(1, d) for i in range(3))
    o = 3 * d
    mu_l = jnp.concatenate([
        pad_vec(mu[o:o + cfg.decay_lora], lw_pad),
        pad_vec(mu[o + cfg.decay_lora:o + cfg.decay_lora + cfg.iclr_lora], lw_pad),
        mu[o + cfg.decay_lora + cfg.iclr_lora:].reshape(1, -1)], axis=1)
    row = lambda cb: pl.BlockSpec((None, ts, d), lambda b, i: (b, i, cb))
    lrow = pl.BlockSpec((None, ts, lo_w), lambda b, i: (b, i, 0))
    vec = lambda n: pl.BlockSpec((1, n), lambda b, i: (0, 0))
    mat = lambda rws: pl.BlockSpec((rws, d), lambda b, i: (0, 0))
    orow = pl.BlockSpec((None, ts, d), lambda b, i: (b, i, 0))
    return pl.pallas_call(
        functools.partial(_rwkv_pre_kernel, ts=ts, hd=cfg.rwkv_head_dim, lw_pad=lw_pad),
        grid=(bsz, s // ts),
        in_specs=[row(c0), row(c0 + 1), row(c0 + 2), lrow, vec(d), vec(d), vec(d), vec(lo_w),
                  vec(d), mat(lw_pad), vec(d), mat(lw_pad), mat(cfg.gate_lora), vec(d), vec(d)],
        out_specs=[orow] * 7,
        out_shape=[jax.ShapeDtypeStruct((bsz, s, d), F32)] * 7,
        scratch_shapes=[pltpu.VMEM((V7X_SUBLANES, d), F32)] * 3 + [pltpu.VMEM((V7X_SUBLANES, lo_w), F32)],
        compiler_params=_cparams(cfg, "parallel", "arbitrary"),
        name="rwkv_token_shift_lora",
    )(proj, proj, proj, lora, mu_r, mu_k, mu_v, mu_l, w0.reshape(1, d), pad_rows(w_up),
      a0.reshape(1, d), pad_rows(a_up), g_up, k_k.reshape(1, d), k_a.reshape(1, d))


def _scan_chunk(r, lw, k, v, a, b, st, consts):
    tri, strict_bd, incl_bd, eye, lane_a, bd = consts
    L = r[0].shape[0]
    each = lambda f, *ls: [f(*xs) for xs in zip(*ls)]
    bf = lambda x: x.astype(BF16)
    stack = lambda x: jnp.concatenate([x, x], axis=0)
    unstack = lambda x: jnp.where(lane_a, x[:L], x[L:])
    left, right = (lambda x: x[:, :V7X_LANES]), (lambda x: x[:, V7X_LANES:])

    cs = each(lambda x: _dot(tri, jnp.concatenate(_split2(x), axis=1)), lw)
    cum = each(lambda x: left(x) + right(x), cs)
    cl = each(lambda x: x[L - 1:L, :], cum)
    tail = each(lambda x, y: jnp.exp(x - y), cl, cum)
    at = each(lambda x, c, w: x * jnp.exp(c - w), a, cum, lw)
    rt = each(lambda x, c: x * jnp.exp(c), r, cum)
    g_inv = each(lambda c: jnp.exp(-c), cum)
    bk = each(lambda x, y, g: jnp.concatenate([stack(bf(x * g)), stack(bf(y * g))], axis=0), b, k, g_inv)
    heads2 = lambda x: jnp.concatenate([bf(jnp.where(lane_a, x, 0.0)), bf(jnp.where(lane_a, 0.0, x))],
                                       axis=0)
    xa = each(lambda x, y: _dot_nt(heads2(x), y), at, bk)
    xr = each(lambda x, y: _dot_nt(heads2(x), y), rt, bk)
    n = each(lambda x: jnp.where(strict_bd, left(x), 0.0), xa)
    m = each(lambda x: bf(jnp.where(strict_bd, right(x), 0.0)), xa)
    p = each(lambda x: bf(jnp.where(incl_bd, left(x), 0.0)), xr)
    q = each(lambda x: bf(jnp.where(incl_bd, right(x), 0.0)), xr)
    vb = each(bf, v)
    v_st = each(stack, vb)
    mv = each(_dot, m, v_st)
    w = each(lambda x: eye + x, n)
    pw = each(bf, n)
    for _ in range(int(math.log2(L)) - 1):
        pw = each(lambda x: bf(_dot(x, x)), pw)
        w = each(lambda x, y: x + _dot(bf(x), y), w, pw)
    au = each(lambda ww, x, y: _dot(bf(ww), jnp.concatenate([stack(bf(x)), bf(y)], axis=1)),
              w, at, mv)
    pau = each(lambda x, y: _dot(x, bf(y)), p, au)
    qv = each(_dot, q, v_st)
    rbar = each(lambda x, y: bf(x + unstack(left(y))), rt, pau)
    ybar = each(lambda x, y: unstack(right(x) + y), pau, qv)
    bh = each(lambda x, t: stack(bf(x * t)), b, tail)
    kh = each(lambda x, t: bf(x * t), k, tail)
    abar = each(lambda x: bf(jnp.where(bd, left(x), 0.0)), au)
    ubar = each(lambda x: bf(jnp.where(bd, right(x), 0.0)), au)
    tt = each(lambda x, y: bf(jnp.where(bd, _dot_tn(x, y), 0.0)), abar, bh)
    z = each(lambda u, vv, x, y: jnp.where(bd, _dot_tn(jnp.concatenate([u, vv], axis=0),
                                                       jnp.concatenate([x, y], axis=0)), 0.0),
             ubar, vb, bh, kh)
    sb = each(bf, st)
    y = each(lambda x, s, yb: _dot_nt(x, s) + yb, rbar, sb, ybar)
    st_new = each(lambda s, c, s16, t, zz: s * jnp.exp(c) + _dot(s16, t) + zz, st, cl, sb, tt, z)
    return y, st_new


def _scan_kernel(r_ref, lw_ref, k_ref, v_ref, a_ref, b_ref, y_ref, st_ref, *, ts, L, pairs, hd):
    @pl.when(pl.program_id(2) == 0)
    def _():
        st_ref[...] = jnp.zeros_like(st_ref)

    ri = lax.broadcasted_iota(jnp.int32, (L, L), 0)
    ci = lax.broadcasted_iota(jnp.int32, (L, L), 1)
    lane = lax.broadcasted_iota(jnp.int32, (1, V7X_LANES), 1)
    r2 = lax.broadcasted_iota(jnp.int32, (2 * L, 2 * L), 0)
    c2 = lax.broadcasted_iota(jnp.int32, (2 * L, 2 * L), 1)
    bd = (r2 < L) == (c2 < L)
    rl, cl2 = r2 & (L - 1), c2 & (L - 1)
    consts = (jnp.where(ri >= ci, 1.0, 0.0).astype(BF16), bd & (rl > cl2), bd & (rl >= cl2),
              jnp.where(r2 == c2, 1.0, 0.0), lane < hd, bd)

    def body(c, carry):
        rows = pl.ds(pl.multiple_of(c * L, L), L)
        cols = [slice(p * V7X_LANES, (p + 1) * V7X_LANES) for p in range(pairs)]
        load = lambda ref: [ref[rows, cs] for cs in cols]
        ys, sts = _scan_chunk(load(r_ref), load(lw_ref), load(k_ref), load(v_ref), load(a_ref),
                              load(b_ref), [st_ref[p] for p in range(pairs)], consts)
        for p in range(pairs):
            y_ref[rows, cols[p]] = ys[p]
            st_ref[p] = sts[p]
        return carry

    lax.fori_loop(0, ts // L, body, 0)


def _rwkv_scan(cfg, r, lw, k, v, a, b):
    bsz, s, d = r.shape
    hd = cfg.rwkv_head_dim
    assert 2 * hd == V7X_LANES
    ts, L = min(cfg.scan_rows, s), cfg.scan_chunk
    npairs = d // V7X_LANES
    pairs = min(cfg.scan_pairs, npairs)
    assert s % ts == 0 and ts % L == 0 and npairs % pairs == 0
    blk = pl.BlockSpec((None, ts, pairs * V7X_LANES), lambda bb, p, i: (bb, i, p))
    return pl.pallas_call(
        functools.partial(_scan_kernel, ts=ts, L=L, pairs=pairs, hd=hd),
        grid=(bsz, npairs // pairs, s // ts),
        in_specs=[blk] * 6,
        out_specs=blk,
        out_shape=jax.ShapeDtypeStruct((bsz, s, d), F32),
        scratch_shapes=[pltpu.VMEM((pairs, V7X_LANES, V7X_LANES), F32)],
        compiler_params=_cparams(cfg, "parallel", "parallel", "arbitrary"),
        name="rwkv7_scan",
    )(r, lw, k, v, a, b)


def _rwkv_post_kernel(y_ref, r_ref, k_ref, v_ref, g_ref, rk_ref, lg_ref, lb_ref, o_ref, *, hd):
    gmat = _group_ones(V7X_LANES, hd)
    y = y_ref[...]
    mu = _group_sum(y, gmat) * (1.0 / hd)
    d = y - mu
    var = _group_sum(d * d, gmat) * (1.0 / hd)
    yn = d * lax.rsqrt(var + RWKV_LNX_EPS) * lg_ref[...] + lb_ref[...]
    bonus = _group_sum(r_ref[...] * k_ref[...] * rk_ref[...], gmat) * v_ref[...]
    o_ref[...] = ((yn + bonus) * g_ref[...]).astype(o_ref.dtype)


def _rwkv_post(cfg, y, r, k, v, g, r_k, lnx_g, lnx_b):
    bsz, s, d = y.shape
    ts = min(cfg.row_tile, s)
    row = pl.BlockSpec((None, ts, d), lambda b, i: (b, i, 0))
    vec = pl.BlockSpec((1, d), lambda b, i: (0, 0))
    return pl.pallas_call(
        functools.partial(_rwkv_post_kernel, hd=cfg.rwkv_head_dim),
        grid=(bsz, s // ts),
        in_specs=[row] * 5 + [vec] * 3,
        out_specs=row,
        out_shape=jax.ShapeDtypeStruct((bsz, s, d), BF16),
        compiler_params=_cparams(cfg, "parallel", "parallel"),
        name="rwkv_groupnorm_gate",
    )(y, r, k, v, g, r_k.reshape(1, d), lnx_g.reshape(1, d), lnx_b.reshape(1, d))


def _even_mixer(cfg, h, w_in, w_out):
    bsz, s, d = h.shape
    proj = _matmul(cfg, [h.reshape(bsz * s, d)], w_in, 0, cfg.even_in, F32,
                   cfg.mm_tm, cfg.mm_tn, single_buffer_x=True).reshape(bsz, s, cfg.even_in)
    o_m = _moba(cfg, proj).reshape(bsz * s, cfg.dm)
    o_r = _retention(cfg, proj).reshape(bsz * s, cfg.dv)
    return _matmul(cfg, [o_m, o_r], w_out, 0, d, BF16, cfg.mm_tm // 2,
                   cfg.mm_tn // 2).reshape(bsz, s, d)


def _odd_mixer(cfg, h, w_in, w_out, conv_w, conv_b, conv_ln_g, conv_ln_b, mu, w0, w_up, a0, a_up,
               g_up, k_k, k_a, r_k, lnx_g, lnx_b):
    bsz, s, d = h.shape
    h2 = h.reshape(bsz * s, d)
    w_nk = jnp.swapaxes(w_in, 1, 2)
    proj = _matmul(cfg, [h2], w_nk, 0, cfg.odd_main, F32,
                   cfg.mm_tm, cfg.mm_tn, w_is_nk=True, single_buffer_x=True).reshape(bsz, s, cfg.odd_main)
    lw_pad = V7X_LANES
    wl = w_nk[0, cfg.odd_main:]
    zr = lambda n: jnp.zeros((n, d), F32)
    o1, o2 = cfg.decay_lora, cfg.decay_lora + cfg.iclr_lora
    wl = jnp.concatenate([wl[:o1], zr(lw_pad - cfg.decay_lora), wl[o1:o2],
                          zr(lw_pad - cfg.iclr_lora), wl[o2:]], axis=0)[None]
    lo_w = wl.shape[1]
    lora = _matmul(cfg, [h2], wl, 0, lo_w, F32, cfg.mm_tm // 2, lo_w,
                   w_is_nk=True).reshape(bsz, s, lo_w)

    u = _ln_silu(cfg, _conv_glu(cfg, proj, conv_w, conv_b), conv_ln_g, conv_ln_b)
    r, lw, k, v, a, b, g = _rwkv_pre(cfg, proj, lora, mu, w0, w_up, a0, a_up, g_up, k_k, k_a)
    y = _rwkv_scan(cfg, r, lw, k, v, a, b)
    y = _rwkv_post(cfg, y, r, k, v, g, r_k.reshape(-1), lnx_g, lnx_b)
    return _matmul(cfg, [u.reshape(bsz * s, -1), y.reshape(bsz * s, -1)], w_out, 0, d, BF16,
                   cfg.mm_tm, cfg.mm_tn, single_buffer_x=True).reshape(bsz, s, d)


def _forward(cfg, x, c, w_ada, b_ada, norm_g, w_ffn_in, w_ffn_out, even_w_in, even_w_out, odd_w_in,
             odd_w_out, conv_w, conv_b, conv_ln_g, conv_ln_b, rwkv_mu, rwkv_w0, rwkv_w_up, rwkv_a0,
             rwkv_a_up, rwkv_g_up, rwkv_k_k, rwkv_k_a, rwkv_r_k, rwkv_lnx_g, rwkv_lnx_b):
    bsz, s, d = x.shape
    depth = w_ada.shape[0]
    mods = _modulation(cfg, c, w_ada, b_ada)
    sh_m, sc_m = mods[0, :, 0], mods[0, :, 1]
    h = _norm_mod(cfg, x, norm_g[0, 0], sc_m, sh_m)
    for layer in range(depth):
        g_m, sh_f, sc_f, g_f = (mods[layer, :, i] for i in (2, 3, 4, 5))
        j = layer // 2
        if layer % 2 == 0:
            o = _even_mixer(cfg, h, even_w_in[j:j + 1], even_w_out[j:j + 1])
        else:
            o = _odd_mixer(cfg, h, odd_w_in[j:j + 1], odd_w_out[j:j + 1], conv_w[j], conv_b[j],
                           conv_ln_g[j], conv_ln_b[j], rwkv_mu[j], rwkv_w0[j], rwkv_w_up[j],
                           rwkv_a0[j], rwkv_a_up[j], rwkv_g_up[j], rwkv_k_k[j], rwkv_k_a[j],
                           rwkv_r_k[j], rwkv_lnx_g[j], rwkv_lnx_b[j])
        x, h = _resid(cfg, x, o, norm_g[layer, 1], g_m, (norm_g[layer, 2], sc_f, sh_f))
        act = _ffn_in(cfg, h.reshape(bsz * s, d), w_ffn_in, layer)
        f = _matmul(cfg, [act], w_ffn_out, layer, d, BF16, cfg.ffn_out_tm, cfg.ffn_tn,
                    single_buffer_x=True).reshape(bsz, s, d)
        if layer + 1 < depth:
            nxt = (norm_g[layer + 1, 0], mods[layer + 1, :, 1], mods[layer + 1, :, 0])
            x, h = _resid(cfg, x, f, norm_g[layer, 3], g_f, nxt)
        else:
            x = _resid(cfg, x, f, norm_g[layer, 3], g_f)
    return x


def kernel(x, c, w_ada, b_ada, norm_g, w_ffn_in, w_ffn_out, even_w_in, even_w_out, odd_w_in, odd_w_out, conv_w, conv_b, conv_ln_g, conv_ln_b, rwkv_mu, rwkv_w0, rwkv_w_up, rwkv_a0, rwkv_a_up, rwkv_g_up, rwkv_k_k, rwkv_k_a, rwkv_r_k, rwkv_lnx_g, rwkv_lnx_b):
    return _forward(Config(), x, c, w_ada, b_ada, norm_g, w_ffn_in, w_ffn_out, even_w_in, even_w_out,
                    odd_w_in, odd_w_out, conv_w, conv_b, conv_ln_g, conv_ln_b, rwkv_mu, rwkv_w0,
                    rwkv_w_up, rwkv_a0, rwkv_a_up, rwkv_g_up, rwkv_k_k, rwkv_k_a, rwkv_r_k,
                    rwkv_lnx_g, rwkv_lnx_b)
```

```python
import dataclasses
import functools
import math

import jax
import jax.numpy as jnp
from jax import lax
from jax.experimental import pallas as pl
from jax.experimental.pallas import tpu as pltpu

F32 = jnp.float32
BF16 = jnp.bfloat16
HIGHEST = lax.Precision.HIGHEST

V7X_LANES = 128
V7X_SUBLANES = 8
V7X_VMEM_MIB = 64
MIB = 1024 * 1024
NORM_EPS = 1e-6
ROPE_THETA = 10000.0
CONV_LN_EPS = 1e-5
MOBA_ONES_ROWS = 16
RWKV_LNX_EPS = 64e-5


@dataclasses.dataclass(frozen=True)
class Config:
    d_model: int = 4096
    moba_heads: int = 16
    moba_head_dim: int = 128
    moba_block: int = 256
    moba_topk: int = 3
    ret_heads: int = 8
    ret_key_dim: int = 256
    ret_val_dim: int = 512
    ret_chunk: int = 128
    conv_ch: int = 2048
    conv_width: int = 31
    rwkv_dim: int = 2048
    rwkv_head_dim: int = 64
    decay_lora: int = 96
    iclr_lora: int = 96
    gate_lora: int = 256
    ffn_hidden: int = 11008
    row_tile: int = 256
    pre_rows: int = 128
    mm_tm: int = 2048
    ffn_out_tm: int = 1024
    mm_tn: int = 512
    ffn_tm: int = 2048
    ffn_tn: int = 256
    ret_rows: int = 1024
    ret_heads_per_step: int = 2
    conv_rows: int = 1024
    conv_cols: int = 256
    scan_rows: int = 256
    scan_chunk: int = 64
    scan_pairs: int = 16
    moba_group: int = 4
    moba_heads_per_step: int = 4
    vmem_mib: int = V7X_VMEM_MIB - 8

    @property
    def dm(self):
        return self.moba_heads * self.moba_head_dim

    @property
    def dk(self):
        return self.ret_heads * self.ret_key_dim

    @property
    def dv(self):
        return self.ret_heads * self.ret_val_dim

    @property
    def even_in(self):
        return 3 * self.dm + 2 * self.dk + 2 * self.dv

    @property
    def odd_main(self):
        return 2 * self.conv_ch + 3 * self.rwkv_dim


def _cparams(cfg, *sem):
    return pltpu.CompilerParams(dimension_semantics=sem, vmem_limit_bytes=cfg.vmem_mib * MIB)


def _silu(x):
    return x * jax.nn.sigmoid(x)


def _dot(a, b, **kw):
    return jnp.dot(a, b, preferred_element_type=F32, **kw)


def _dot_nt(a, b, **kw):
    return lax.dot_general(a, b, (((1,), (1,)), ((), ())), preferred_element_type=F32, **kw)


def _dot_tn(a, b, **kw):
    return lax.dot_general(a, b, (((0,), (0,)), ((), ())), preferred_element_type=F32, **kw)


def _ada_kernel(c_ref, w_ref, b_ref, o_ref):
    s = _silu(c_ref[...])
    hi = s.astype(BF16).astype(F32)
    parts = _dot(jnp.concatenate([hi, s - hi], axis=0).astype(BF16), w_ref[...].astype(BF16))
    o_ref[...] = parts[:V7X_SUBLANES] + parts[V7X_SUBLANES:] + b_ref[...]


def _modulation(cfg, c, w_ada, b_ada, tn=512):
    depth, d, n = w_ada.shape
    bsz = c.shape[0]
    cp = jnp.zeros((V7X_SUBLANES, d), F32).at[:bsz].set(c)
    out = pl.pallas_call(
        _ada_kernel,
        grid=(depth, n // tn),
        in_specs=[
            pl.BlockSpec((V7X_SUBLANES, d), lambda l, j: (0, 0)),
            pl.BlockSpec((None, d, tn), lambda l, j: (l, 0, j)),
            pl.BlockSpec((None, 1, tn), lambda l, j: (l, 0, j)),
        ],
        out_specs=pl.BlockSpec((None, V7X_SUBLANES, tn), lambda l, j: (l, 0, j)),
        out_shape=jax.ShapeDtypeStruct((depth, V7X_SUBLANES, n), F32),
        compiler_params=_cparams(cfg, "parallel", "parallel"),
        name="adaln_modulation",
    )(cp, w_ada, b_ada.reshape(depth, 1, n))
    return out[:, :bsz].reshape(depth, bsz, 6, 1, d)


def _rms(x, g):
    return x * lax.rsqrt(jnp.mean(x * x, axis=-1, keepdims=True) + NORM_EPS) * g


def _norm_mod_kernel(x_ref, g_ref, sc_ref, sh_ref, o_ref):
    y = _rms(x_ref[...], g_ref[...])
    o_ref[...] = (y * (1.0 + sc_ref[...]) + sh_ref[...]).astype(o_ref.dtype)


def _norm_mod(cfg, x, g, sc, sh):
    bsz, s, d = x.shape
    ts = min(cfg.row_tile, s)
    row = pl.BlockSpec((None, ts, d), lambda b, i: (b, i, 0))
    vec = pl.BlockSpec((1, d), lambda b, i: (0, 0))
    mod = pl.BlockSpec((None, 1, d), lambda b, i: (b, 0, 0))
    return pl.pallas_call(
        _norm_mod_kernel,
        grid=(bsz, s // ts),
        in_specs=[row, vec, mod, mod],
        out_specs=row,
        out_shape=jax.ShapeDtypeStruct((bsz, s, d), BF16),
        compiler_params=_cparams(cfg, "parallel", "parallel"),
        name="norm_modulate",
    )(x, g.reshape(1, d), sc, sh)


def _resid_kernel(x_ref, o_ref, ga_ref, gate_ref, *rest, with_h):
    xn = x_ref[...] + gate_ref[...] * _rms(o_ref[...].astype(F32), ga_ref[...])
    if with_h:
        gb_ref, sc_ref, sh_ref, xn_ref, h_ref = rest
        xn_ref[...] = xn
        h_ref[...] = (_rms(xn, gb_ref[...]) * (1.0 + sc_ref[...]) + sh_ref[...]).astype(h_ref.dtype)
    else:
        (xn_ref,) = rest
        xn_ref[...] = xn


def _resid(cfg, x, o, ga, gate, nxt=None):
    bsz, s, d = x.shape
    ts = min(cfg.row_tile, s)
    row = pl.BlockSpec((None, ts, d), lambda b, i: (b, i, 0))
    vec = pl.BlockSpec((1, d), lambda b, i: (0, 0))
    mod = pl.BlockSpec((None, 1, d), lambda b, i: (b, 0, 0))
    with_h = nxt is not None
    in_specs = [row, row, vec, mod]
    args = [x, o, ga.reshape(1, d), gate]
    out_specs = [row]
    out_shape = [jax.ShapeDtypeStruct((bsz, s, d), F32)]
    if with_h:
        gb, sc, sh = nxt
        in_specs += [vec, mod, mod]
        args += [gb.reshape(1, d), sc, sh]
        out_specs.append(row)
        out_shape.append(jax.ShapeDtypeStruct((bsz, s, d), BF16))
    outs = pl.pallas_call(
        functools.partial(_resid_kernel, with_h=with_h),
        grid=(bsz, s // ts),
        in_specs=in_specs,
        out_specs=out_specs,
        out_shape=out_shape,
        compiler_params=_cparams(cfg, "parallel", "parallel"),
        name="residual_norm",
    )(*args)
    return outs if with_h else outs[0]


def _mm_kernel(*refs, widths, w_is_nk):
    x_refs, (w_ref, o_ref) = refs[:len(widths)], refs[len(widths):]
    acc, off = None, 0
    for x_ref, wd in zip(x_refs, widths):
        if w_is_nk:
            part = _dot_nt(x_ref[...], w_ref[:, off:off + wd].astype(BF16))
        else:
            part = _dot(x_ref[...], w_ref[off:off + wd, :].astype(BF16))
        acc = part if acc is None else acc + part
        off += wd
    o_ref[...] = acc.astype(o_ref.dtype)


def _matmul(cfg, xs, w, layer, n, out_dtype, tm, tn, w_is_nk=False, single_buffer_x=False):
    m = xs[0].shape[0]
    widths = tuple(x.shape[1] for x in xs)
    kdim = sum(widths)
    tm, tn = min(tm, m), min(tn, n)
    assert m % tm == 0 and n % tn == 0 and w.shape[2 if w_is_nk else 1] == kdim
    wspec = (pl.BlockSpec((None, tn, kdim), lambda i, j: (layer, j, 0)) if w_is_nk
             else pl.BlockSpec((None, kdim, tn), lambda i, j: (layer, 0, j)))
    xmode = dict(pipeline_mode=pl.Buffered(1)) if single_buffer_x else {}
    return pl.pallas_call(
        functools.partial(_mm_kernel, widths=widths, w_is_nk=w_is_nk),
        grid=(m // tm, n // tn),
        in_specs=[pl.BlockSpec((tm, wd), lambda i, j: (i, 0), **xmode) for wd in widths] + [wspec],
        out_specs=pl.BlockSpec((tm, tn), lambda i, j: (i, j)),
        out_shape=jax.ShapeDtypeStruct((m, n), out_dtype),
        compiler_params=_cparams(cfg, "parallel", "parallel"),
        name="matmul",
    )(*xs, w)


def _ffn_in_kernel(x_ref, wg_ref, wu_ref, o_ref):
    x = x_ref[...]
    gate = _dot(x, wg_ref[...].astype(BF16))
    up = _dot(x, wu_ref[...].astype(BF16))
    o_ref[...] = (_silu(gate) * up).astype(o_ref.dtype)


def _ffn_in(cfg, x, w, layer):
    m, kdim = x.shape
    hid = w.shape[2] // 2
    tm, tn = min(cfg.ffn_tm, m), min(cfg.ffn_tn, hid)
    nt = hid // tn
    assert m % tm == 0 and hid % tn == 0
    return pl.pallas_call(
        _ffn_in_kernel,
        grid=(m // tm, nt),
        in_specs=[
            pl.BlockSpec((tm, kdim), lambda i, j: (i, 0)),
            pl.BlockSpec((None, kdim, tn), lambda i, j: (layer, 0, j)),
            pl.BlockSpec((None, kdim, tn), lambda i, j: (layer, 0, nt + j)),
        ],
        out_specs=pl.BlockSpec((tm, tn), lambda i, j: (i, j)),
        out_shape=jax.ShapeDtypeStruct((m, hid), BF16),
        compiler_params=_cparams(cfg, "parallel", "parallel"),
        name="ffn_in_swiglu",
    )(x, w, w)


def _moba_kernel(q_ref, k_ref, v_ref, cq_ref, sq_ref, ck_ref, sk_ref, o_ref,
                 kr_ref, vt_ref, km_ref, sel_ref, *, nb, blk, dh, topk, group, hp):
    qi = pl.program_id(2)
    half = dh // 2
    heads = range(hp)
    lanes = [slice(n * dh, (n + 1) * dh) for n in heads]
    each = lambda f, *ls: [f(*xs) for xs in zip(*ls)]
    colmax = lambda x: jnp.max(x, axis=0, keepdims=True)

    @pl.when(qi == 0)
    def _():
        for j in range(nb):
            rows = slice(j * blk, (j + 1) * blk)
            for n in heads:
                kb = k_ref[rows, lanes[n]]
                kr = kb * ck_ref[rows, :] + pltpu.roll(kb, half, 1) * sk_ref[rows, :]
                km_ref[n, j:j + 1, :] = jnp.mean(kr, axis=0, keepdims=True)
                kr_ref[n, j] = kr.astype(BF16)
                vt_ref[n, j, 0:dh, :] = v_ref[rows, lanes[n]].T.astype(BF16)
                vt_ref[n, j, dh:dh + MOBA_ONES_ROWS, :] = jnp.ones((MOBA_ONES_ROWS, blk), BF16)

    cq, sq = cq_ref[...], sq_ref[...]
    qr = [(lambda q: q * cq + pltpu.roll(q, half, 1) * sq)(q_ref[:, lanes[n]]) for n in heads]

    gate = [_dot_nt(km_ref[n], qr[n], precision=HIGHEST) for n in heads]
    brow = lax.broadcasted_iota(jnp.int32, (nb, blk), 0)
    browf = brow.astype(F32)
    gm = each(lambda g: jnp.where(brow < qi, g, -jnp.inf), gate)
    keep = [jnp.zeros((nb, blk), F32) for _ in heads]
    for _ in range(topk):
        mx = each(colmax, gm)
        first = each(lambda g, t: jnp.min(jnp.where(g == t, browf, float(nb)), axis=0, keepdims=True),
                     gm, mx)
        pick = each(lambda t, f: jnp.where(jnp.abs(t) < jnp.inf, f, -1.0), mx, first)
        keep = each(lambda kp, pk: jnp.where(browf == pk, 1.0, kp), keep, pick)
        gm = each(lambda g, pk: jnp.where(browf == pk, -jnp.inf, g), gm, pick)
    for j in range(nb):
        for n in heads:
            sel_ref[n, j] = jnp.broadcast_to(keep[n][j:j + 1, :], (V7X_SUBLANES, blk))

    qs = each(lambda x: (x * (dh ** -0.5 * math.log2(math.e))).T.astype(BF16), qr)
    kpos = lax.broadcasted_iota(jnp.int32, (blk, blk), 0)
    qpos = lax.broadcasted_iota(jnp.int32, (blk, blk), 1)
    s = [jnp.where(kpos <= qpos, _dot(kr_ref[n, qi], qs[n]), -jnp.inf) for n in heads]
    m = each(colmax, s)
    p = each(lambda x, y: jnp.exp2(x - y), s, m)
    acc = [_dot(vt_ref[n, qi], p[n].astype(BF16)) for n in heads]

    def past_blocks(width, first):
        def body(g, carry):
            m, acc = carry
            js = [first + g * width + u for u in range(width)]
            ss = [[jnp.where(sel_ref[n, j][0:1, :] > 0.0, _dot(kr_ref[n, j], qs[n]), -jnp.inf)
                   for n in heads] for j in js]
            m_new = list(m)
            for su in ss:
                m_new = each(lambda x, y: jnp.maximum(x, colmax(y)), m_new, su)
            acc = each(lambda x, y, a: jnp.exp2(x - y) * a, m, m_new, acc)
            for j, su in zip(js, ss):
                p = each(lambda x, y: jnp.exp2(x - y), su, m_new)
                acc = [acc[n] + _dot(vt_ref[n, j], p[n].astype(BF16)) for n in heads]
            return tuple(m_new), tuple(acc)
        return body

    whole = qi // group
    carry = lax.fori_loop(0, whole, past_blocks(group, 0), (tuple(m), tuple(acc)))
    m, acc = lax.fori_loop(0, qi - whole * group, past_blocks(1, whole * group), carry)
    for n in heads:
        o_ref[:, lanes[n]] = (acc[n][:dh] / acc[n][dh:dh + 1]).T.astype(o_ref.dtype)


def _rope_tables(seq, dim):
    inv = 1.0 / (ROPE_THETA ** (jnp.arange(0, dim, 2, dtype=F32) / dim))
    ang = jnp.arange(seq, dtype=F32)[:, None] * inv[None, :]
    return jnp.cos(ang), jnp.sin(ang)


def _moba(cfg, proj):
    bsz, s, _ = proj.shape
    h, dh, blk = cfg.moba_heads, cfg.moba_head_dim, cfg.moba_block
    assert dh == V7X_LANES and s % blk == 0
    nb = s // blk
    group = math.gcd(cfg.moba_group, nb)
    hp = math.gcd(cfg.moba_heads_per_step, h)
    hg = h // hp
    cos, sin = _rope_tables(s, dh)
    cosf = jnp.concatenate([cos, cos], axis=1)
    sinf = jnp.concatenate([-sin, sin], axis=1)
    qspec = pl.BlockSpec((None, blk, hp * dh), lambda b, hh, i: (b, i, hh))
    kspec = pl.BlockSpec((None, s, hp * dh), lambda b, hh, i: (b, 0, hg + hh))
    vspec = pl.BlockSpec((None, s, hp * dh), lambda b, hh, i: (b, 0, 2 * hg + hh))
    tq = pl.BlockSpec((blk, dh), lambda b, hh, i: (i, 0))
    tk = pl.BlockSpec((s, dh), lambda b, hh, i: (0, 0), pipeline_mode=pl.Buffered(1))
    return pl.pallas_call(
        functools.partial(_moba_kernel, nb=nb, blk=blk, dh=dh, topk=cfg.moba_topk, group=group,
                          hp=hp),
        grid=(bsz, hg, nb),
        in_specs=[qspec, kspec, vspec, tq, tq, tk, tk],
        out_specs=pl.BlockSpec((None, blk, hp * dh), lambda b, hh, i: (b, i, hh)),
        out_shape=jax.ShapeDtypeStruct((bsz, s, h * dh), BF16),
        scratch_shapes=[
            pltpu.VMEM((hp, nb, blk, dh), BF16),
            pltpu.VMEM((hp, nb, dh + MOBA_ONES_ROWS, blk), BF16),
            pltpu.VMEM((hp, nb, dh), F32),
            pltpu.VMEM((hp, nb, V7X_SUBLANES, blk), F32),
        ],
        compiler_params=_cparams(cfg, "parallel", "parallel", "arbitrary"),
        name="moba_attention",
    )(proj, proj, proj, cosf, sinf, cosf, sinf)


def _ret_kernel(q_ref, k_ref, v_ref, g_ref, cos_ref, sin_ref, dm_ref, qd_ref, kd_ref, cd_ref,
                o_ref, st_ref, *, c, nsub, dk, dv, hp):
    @pl.when(pl.program_id(2) == 0)
    def _():
        st_ref[...] = jnp.zeros_like(st_ref)

    half = dk // 2
    heads = range(hp)
    each = lambda f, *ls: [f(*xs) for xs in zip(*ls)]
    bf = lambda x: x.astype(BF16)
    dm, qd, kd, cd = ([ref[n] for n in heads] for ref in (dm_ref, qd_ref, kd_ref, cd_ref))
    for sidx in range(nsub):
        rows = slice(sidx * c, (sidx + 1) * c)
        cos = cos_ref[rows, :]
        sin = sin_ref[rows, :]

        def rope(x):
            x1, x2 = x[:, :half], x[:, half:]
            return jnp.concatenate([x1 * cos - x2 * sin, x2 * cos + x1 * sin], axis=1)

        q = [rope(q_ref[rows, n * dk:(n + 1) * dk]) for n in heads]
        k = [rope(k_ref[rows, n * dk:(n + 1) * dk]) * (dk ** -0.5) for n in heads]
        qb, kb = each(bf, q), each(bf, k)
        vb = [bf(v_ref[rows, n * dv:(n + 1) * dv]) for n in heads]
        st = [st_ref[n] for n in heads]
        inner = each(lambda x, y, d: bf(_dot_nt(x, y) * d), qb, kb, dm)
        cross = each(lambda x, s, d: _dot(x, bf(s)) * d, qb, st, qd)
        o = each(lambda i, v, x: _dot(i, v) + x, inner, vb, cross)
        kdb = each(lambda x, d: bf(x * d), k, kd)
        new_st = each(lambda s, d, x, v: s * d + _dot_tn(x, v), st, cd, kdb, vb)
        for n in heads:
            st_ref[n] = new_st[n]
        on = each(lambda x: x * lax.rsqrt(jnp.mean(x * x, axis=-1, keepdims=True) + NORM_EPS), o)
        for n in heads:
            cols = slice(n * dv, (n + 1) * dv)
            o_ref[rows, cols] = (on[n] * _silu(g_ref[rows, cols])).astype(o_ref.dtype)


def _retention(cfg, proj):
    bsz, s, _ = proj.shape
    h, dk, dv, c = cfg.ret_heads, cfg.ret_key_dim, cfg.ret_val_dim, cfg.ret_chunk
    ts = min(cfg.ret_rows, s)
    assert s % ts == 0 and ts % c == 0
    q0 = 3 * cfg.dm // dk
    k0 = (3 * cfg.dm + cfg.dk) // dk
    v0 = (3 * cfg.dm + 2 * cfg.dk) // dv
    g0 = (3 * cfg.dm + 2 * cfg.dk + cfg.dv) // dv
    assert (3 * cfg.dm) % dk == 0 and (3 * cfg.dm + 2 * cfg.dk) % dv == 0
    cos, sin = _rope_tables(s, dk)
    log_g = jnp.log1p(-jnp.exp2(-5.0 - jnp.arange(h, dtype=F32)))
    idx = jnp.arange(c, dtype=F32)
    diff = idx[:, None] - idx[None, :]
    dmask = jnp.where(diff >= 0, jnp.exp(jnp.maximum(diff, 0.0) * log_g[:, None, None]), 0.0)
    qdec = jnp.exp((idx + 1.0) * log_g[:, None])[..., None]
    kdec = jnp.exp((c - 1.0 - idx) * log_g[:, None])[..., None]
    cdec = jnp.broadcast_to(jnp.exp(c * log_g)[:, None, None], (h, 1, dv))
    hp = math.gcd(cfg.ret_heads_per_step, h)
    assert q0 % hp == 0 and k0 % hp == 0 and v0 % hp == 0 and g0 % hp == 0
    rowspec = lambda w, c0: pl.BlockSpec((None, ts, hp * w), lambda b, hh, i: (b, i, c0 // hp + hh))
    tab = pl.BlockSpec((ts, dk // 2), lambda b, hh, i: (i, 0))
    const = lambda r, w: pl.BlockSpec((hp, r, w), lambda b, hh, i: (hh, 0, 0))
    return pl.pallas_call(
        functools.partial(_ret_kernel, c=c, nsub=ts // c, dk=dk, dv=dv, hp=hp),
        grid=(bsz, h // hp, s // ts),
        in_specs=[
            rowspec(dk, q0), rowspec(dk, k0), rowspec(dv, v0), rowspec(dv, g0), tab, tab,
            const(c, c), const(c, 1), const(c, 1), const(1, dv),
        ],
        out_specs=pl.BlockSpec((None, ts, hp * dv), lambda b, hh, i: (b, i, hh)),
        out_shape=jax.ShapeDtypeStruct((bsz, s, h * dv), BF16),
        scratch_shapes=[pltpu.VMEM((hp, dk, dv), F32)],
        compiler_params=_cparams(cfg, "parallel", "parallel", "arbitrary"),
        name="retention",
    )(proj, proj, proj, proj, cos, sin, dmask, qdec, kdec, cdec)


CONV_HALO = 32


def _conv_kernel(a_ref, g_ref, w_ref, b_ref, o_ref, buf_ref, sh_ref, *, ts, kw, rc):
    @pl.when(pl.program_id(2) == 0)
    def _():
        buf_ref[0:CONV_HALO, :] = jnp.zeros((CONV_HALO, buf_ref.shape[1]), F32)

    buf_ref[CONV_HALO:CONV_HALO + ts, :] = a_ref[...] * jax.nn.sigmoid(g_ref[...])
    nshift = CONV_HALO + ts - V7X_SUBLANES
    for ph in range(1, V7X_SUBLANES):
        sh_ref[ph, 0:nshift, :] = buf_ref[ph:ph + nshift, :]
    off = CONV_HALO - (kw - 1)
    bias = b_ref[...]
    for r0 in range(0, ts, rc):
        acc = jnp.broadcast_to(bias, (rc, bias.shape[1]))
        for j in range(kw):
            ph = (off + j) % V7X_SUBLANES
            base = off + j - ph + r0
            rows = buf_ref[base:base + rc, :] if ph == 0 else sh_ref[ph, base:base + rc, :]
            acc = acc + w_ref[j:j + 1, :] * rows
        o_ref[r0:r0 + rc, :] = acc
    buf_ref[0:CONV_HALO, :] = buf_ref[ts:ts + CONV_HALO, :]


def _conv_glu(cfg, proj, conv_w, conv_b):
    bsz, s, _ = proj.shape
    ch, kw = cfg.conv_ch, cfg.conv_width
    ts, tc = min(cfg.conv_rows, s), min(cfg.conv_cols, ch)
    assert kw - 1 <= CONV_HALO <= ts and s % ts == 0 and ch % tc == 0
    nct = ch // tc
    wp = jnp.zeros((CONV_HALO, ch), F32).at[:kw].set(conv_w)
    return pl.pallas_call(
        functools.partial(_conv_kernel, ts=ts, kw=kw, rc=64),
        grid=(bsz, nct, s // ts),
        in_specs=[
            pl.BlockSpec((None, ts, tc), lambda b, c, i: (b, i, c)),
            pl.BlockSpec((None, ts, tc), lambda b, c, i: (b, i, nct + c)),
            pl.BlockSpec((CONV_HALO, tc), lambda b, c, i: (0, c)),
            pl.BlockSpec((1, tc), lambda b, c, i: (0, c)),
        ],
        out_specs=pl.BlockSpec((None, ts, tc), lambda b, c, i: (b, i, c)),
        out_shape=jax.ShapeDtypeStruct((bsz, s, ch), F32),
        scratch_shapes=[pltpu.VMEM((CONV_HALO + ts, tc), F32),
                        pltpu.VMEM((V7X_SUBLANES, CONV_HALO + ts, tc), F32)],
        compiler_params=_cparams(cfg, "parallel", "parallel", "arbitrary"),
        name="glu_causal_conv",
    )(proj, proj, wp, conv_b.reshape(1, ch))


def _ln_silu_kernel(x_ref, g_ref, b_ref, o_ref):
    x = x_ref[...]
    mu = jnp.mean(x, axis=-1, keepdims=True)
    d = x - mu
    var = jnp.mean(d * d, axis=-1, keepdims=True)
    y = d * lax.rsqrt(var + CONV_LN_EPS) * g_ref[...] + b_ref[...]
    o_ref[...] = _silu(y).astype(o_ref.dtype)


def _ln_silu(cfg, x, g, b):
    bsz, s, d = x.shape
    ts = min(cfg.row_tile, s)
    row = pl.BlockSpec((None, ts, d), lambda bb, i: (bb, i, 0))
    vec = pl.BlockSpec((1, d), lambda bb, i: (0, 0))
    return pl.pallas_call(
        _ln_silu_kernel,
        grid=(bsz, s // ts),
        in_specs=[row, vec, vec],
        out_specs=row,
        out_shape=jax.ShapeDtypeStruct((bsz, s, d), BF16),
        compiler_params=_cparams(cfg, "parallel", "parallel"),
        name="layernorm_swish",
    )(x, g.reshape(1, d), b.reshape(1, d))


def _group_ones(n, group):
    r = lax.broadcasted_iota(jnp.int32, (n, n), 0)
    c = lax.broadcasted_iota(jnp.int32, (n, n), 1)
    shift = int(math.log2(group))
    return jnp.where((r >> shift) == (c >> shift), 1.0, 0.0).astype(BF16)


def _split2(x):
    hi = x.astype(BF16)
    return hi, (x - hi.astype(F32)).astype(BF16)


def _dot_split(x, w):
    xh, xl = _split2(x)
    wh, wl = _split2(w)
    return _dot(xh, wh) + (_dot(xh, wl) + _dot(xl, wh))


def _group_sum(x, gmat):
    n = x.shape[1]
    hi, lo = _split2(x)
    parts = [_dot(hi[:, s0:s0 + V7X_LANES], gmat) + _dot(lo[:, s0:s0 + V7X_LANES], gmat)
             for s0 in range(0, n, V7X_LANES)]
    return parts[0] if len(parts) == 1 else jnp.concatenate(parts, axis=1)


def _rwkv_pre_kernel(r_ref, k_ref, v_ref, lo_ref, mur_ref, muk_ref, muv_ref, mul_ref,
                     w0_ref, wup_ref, a0_ref, aup_ref, gup_ref, kk_ref, ka_ref,
                     ro_ref, lw_ref, ko_ref, vo_ref, ao_ref, bo_ref, go_ref,
                     lr_ref, lk_ref, lv_ref, ll_ref, *, ts, hd, lw_pad):
    first = pl.program_id(1) == 0

    def shift(x_ref, last_ref, mu_ref):
        @pl.when(first)
        def _():
            last_ref[...] = jnp.zeros_like(last_ref)

        x = x_ref[...]
        row = lax.broadcasted_iota(jnp.int32, x.shape, 0)
        prev = jnp.where(row == 0, last_ref[V7X_SUBLANES - 1:V7X_SUBLANES, :], pltpu.roll(x, 1, 0))
        last_ref[...] = x[ts - V7X_SUBLANES:ts, :]
        return x + (prev - x) * mu_ref[...]

    r = shift(r_ref, lr_ref, mur_ref)
    k = shift(k_ref, lk_ref, muk_ref)
    v = shift(v_ref, lv_ref, muv_ref)
    lo = shift(lo_ref, ll_ref, mul_ref)
    xw, xa, xg = lo[:, :lw_pad], lo[:, lw_pad:2 * lw_pad], lo[:, 2 * lw_pad:]

    z = w0_ref[...] + _dot_split(jnp.tanh(xw), wup_ref[...])
    lw_ref[...] = -math.exp(-0.5) * jax.nn.sigmoid(z)
    a = jax.nn.sigmoid(a0_ref[...] + _dot_split(xa, aup_ref[...]))
    go_ref[...] = _dot(jax.nn.sigmoid(xg).astype(BF16), gup_ref[...].astype(BF16))

    kkr = k * kk_ref[...]
    ss = _group_sum(kkr * kkr, _group_ones(V7X_LANES, hd))
    kk = kkr * lax.rsqrt(jnp.maximum(ss, 1e-24))
    ro_ref[...] = r
    vo_ref[...] = v
    ko_ref[...] = k * (1.0 + (a - 1.0) * ka_ref[...])
    ao_ref[...] = -kk
    bo_ref[...] = kk * a


def _rwkv_pre(cfg, proj, lora, mu, w0, w_up, a0, a_up, g_up, k_k, k_a):
    bsz, s, _ = proj.shape
    d = cfg.rwkv_dim
    ts = min(cfg.pre_rows, s)
    lw_pad = V7X_LANES
    assert cfg.decay_lora <= lw_pad and cfg.iclr_lora <= lw_pad and (2 * cfg.conv_ch) % d == 0
    lo_w = lora.shape[2]
    c0 = 2 * cfg.conv_ch // d
    pad_rows = lambda w: jnp.zeros((lw_pad, d), F32).at[:w.shape[0]].set(w)
    pad_vec = lambda vv, n: jnp.zeros((1, n), F32).at[0, :vv.shape[0]].set(vv)
    mu_r, mu_k, mu_v = (mu[i * d:(i + 1) * d].reshape(1, d) for i in range(3))
    o = 3 * d
    mu_l = jnp.concatenate([
        pad_vec(mu[o:o + cfg.decay_lora], lw_pad),
        pad_vec(mu[o + cfg.decay_lora:o + cfg.decay_lora + cfg.iclr_lora], lw_pad),
        mu[o + cfg.decay_lora + cfg.iclr_lora:].reshape(1, -1)], axis=1)
    row = lambda cb: pl.BlockSpec((None, ts, d), lambda b, i: (b, i, cb))
    lrow = pl.BlockSpec((None, ts, lo_w), lambda b, i: (b, i, 0))
    vec = lambda n: pl.BlockSpec((1, n), lambda b, i: (0, 0))
    mat = lambda rws: pl.BlockSpec((rws, d), lambda b, i: (0, 0))
    orow = pl.BlockSpec((None, ts, d), lambda b, i: (b, i, 0))
    return pl.pallas_call(
        functools.partial(_rwkv_pre_kernel, ts=ts, hd=cfg.rwkv_head_dim, lw_pad=lw_pad),
        grid=(bsz, s // ts),
        in_specs=[row(c0), row(c0 + 1), row(c0 + 2), lrow, vec(d), vec(d), vec(d), vec(lo_w),
                  vec(d), mat(lw_pad), vec(d), mat(lw_pad), mat(cfg.gate_lora), vec(d), vec(d)],
        out_specs=[orow] * 7,
        out_shape=[jax.ShapeDtypeStruct((bsz, s, d), F32)] * 7,
        scratch_shapes=[pltpu.VMEM((V7X_SUBLANES, d), F32)] * 3 + [pltpu.VMEM((V7X_SUBLANES, lo_w), F32)],
        compiler_params=_cparams(cfg, "parallel", "arbitrary"),
        name="rwkv_token_shift_lora",
    )(proj, proj, proj, lora, mu_r, mu_k, mu_v, mu_l, w0.reshape(1, d), pad_rows(w_up),
      a0.reshape(1, d), pad_rows(a_up), g_up, k_k.reshape(1, d), k_a.reshape(1, d))


def _scan_chunk(r, lw, k, v, a, b, st, consts):
    tri, strict_bd, incl_bd, eye, lane_a, bd = consts
    L = r[0].shape[0]
    each = lambda f, *ls: [f(*xs) for xs in zip(*ls)]
    bf = lambda x: x.astype(BF16)
    stack = lambda x: jnp.concatenate([x, x], axis=0)
    unstack = lambda x: jnp.where(lane_a, x[:L], x[L:])
    left, right = (lambda x: x[:, :V7X_LANES]), (lambda x: x[:, V7X_LANES:])

    cs = each(lambda x: _dot(tri, jnp.concatenate(_split2(x), axis=1)), lw)
    cum = each(lambda x: left(x) + right(x), cs)
    cl = each(lambda x: x[L - 1:L, :], cum)
    tail = each(lambda x, y: jnp.exp(x - y), cl, cum)
    at = each(lambda x, c, w: x * jnp.exp(c - w), a, cum, lw)
    rt = each(lambda x, c: x * jnp.exp(c), r, cum)
    g_inv = each(lambda c: jnp.exp(-c), cum)
    bk = each(lambda x, y, g: jnp.concatenate([stack(bf(x * g)), stack(bf(y * g))], axis=0), b, k, g_inv)
    heads2 = lambda x: jnp.concatenate([bf(jnp.where(lane_a, x, 0.0)), bf(jnp.where(lane_a, 0.0, x))],
                                       axis=0)
    xa = each(lambda x, y: _dot_nt(heads2(x), y), at, bk)
    xr = each(lambda x, y: _dot_nt(heads2(x), y), rt, bk)
    n = each(lambda x: jnp.where(strict_bd, left(x), 0.0), xa)
    m = each(lambda x: bf(jnp.where(strict_bd, right(x), 0.0)), xa)
    p = each(lambda x: bf(jnp.where(incl_bd, left(x), 0.0)), xr)
    q = each(lambda x: bf(jnp.where(incl_bd, right(x), 0.0)), xr)
    vb = each(bf, v)
    v_st = each(stack, vb)
    mv = each(_dot, m, v_st)
    w = each(lambda x: eye + x, n)
    pw = each(bf, n)
    for _ in range(int(math.log2(L)) - 1):
        pw = each(lambda x: bf(_dot(x, x)), pw)
        w = each(lambda x, y: x + _dot(bf(x), y), w, pw)
    au = each(lambda ww, x, y: _dot(bf(ww), jnp.concatenate([stack(bf(x)), bf(y)], axis=1)),
              w, at, mv)
    pau = each(lambda x, y: _dot(x, bf(y)), p, au)
    qv = each(_dot, q, v_st)
    rbar = each(lambda x, y: bf(x + unstack(left(y))), rt, pau)
    ybar = each(lambda x, y: unstack(right(x) + y), pau, qv)
    bh = each(lambda x, t: stack(bf(x * t)), b, tail)
    kh = each(lambda x, t: bf(x * t), k, tail)
    abar = each(lambda x: bf(jnp.where(bd, left(x), 0.0)), au)
    ubar = each(lambda x: bf(jnp.where(bd, right(x), 0.0)), au)
    tt = each(lambda x, y: bf(jnp.where(bd, _dot_tn(x, y), 0.0)), abar, bh)
    z = each(lambda u, vv, x, y: jnp.where(bd, _dot_tn(jnp.concatenate([u, vv], axis=0),
                                                       jnp.concatenate([x, y], axis=0)), 0.0),
             ubar, vb, bh, kh)
    sb = each(bf, st)
    y = each(lambda x, s, yb: _dot_nt(x, s) + yb, rbar, sb, ybar)
    st_new = each(lambda s, c, s16, t, zz: s * jnp.exp(c) + _dot(s16, t) + zz, st, cl, sb, tt, z)
    return y, st_new


def _scan_kernel(r_ref, lw_ref, k_ref, v_ref, a_ref, b_ref, y_ref, st_ref, *, ts, L, pairs, hd):
    @pl.when(pl.program_id(2) == 0)
    def _():
        st_ref[...] = jnp.zeros_like(st_ref)

    ri = lax.broadcasted_iota(jnp.int32, (L, L), 0)
    ci = lax.broadcasted_iota(jnp.int32, (L, L), 1)
    lane = lax.broadcasted_iota(jnp.int32, (1, V7X_LANES), 1)
    r2 = lax.broadcasted_iota(jnp.int32, (2 * L, 2 * L), 0)
    c2 = lax.broadcasted_iota(jnp.int32, (2 * L, 2 * L), 1)
    bd = (r2 < L) == (c2 < L)
    rl, cl2 = r2 & (L - 1), c2 & (L - 1)
    consts = (jnp.where(ri >= ci, 1.0, 0.0).astype(BF16), bd & (rl > cl2), bd & (rl >= cl2),
              jnp.where(r2 == c2, 1.0, 0.0), lane < hd, bd)

    def body(c, carry):
        rows = pl.ds(pl.multiple_of(c * L, L), L)
        cols = [slice(p * V7X_LANES, (p + 1) * V7X_LANES) for p in range(pairs)]
        load = lambda ref: [ref[rows, cs] for cs in cols]
        ys, sts = _scan_chunk(load(r_ref), load(lw_ref), load(k_ref), load(v_ref), load(a_ref),
                              load(b_ref), [st_ref[p] for p in range(pairs)], consts)
        for p in range(pairs):
            y_ref[rows, cols[p]] = ys[p]
            st_ref[p] = sts[p]
        return carry

    lax.fori_loop(0, ts // L, body, 0)


def _rwkv_scan(cfg, r, lw, k, v, a, b):
    bsz, s, d = r.shape
    hd = cfg.rwkv_head_dim
    assert 2 * hd == V7X_LANES
    ts, L = min(cfg.scan_rows, s), cfg.scan_chunk
    npairs = d // V7X_LANES
    pairs = min(cfg.scan_pairs, npairs)
    assert s % ts == 0 and ts % L == 0 and npairs % pairs == 0
    blk = pl.BlockSpec((None, ts, pairs * V7X_LANES), lambda bb, p, i: (bb, i, p))
    return pl.pallas_call(
        functools.partial(_scan_kernel, ts=ts, L=L, pairs=pairs, hd=hd),
        grid=(bsz, npairs // pairs, s // ts),
        in_specs=[blk] * 6,
        out_specs=blk,
        out_shape=jax.ShapeDtypeStruct((bsz, s, d), F32),
        scratch_shapes=[pltpu.VMEM((pairs, V7X_LANES, V7X_LANES), F32)],
        compiler_params=_cparams(cfg, "parallel", "parallel", "arbitrary"),
        name="rwkv7_scan",
    )(r, lw, k, v, a, b)


def _rwkv_post_kernel(y_ref, r_ref, k_ref, v_ref, g_ref, rk_ref, lg_ref, lb_ref, o_ref, *, hd):
    gmat = _group_ones(V7X_LANES, hd)
    y = y_ref[...]
    mu = _group_sum(y, gmat) * (1.0 / hd)
    d = y - mu
    var = _group_sum(d * d, gmat) * (1.0 / hd)
    yn = d * lax.rsqrt(var + RWKV_LNX_EPS) * lg_ref[...] + lb_ref[...]
    bonus = _group_sum(r_ref[...] * k_ref[...] * rk_ref[...], gmat) * v_ref[...]
    o_ref[...] = ((yn + bonus) * g_ref[...]).astype(o_ref.dtype)


def _rwkv_post(cfg, y, r, k, v, g, r_k, lnx_g, lnx_b):
    bsz, s, d = y.shape
    ts = min(cfg.row_tile, s)
    row = pl.BlockSpec((None, ts, d), lambda b, i: (b, i, 0))
    vec = pl.BlockSpec((1, d), lambda b, i: (0, 0))
    return pl.pallas_call(
        functools.partial(_rwkv_post_kernel, hd=cfg.rwkv_head_dim),
        grid=(bsz, s // ts),
        in_specs=[row] * 5 + [vec] * 3,
        out_specs=row,
        out_shape=jax.ShapeDtypeStruct((bsz, s, d), BF16),
        compiler_params=_cparams(cfg, "parallel", "parallel"),
        name="rwkv_groupnorm_gate",
    )(y, r, k, v, g, r_k.reshape(1, d), lnx_g.reshape(1, d), lnx_b.reshape(1, d))


def _even_mixer(cfg, h, w_in, w_out):
    bsz, s, d = h.shape
    proj = _matmul(cfg, [h.reshape(bsz * s, d)], w_in, 0, cfg.even_in, F32,
                   cfg.mm_tm, cfg.mm_tn, single_buffer_x=True).reshape(bsz, s, cfg.even_in)
    o_m = _moba(cfg, proj).reshape(bsz * s, cfg.dm)
    o_r = _retention(cfg, proj).reshape(bsz * s, cfg.dv)
    return _matmul(cfg, [o_m, o_r], w_out, 0, d, BF16, cfg.mm_tm // 2,
                   cfg.mm_tn // 2).reshape(bsz, s, d)


def _odd_mixer(cfg, h, w_in, w_out, conv_w, conv_b, conv_ln_g, conv_ln_b, mu, w0, w_up, a0, a_up,
               g_up, k_k, k_a, r_k, lnx_g, lnx_b):
    bsz, s, d = h.shape
    h2 = h.reshape(bsz * s, d)
    w_nk = jnp.swapaxes(w_in, 1, 2)
    proj = _matmul(cfg, [h2], w_nk, 0, cfg.odd_main, F32,
                   cfg.mm_tm, cfg.mm_tn, w_is_nk=True, single_buffer_x=True).reshape(bsz, s, cfg.odd_main)
    lw_pad = V7X_LANES
    wl = w_nk[0, cfg.odd_main:]
    zr = lambda n: jnp.zeros((n, d), F32)
    o1, o2 = cfg.decay_lora, cfg.decay_lora + cfg.iclr_lora
    wl = jnp.concatenate([wl[:o1], zr(lw_pad - cfg.decay_lora), wl[o1:o2],
                          zr(lw_pad - cfg.iclr_lora), wl[o2:]], axis=0)[None]
    lo_w = wl.shape[1]
    lora = _matmul(cfg, [h2], wl, 0, lo_w, F32, cfg.mm_tm // 2, lo_w,
                   w_is_nk=True).reshape(bsz, s, lo_w)

    u = _ln_silu(cfg, _conv_glu(cfg, proj, conv_w, conv_b), conv_ln_g, conv_ln_b)
    r, lw, k, v, a, b, g = _rwkv_pre(cfg, proj, lora, mu, w0, w_up, a0, a_up, g_up, k_k, k_a)
    y = _rwkv_scan(cfg, r, lw, k, v, a, b)
    y = _rwkv_post(cfg, y, r, k, v, g, r_k.reshape(-1), lnx_g, lnx_b)
    return _matmul(cfg, [u.reshape(bsz * s, -1), y.reshape(bsz * s, -1)], w_out, 0, d, BF16,
                   cfg.mm_tm, cfg.mm_tn, single_buffer_x=True).reshape(bsz, s, d)


def _forward(cfg, x, c, w_ada, b_ada, norm_g, w_ffn_in, w_ffn_out, even_w_in, even_w_out, odd_w_in,
             odd_w_out, conv_w, conv_b, conv_ln_g, conv_ln_b, rwkv_mu, rwkv_w0, rwkv_w_up, rwkv_a0,
             rwkv_a_up, rwkv_g_up, rwkv_k_k, rwkv_k_a, rwkv_r_k, rwkv_lnx_g, rwkv_lnx_b):
    bsz, s, d = x.shape
    depth = w_ada.shape[0]
    mods = _modulation(cfg, c, w_ada, b_ada)
    sh_m, sc_m = mods[0, :, 0], mods[0, :, 1]
    h = _norm_mod(cfg, x, norm_g[0, 0], sc_m, sh_m)
    for layer in range(depth):
        g_m, sh_f, sc_f, g_f = (mods[layer, :, i] for i in (2, 3, 4, 5))
        j = layer // 2
        if layer % 2 == 0:
            o = _even_mixer(cfg, h, even_w_in[j:j + 1], even_w_out[j:j + 1])
        else:
            o = _odd_mixer(cfg, h, odd_w_in[j:j + 1], odd_w_out[j:j + 1], conv_w[j], conv_b[j],
                           conv_ln_g[j], conv_ln_b[j], rwkv_mu[j], rwkv_w0[j], rwkv_w_up[j],
                           rwkv_a0[j], rwkv_a_up[j], rwkv_g_up[j], rwkv_k_k[j], rwkv_k_a[j],
                           rwkv_r_k[j], rwkv_lnx_g[j], rwkv_lnx_b[j])
        x, h = _resid(cfg, x, o, norm_g[layer, 1], g_m, (norm_g[layer, 2], sc_f, sh_f))
        act = _ffn_in(cfg, h.reshape(bsz * s, d), w_ffn_in, layer)
        f = _matmul(cfg, [act], w_ffn_out, layer, d, BF16, cfg.ffn_out_tm, cfg.ffn_tn,
                    single_buffer_x=True).reshape(bsz, s, d)
        if layer + 1 < depth:
            nxt = (norm_g[layer + 1, 0], mods[layer + 1, :, 1], mods[layer + 1, :, 0])
            x, h = _resid(cfg, x, f, norm_g[layer, 3], g_f, nxt)
        else:
            x = _resid(cfg, x, f, norm_g[layer, 3], g_f)
    return x


def kernel(x, c, w_ada, b_ada, norm_g, w_ffn_in, w_ffn_out, even_w_in, even_w_out, odd_w_in, odd_w_out, conv_w, conv_b, conv_ln_g, conv_ln_b, rwkv_mu, rwkv_w0, rwkv_w_up, rwkv_a0, rwkv_a_up, rwkv_g_up, rwkv_k_k, rwkv_k_a, rwkv_r_k, rwkv_lnx_g, rwkv_lnx_b):
    return _forward(Config(), x, c, w_ada, b_ada, norm_g, w_ffn_in, w_ffn_out, even_w_in, even_w_out,
                    odd_w_in, odd_w_out, conv_w, conv_b, conv_ln_g, conv_ln_b, rwkv_mu, rwkv_w0,
                    rwkv_w_up, rwkv_a0, rwkv_a_up, rwkv_g_up, rwkv_k_k, rwkv_k_a, rwkv_r_k,
                    rwkv_lnx_g, rwkv_lnx_b)
```
